```python
import math
import jax
import jax.numpy as jnp
from jax import lax
import numpy as np

D_MODEL = 2048
BATCH = 4
SEQ = 4096
DEPTH = 2

D_MIX = D_MODEL
N_GROUPS = 4
GROUP_W = D_MIX // N_GROUPS
ROPE_THETA = 500000.0
NORM_EPS = 1e-6
Q_BLOCK = 128

RW_HEAD = 64
RW_HEADS = GROUP_W // RW_HEAD
RW_DECAY_LORA = 32
RW_A_LORA = 32
RW_V_LORA = 32
RW_GN_EPS = 64e-5
RW_COLS = 3 * GROUP_W + RW_DECAY_LORA + RW_A_LORA

ML_HEADS = 4
ML_HEAD = GROUP_W // ML_HEADS
ML_CONV = 4
ML_CHUNK = 64
ML_COLS = 4 * GROUP_W + 2 * ML_HEADS

MLA_HEADS = 4
MLA_V = GROUP_W // MLA_HEADS
MLA_NOPE = 128
MLA_ROPE = 64
MLA_QK = MLA_NOPE + MLA_ROPE
MLA_Q_LORA = 384
MLA_KV_LORA = 256
MLA_COLS = MLA_Q_LORA + MLA_KV_LORA + MLA_ROPE

DA_HEADS = 4
DA_V = GROUP_W // DA_HEADS
DA_QK = DA_V // 2
DA_ROT = DA_QK // 4
DA_COLS = 3 * GROUP_W

IN_WIDTHS = (D_MIX, RW_COLS, ML_COLS, MLA_COLS, DA_COLS)
N_IN = sum(IN_WIDTHS)

kernel_name = 'hybrid_rwkv7_mlstm_mla_diffattn_parallel_heads'


def _split(a, widths):
    idx = [int(i) for i in np.cumsum(widths)[:-1]]
    return jnp.split(a, idx, axis=-1)


def rms_norm(x, g, eps=NORM_EPS):
    xf = x.astype(jnp.float32)
    return xf * lax.rsqrt(jnp.mean(xf * xf, -1, keepdims=True) + eps) * g.astype(jnp.float32)


def rope_tables(T, rot_dim):
    inv = ROPE_THETA ** (-jnp.arange(0, rot_dim, 2, dtype=jnp.float32) / rot_dim)
    ang = jnp.arange(T, dtype=jnp.float32)[:, None] * inv[None, :]
    return jnp.cos(ang), jnp.sin(ang)


def apply_rope(x, cos, sin):
    half = x.shape[-1] // 2
    extra = x.ndim - 3
    c = cos.reshape((cos.shape[0],) + (1,) * extra + (half,))
    s = sin.reshape((sin.shape[0],) + (1,) * extra + (half,))
    x1, x2 = x[..., :half], x[..., half:]
    return jnp.concatenate([x1 * c - x2 * s, x2 * c + x1 * s], axis=-1)


def _token_shift(p, mu):
    prev = jnp.pad(p, ((0, 0), (1, 0), (0, 0)))[:, :-1]
    return p + (prev - p) * mu


def _causal_dwconv(x, w, b):
    K, C = w.shape
    out = lax.conv_general_dilated(x, w.astype(x.dtype)[:, None, :], window_strides=(1,),
                                   padding=[(K - 1, 0)], dimension_numbers=('NWC', 'WIO', 'NWC'),
                                   feature_group_count=C)
    return out + b.astype(x.dtype)


def causal_attention_blocks(q, k, v, coeffs, scale):
    B, T, H, M, dk = q.shape
    nb = T // Q_BLOCK
    qb = jnp.moveaxis(q.astype(jnp.float32).reshape(B, nb, Q_BLOCK, H, M, dk), 1, 0)
    kf = k.astype(jnp.float32)
    vf = v.astype(jnp.float32)
    kpos = jnp.arange(T)
    neg = jnp.finfo(jnp.float32).min

    def one_block(args):
        q_blk, start = args
        qpos = start + jnp.arange(Q_BLOCK)
        s = jnp.einsum('bqhmd,bkhmd->bhmqk', q_blk, kf) * scale
        s = jnp.where(kpos[None, :] <= qpos[:, None], s, neg)
        p = jax.nn.softmax(s, axis=-1)
        attn = jnp.einsum('m,bhmqk->bhqk', coeffs, p)
        return jnp.einsum('bhqk,bkhv->bqhv', attn, vf)

    out = lax.map(one_block, (qb, jnp.arange(nb) * Q_BLOCK))
    return jnp.moveaxis(out, 0, 1).reshape(B, T, H, v.shape[-1])


def _group_norm(y, w, b, eps):
    H, N = y.shape[-2:]
    mu = jnp.mean(y, -1, keepdims=True)
    var = jnp.mean(jnp.square(y - mu), -1, keepdims=True)
    return (y - mu) * lax.rsqrt(var + eps) * w.reshape(H, N) + b.reshape(H, N)


def _wkv7_scan(r, w, k, v, kk, a):
    B, T, H, N = r.shape

    def step(S, inp):
        r_t, w_t, k_t, v_t, kk_t, a_t = inp
        sa = jnp.einsum('bhvk,bhk->bhv', S, -kk_t)
        S = S * w_t[:, :, None, :] + sa[..., None] * (kk_t * a_t)[:, :, None, :] + v_t[..., None] * k_t[:, :, None, :]
        return S, jnp.einsum('bhvk,bhk->bhv', S, r_t)

    xs = tuple(jnp.moveaxis(t, 1, 0) for t in (r, w, k, v, kk, a))
    _, y = lax.scan(step, jnp.zeros((B, H, N, N), jnp.float32), xs)
    return jnp.moveaxis(y, 0, 1)


def rwkv7_group(p, v_first, mu, w0, w_up, a0, a_up, v_res, k_k, k_a, r_k, ln_w, ln_b):
    B, T, _ = p.shape
    p = _token_shift(p.astype(jnp.float32), mu)
    r, k, v, wd, ad = _split(p, (GROUP_W, GROUP_W, GROUP_W, RW_DECAY_LORA, RW_A_LORA))
    decay = jnp.exp(-jnp.exp(-jax.nn.softplus(-(w0 + jnp.tanh(wd) @ w_up)) - 0.5))
    a = jax.nn.sigmoid(a0 + ad @ a_up)
    if v_res is None:
        v_first = v
    else:
        v0, v_dn, v_up = v_res
        v = v + (v_first - v) * jax.nn.sigmoid(v0 + (v @ v_dn) @ v_up)

    def hd(t):
        return t.reshape(B, T, RW_HEADS, RW_HEAD)

    kk = hd(k * k_k)
    kk = kk / jnp.maximum(jnp.sqrt(jnp.sum(kk * kk, -1, keepdims=True)), 1e-12)
    k = k * (1.0 + (a - 1.0) * k_a)
    r, k, v, decay, a = hd(r), hd(k), hd(v), hd(decay), hd(a)
    y = _wkv7_scan(r, decay, k, v, kk, a)
    y = _group_norm(y, ln_w, ln_b, RW_GN_EPS) + jnp.sum(r * k * r_k, -1, keepdims=True) * v
    return y.reshape(B, T, GROUP_W), v_first


def _mlstm_chunkwise(q, k, v, i_pre, log_f):
    B, H, T, dk = q.shape
    dv = v.shape[-1]
    L = ML_CHUNK
    NC = T // L
    q = q.reshape(B, H, NC, L, dk)
    k = k.reshape(B, H, NC, L, dk)
    v = v.reshape(B, H, NC, L, dv)
    ig = i_pre.reshape(B, H, NC, L)
    b = jnp.cumsum(log_f.reshape(B, H, NC, L), axis=-1)
    b_last = b[..., -1]
    g = b_last[..., None] - b + ig
    g_max = jnp.max(g, axis=-1)
    wgt = jnp.exp(g - g_max[..., None])
    C_chunk = jnp.einsum('bhcs,bhcsv,bhcsk->bhcvk', wgt, v, k)
    n_chunk = jnp.einsum('bhcs,bhcsk->bhck', wgt, k)

    def step(carry, inp):
        C, n, m = carry
        bl, gm, Cc, nc = inp
        m_new = jnp.maximum(bl + m, gm)
        a_old = jnp.exp(bl + m - m_new)
        a_new = jnp.exp(gm - m_new)
        C_new = a_old[..., None, None] * C + a_new[..., None, None] * Cc
        n_new = a_old[..., None] * n + a_new[..., None] * nc
        return (C_new, n_new, m_new), (C, n, m)

    init = (jnp.zeros((B, H, dv, dk), jnp.float32), jnp.zeros((B, H, dk), jnp.float32),
            jnp.zeros((B, H), jnp.float32))
    xs = (jnp.moveaxis(b_last, 2, 0), jnp.moveaxis(g_max, 2, 0),
          jnp.moveaxis(C_chunk, 2, 0), jnp.moveaxis(n_chunk, 2, 0))
    _, (C_prev, n_prev, m_prev) = lax.scan(step, init, xs)
    C_prev = jnp.moveaxis(C_prev, 0, 2)
    n_prev = jnp.moveaxis(n_prev, 0, 2)
    m_prev = jnp.moveaxis(m_prev, 0, 2)

    mask = jnp.tril(jnp.ones((L, L), dtype=bool))
    D = b[..., :, None] - b[..., None, :] + ig[..., None, :]
    D = jnp.where(mask, D, -jnp.inf)
    inter_log = b + m_prev[..., None]
    m_j = jnp.maximum(inter_log, jnp.max(D, axis=-1))
    s = jnp.einsum('bhcjd,bhcsd->bhcjs', q, k) * jnp.exp(D - m_j[..., None])
    inter_w = jnp.exp(inter_log - m_j)
    num = jnp.einsum('bhcjs,bhcsv->bhcjv', s, v) + inter_w[..., None] * jnp.einsum('bhcvk,bhcjk->bhcjv', C_prev, q)
    den = jnp.sum(s, axis=-1) + inter_w * jnp.einsum('bhck,bhcjk->bhcj', n_prev, q)
    h = num / jnp.maximum(jnp.abs(den), jnp.exp(-m_j))[..., None]
    return h.reshape(B, H, T, dv)


def mlstm_group(p, conv_w, conv_b, i_b, f_b, norm_g):
    B, T, _ = p.shape
    qk, v, o, ig, fg = _split(p.astype(jnp.float32), (2 * GROUP_W, GROUP_W, GROUP_W, ML_HEADS, ML_HEADS))
    qk = jax.nn.silu(_causal_dwconv(qk, conv_w, conv_b))
    q, k = _split(qk, (GROUP_W, GROUP_W))

    def hd(t):
        return t.reshape(B, T, ML_HEADS, ML_HEAD).transpose(0, 2, 1, 3)

    i_pre = jnp.transpose(ig + i_b, (0, 2, 1))
    log_f = jnp.transpose(jax.nn.log_sigmoid(fg + f_b), (0, 2, 1))
    h = _mlstm_chunkwise(hd(q), hd(k) * ML_HEAD ** -0.5, hd(v), i_pre, log_f)
    h = rms_norm(h.transpose(0, 2, 1, 3), norm_g.reshape(ML_HEADS, ML_HEAD))
    h = jax.nn.sigmoid(o).reshape(B, T, ML_HEADS, ML_HEAD) * h
    return h.reshape(B, T, GROUP_W)


def mla_group(p, cos, sin, q_norm_g, kv_norm_g, w_q_b, w_kv_b, q_g, k_g):
    B, T, _ = p.shape
    q_lat, kv_lat, k_pe = _split(p.astype(jnp.float32), (MLA_Q_LORA, MLA_KV_LORA, MLA_ROPE))
    q = (rms_norm(q_lat, q_norm_g) @ w_q_b).reshape(B, T, MLA_HEADS, MLA_QK)
    kv = (rms_norm(kv_lat, kv_norm_g) @ w_kv_b).reshape(B, T, MLA_HEADS, MLA_NOPE + MLA_V)
    k_nope, v = kv[..., :MLA_NOPE], kv[..., MLA_NOPE:]
    k = jnp.concatenate([k_nope, jnp.broadcast_to(k_pe[:, :, None, :], (B, T, MLA_HEADS, MLA_ROPE))], -1)
    q = rms_norm(q, q_g)
    k = rms_norm(k, k_g)
    q = jnp.concatenate([q[..., :MLA_NOPE], apply_rope(q[..., MLA_NOPE:], cos, sin)], -1)
    k = jnp.concatenate([k[..., :MLA_NOPE], apply_rope(k[..., MLA_NOPE:], cos, sin)], -1)
    o = causal_attention_blocks(q[:, :, :, None, :], k[:, :, :, None, :], v,
                                jnp.ones((1,), jnp.float32), MLA_QK ** -0.5)
    return o.reshape(B, T, GROUP_W)


def diff_attn_group(p, layer, cos, sin, q_g, k_g, lq1, lk1, lq2, lk2, sub_g):
    B, T, _ = p.shape
    q, k, v = _split(p.astype(jnp.float32), (GROUP_W, GROUP_W, GROUP_W))
    q = rms_norm(q.reshape(B, T, DA_HEADS, 2, DA_QK), q_g)
    k = rms_norm(k.reshape(B, T, DA_HEADS, 2, DA_QK), k_g)
    q = jnp.concatenate([apply_rope(q[..., :DA_ROT], cos, sin), q[..., DA_ROT:]], -1)
    k = jnp.concatenate([apply_rope(k[..., :DA_ROT], cos, sin), k[..., DA_ROT:]], -1)
    v = v.reshape(B, T, DA_HEADS, DA_V)
    lam_init = 0.8 - 0.6 * math.exp(-0.3 * layer)
    lam = (jnp.exp(jnp.sum(lq1.astype(jnp.float32) * lk1.astype(jnp.float32)))
           - jnp.exp(jnp.sum(lq2.astype(jnp.float32) * lk2.astype(jnp.float32))) + lam_init)
    coeffs = jnp.stack([jnp.ones((), jnp.float32), -lam])
    o = causal_attention_blocks(q, k, v, coeffs, DA_QK ** -0.5)
    o = rms_norm(o, sub_g) * (1.0 - lam_init)
    return o.reshape(B, T, GROUP_W)


def setup_inputs(seed: int = 0) -> dict:
    key = jax.random.key(seed)
    ks = iter(jax.random.split(key, 48))
    f32 = jnp.float32
    L = DEPTH
    G = GROUP_W

    def nrm(shape, scale):
        return scale * jax.random.normal(next(ks), shape, f32)

    def gain(shape):
        return 1.0 + nrm(shape, 0.02)

    return {
        'x': jax.random.normal(next(ks), (BATCH, SEQ, D_MODEL), f32),
        'norm_g': gain((L, D_MODEL)),
        'w_in': nrm((L, D_MODEL, N_IN), D_MODEL ** -0.5),
        'w_out': nrm((L, D_MIX, D_MODEL), D_MIX ** -0.5),
        'branch_beta': gain((L, N_GROUPS)),
        'rw_mu': jax.random.uniform(next(ks), (L, RW_COLS), f32),
        'rw_w0': jax.random.uniform(next(ks), (L, G), f32, -6.0, 1.0),
        'rw_w_up': nrm((L, RW_DECAY_LORA, G), 0.1 * RW_DECAY_LORA ** -0.5),
        'rw_a0': nrm((L, G), 0.1),
        'rw_a_up': nrm((L, RW_A_LORA, G), 0.1 * RW_A_LORA ** -0.5),
        'rw_v0': 1.0 + nrm((L - 1, G), 0.1),
        'rw_v_dn': nrm((L - 1, G, RW_V_LORA), G ** -0.5),
        'rw_v_up': nrm((L - 1, RW_V_LORA, G), 0.1 * RW_V_LORA ** -0.5),
        'rw_k_k': 0.85 + nrm((L, G), 0.05),
        'rw_k_a': 1.0 + nrm((L, G), 0.05),
        'rw_r_k': nrm((L, RW_HEADS, RW_HEAD), 0.1),
        'rw_ln_w': gain((L, G)),
        'rw_ln_b': nrm((L, G), 0.02),
        'ml_conv_w': nrm((L, ML_CONV, 2 * G), ML_CONV ** -0.5),
        'ml_conv_b': nrm((L, 2 * G), 0.02),
        'ml_i_b': nrm((L, ML_HEADS), 0.1),
        'ml_f_b': jax.random.uniform(next(ks), (L, ML_HEADS), f32, 3.0, 6.0),
        'ml_norm_g': gain((L, G)),
        'mla_q_norm_g': gain((L, MLA_Q_LORA)),
        'mla_kv_norm_g': gain((L, MLA_KV_LORA)),
        'mla_w_q_b': nrm((L, MLA_Q_LORA, MLA_HEADS * MLA_QK), MLA_Q_LORA ** -0.5),
        'mla_w_kv_b': nrm((L, MLA_KV_LORA, MLA_HEADS * (MLA_NOPE + MLA_V)), MLA_KV_LORA ** -0.5),
        'mla_q_g': gain((L, MLA_QK)),
        'mla_k_g': gain((L, MLA_QK)),
        'da_q_g': gain((L, DA_QK)),
        'da_k_g': gain((L, DA_QK)),
        'da_lq1': nrm((L, DA_QK), 0.1),
        'da_lk1': nrm((L, DA_QK), 0.1),
        'da_lq2': nrm((L, DA_QK), 0.1),
        'da_lk2': nrm((L, DA_QK), 0.1),
        'da_sub_g': gain((L, DA_V)),
    }


def reference(x, norm_g, w_in, w_out, branch_beta, rw_mu, rw_w0, rw_w_up, rw_a0, rw_a_up,
              rw_v0, rw_v_dn, rw_v_up, rw_k_k, rw_k_a, rw_r_k, rw_ln_w, rw_ln_b,
              ml_conv_w, ml_conv_b, ml_i_b, ml_f_b, ml_norm_g,
              mla_q_norm_g, mla_kv_norm_g, mla_w_q_b, mla_w_kv_b, mla_q_g, mla_k_g,
              da_q_g, da_k_g, da_lq1, da_lk1, da_lq2, da_lk2, da_sub_g):
    T = x.shape[1]
    cos_mla, sin_mla = rope_tables(T, MLA_ROPE)
    cos_da, sin_da = rope_tables(T, DA_ROT)
    v_first = None
    for l in range(DEPTH):
        h = rms_norm(x, norm_g[l]).astype(x.dtype)
        p = h @ w_in[l]
        z, p_rw, p_ml, p_mla, p_da = _split(p, IN_WIDTHS)
        v_res = None if l == 0 else (rw_v0[l - 1], rw_v_dn[l - 1], rw_v_up[l - 1])
        y_rw, v_first = rwkv7_group(p_rw, v_first, rw_mu[l], rw_w0[l], rw_w_up[l], rw_a0[l], rw_a_up[l],
                                    v_res, rw_k_k[l], rw_k_a[l], rw_r_k[l], rw_ln_w[l], rw_ln_b[l])
        y_ml = mlstm_group(p_ml, ml_conv_w[l], ml_conv_b[l], ml_i_b[l], ml_f_b[l], ml_norm_g[l])
        y_mla = mla_group(p_mla, cos_mla, sin_mla, mla_q_norm_g[l], mla_kv_norm_g[l],
                          mla_w_q_b[l], mla_w_kv_b[l], mla_q_g[l], mla_k_g[l])
        y_da = diff_attn_group(p_da, l, cos_da, sin_da, da_q_g[l], da_k_g[l],
                               da_lq1[l], da_lk1[l], da_lq2[l], da_lk2[l], da_sub_g[l])
        beta = branch_beta[l].astype(jnp.float32)
        y = jnp.concatenate([beta[0] * y_rw, beta[1] * y_ml, beta[2] * y_mla, beta[3] * y_da], axis=-1)
        y = y * jax.nn.silu(z.astype(jnp.float32))
        x = x + (y.astype(x.dtype) @ w_out[l]).astype(x.dtype)
    return x
```

```python
import functools
import math

import jax
import jax.numpy as jnp
from jax import lax
from jax.experimental import pallas as pl
from jax.experimental.pallas import tpu as pltpu

F32 = jnp.float32
BF16 = jnp.bfloat16

D_MODEL = 2048
GROUP_W = 512
NORM_EPS = 1e-6
ROPE_THETA = 500000.0

RW_HEAD = 64
RW_GN_EPS = 64e-5
RW_CHUNK = 64
ML_HEADS = 4
ML_HEAD = 128
ML_CHUNK = 256
MLA_HEADS = 4
MLA_NOPE = 128
MLA_ROPE = 64
MLA_QK = 192
MLA_Q_LORA = 384
MLA_KV_LORA = 256
MLA_PAD = 256
DA_HEADS = 4
DA_QK = 64
DA_ROT = 16
ATT_BLOCK = 512

N_PROJ = 8192
COL_MLA = 0
COL_RW_LORA = 768
COL_ML_GATE = 896
COL_RW = 1024
COL_ML = 2560
COL_DA = 4608
COL_Z = 6144

VMEM_LIMIT = 48 * 1024 * 1024


def _dot(a, b):
    return jnp.dot(a.astype(BF16), b.astype(BF16), preferred_element_type=F32)


def _dot_nt(a, b):
    return lax.dot_general(a.astype(BF16), b.astype(BF16), (((1,), (1,)), ((), ())),
                           preferred_element_type=F32)


def _dot_tn(a, b):
    return lax.dot_general(a.astype(BF16), b.astype(BF16), (((0,), (0,)), ((), ())),
                           preferred_element_type=F32)


def _split3(x):
    h1 = x.astype(BF16)
    r1 = x - h1.astype(F32)
    h2 = r1.astype(BF16)
    h3 = (r1 - h2.astype(F32)).astype(BF16)
    return h1, h2, h3


def _dot_exact_lhs(a_bf16, x):
    h1, h2, h3 = _split3(x)
    f = functools.partial(jnp.dot, preferred_element_type=F32)
    return f(a_bf16, h1) + f(a_bf16, h2) + f(a_bf16, h3)


def _seg_sum(x, ones_bf16):
    hi = x.astype(BF16)
    lo = (x - hi.astype(F32)).astype(BF16)
    f = functools.partial(jnp.dot, preferred_element_type=F32)
    return f(hi, ones_bf16) + f(lo, ones_bf16)


def _sigmoid(x):
    return 1.0 / (1.0 + jnp.exp(-x))


def _log_sigmoid(x):
    return jnp.minimum(x, 0.0) - jnp.log(1.0 + jnp.exp(-jnp.abs(x)))


def _params(*sem):
    return pltpu.CompilerParams(dimension_semantics=sem, vmem_limit_bytes=VMEM_LIMIT)


def _inproj_kernel(x_ref, g_ref, w_ref, o_ref, h_ref):
    @pl.when(pl.program_id(1) == 0)
    def _():
        x = x_ref[...]
        ms = jnp.mean(x * x, axis=-1, keepdims=True)
        h_ref[...] = (x * lax.rsqrt(ms + NORM_EPS) * g_ref[...]).astype(BF16)

    o_ref[...] = jnp.dot(h_ref[...], w_ref[...], preferred_element_type=F32)


def _inproj(x2, g, w, tm=512, tn=1024):
    n = x2.shape[0]
    return pl.pallas_call(
        _inproj_kernel,
        grid=(n // tm, N_PROJ // tn),
        in_specs=[pl.BlockSpec((tm, D_MODEL), lambda i, j: (i, 0)),
                  pl.BlockSpec((1, D_MODEL), lambda i, j: (0, 0)),
                  pl.BlockSpec((D_MODEL, tn), lambda i, j: (0, j))],
        out_specs=pl.BlockSpec((tm, tn), lambda i, j: (i, j)),
        out_shape=jax.ShapeDtypeStruct((n, N_PROJ), F32),
        scratch_shapes=[pltpu.VMEM((tm, D_MODEL), BF16)],
        compiler_params=_params("arbitrary", "arbitrary"),
        name="inproj",
    )(x2, g, w)


def _rw_prep_kernel(has_vres, *refs):
    if has_vres:
        (r_ref, k_ref, v_ref, lo_ref, vf_ref, vec_ref, mulo_ref, wup_ref, aup_ref, vdn_ref, vup_ref,
         ones_ref, ro_ref, ko_ref, vo_ref, lw_ref, kk_ref, a_ref, cr_ref, ck_ref, cv_ref, cl_ref) = refs
    else:
        (r_ref, k_ref, v_ref, lo_ref, vec_ref, mulo_ref, wup_ref, aup_ref,
         ones_ref, ro_ref, ko_ref, vo_ref, lw_ref, kk_ref, a_ref, cr_ref, ck_ref, cv_ref, cl_ref) = refs

    @pl.when(pl.program_id(1) == 0)
    def _():
        cr_ref[...] = jnp.zeros_like(cr_ref)
        ck_ref[...] = jnp.zeros_like(ck_ref)
        cv_ref[...] = jnp.zeros_like(cv_ref)
        cl_ref[...] = jnp.zeros_like(cl_ref)

    def shift(x_ref, carry_ref, mu):
        x = x_ref[...]
        rows = x.shape[0]
        row = lax.broadcasted_iota(jnp.int32, x.shape, 0)
        prev = jnp.where(row == 0, carry_ref[0:1, :], pltpu.roll(x, 1, 0))
        carry_ref[0:1, :] = x[rows - 1:rows, :]
        return x + (prev - x) * mu

    vec = vec_ref[...]
    r = shift(r_ref, cr_ref, vec[0:1])
    k = shift(k_ref, ck_ref, vec[1:2])
    v = shift(v_ref, cv_ref, vec[2:3])
    lo = shift(lo_ref, cl_ref, mulo_ref[0:1])

    u = vec[3:4] + _dot(jnp.tanh(lo), wup_ref[...])
    lw_ref[...] = -math.exp(-0.5) * _sigmoid(u)
    a = _sigmoid(vec[4:5] + _dot(lo, aup_ref[...]))
    if has_vres:
        gate = _sigmoid(vec[7:8] + _dot(_dot(v, vdn_ref[...]), vup_ref[...]))
        v = v + (vf_ref[...] - v) * gate
    kk = k * vec[5:6]
    ss = _seg_sum(kk * kk, ones_ref[...])
    kk = kk / jnp.maximum(jnp.sqrt(ss), 1e-12)
    ro_ref[...] = r
    ko_ref[...] = k * (1.0 + (a - 1.0) * vec[6:7])
    vo_ref[...] = v
    kk_ref[...] = kk
    a_ref[...] = a


def _rw_prep(p_all, v_first, vec, mulo, wup, aup, vdn, vup, ones64, batch, seq, tr=256):
    n = batch * seq
    nt = seq // tr
    has_vres = v_first is not None

    def rows(c):
        return lambda b, i: (b * nt + i, c)

    const = lambda b, i: (0, 0)
    in_specs = [pl.BlockSpec((tr, 512), rows(COL_RW // 512)),
                pl.BlockSpec((tr, 512), rows(COL_RW // 512 + 1)),
                pl.BlockSpec((tr, 512), rows(COL_RW // 512 + 2)),
                pl.BlockSpec((tr, 128), rows(COL_RW_LORA // 128))]
    args = [p_all, p_all, p_all, p_all]
    if has_vres:
        in_specs.append(pl.BlockSpec((tr, 512), rows(0)))
        args.append(v_first)
    in_specs += [pl.BlockSpec((8, 512), const), pl.BlockSpec((8, 128), const),
                 pl.BlockSpec((128, 512), const), pl.BlockSpec((128, 512), const)]
    args += [vec, mulo, wup, aup]
    if has_vres:
        in_specs += [pl.BlockSpec((512, 128), const), pl.BlockSpec((128, 512), const)]
        args += [vdn, vup]
    in_specs.append(pl.BlockSpec((512, 512), const))
    args.append(ones64)
    out = jax.ShapeDtypeStruct((n, 512), F32)
    return pl.pallas_call(
        functools.partial(_rw_prep_kernel, has_vres),
        grid=(batch, nt),
        in_specs=in_specs,
        out_specs=[pl.BlockSpec((tr, 512), rows(0))] * 6,
        out_shape=[out] * 6,
        scratch_shapes=[pltpu.VMEM((8, 512), F32)] * 3 + [pltpu.VMEM((8, 128), F32)],
        compiler_params=_params("arbitrary", "arbitrary"),
        name="rw_prep",
    )(*args)


def _rw_scan_kernel(r_ref, k_ref, v_ref, lw_ref, kk_ref, a_ref, par_ref, ones_ref, y_ref, s_ref):
    L = RW_CHUNK

    @pl.when(pl.program_id(1) == 0)
    def _():
        s_ref[...] = jnp.zeros_like(s_ref)

    row = lax.broadcasted_iota(jnp.int32, (L, 128), 0)
    lane = lax.broadcasted_iota(jnp.int32, (L, 128), 1)
    col = lane & (RW_HEAD - 1)
    strict = row > col
    incl = row >= col
    eye = (row == col).astype(F32)
    first = lane < RW_HEAD
    r128 = lax.broadcasted_iota(jnp.int32, (128, 128), 0)
    c128 = lax.broadcasted_iota(jnp.int32, (128, 128), 1)
    same_head = (r128 < RW_HEAD) == (c128 < RW_HEAD)
    tri = (lax.broadcasted_iota(jnp.int32, (L, L), 0) >= lax.broadcasted_iota(jnp.int32, (L, L), 1)).astype(BF16)

    def bd(x):
        return jnp.concatenate([jnp.where(first, x, 0.0), jnp.where(first, 0.0, x)], axis=0)

    lw = lw_ref[...]
    r = r_ref[...]
    k = k_ref[...]
    v = v_ref[...]
    kk = kk_ref[...]
    a = a_ref[...]

    b = _dot_exact_lhs(tri, lw)
    b_last = b[L - 1:L, :]
    p_inv = jnp.exp(-b)
    p_end = jnp.exp(b_last - b)
    al = -(kk * jnp.exp(b - lw))
    be = kk * a
    bet = be * p_inv
    kt = k * p_inv
    rt = r * jnp.exp(b)
    beh = be * p_end
    kh = k * p_end
    p_chunk = jnp.exp(b_last)

    ys = []
    for pr in range(GROUP_W // 128):
        sl = slice(128 * pr, 128 * pr + 128)
        lhs = jnp.concatenate([al[:, sl], rt[:, sl]], axis=0).astype(BF16)
        rhs = jnp.concatenate([bd(bet[:, sl]), bd(kt[:, sl])], axis=0).astype(BF16)
        g = _dot_nt(lhs, rhs)
        a_ab = jnp.where(strict, g[0:L, 0:128], 0.0)
        a_ak = jnp.where(strict, g[0:L, 128:256], 0.0)
        b_rb = jnp.where(incl, g[L:2 * L, 0:128], 0.0)
        b_rk = jnp.where(incl, g[L:2 * L, 128:256], 0.0)

        t_inv = eye + a_ab
        x = _dot(a_ab, bd(a_ab))
        for _ in range(4):
            xt = _dot(jnp.concatenate([x, t_inv], axis=0), bd(x))
            x = xt[0:L]
            t_inv = t_inv + xt[L:2 * L]
        t_inv = t_inv + _dot(t_inv, bd(x))

        s = s_ref[pr]
        sa = _dot_nt(lhs, s)
        vp = v[:, sl]
        vv = _dot(jnp.concatenate([a_ak, b_rk], axis=0), bd(vp))
        u = _dot(t_inv, bd(sa[0:L] + vv[0:L]))
        y = sa[L:2 * L] + vv[L:2 * L] + _dot(b_rb, bd(u))
        upd = _dot_tn(jnp.concatenate([u, vp], axis=0), jnp.concatenate([beh[:, sl], kh[:, sl]], axis=0))
        s_ref[pr] = s * p_chunk[:, sl] + jnp.where(same_head, upd, 0.0)
        ys.append(y)

    y = jnp.concatenate(ys, axis=1)
    par = par_ref[...]
    ones = ones_ref[...]
    inv_n = 1.0 / RW_HEAD
    mu = _seg_sum(y, ones) * inv_n
    yc = y - mu
    var = _seg_sum(yc * yc, ones) * inv_n
    bonus = _seg_sum(r * k * par[0:1], ones) * v
    y_ref[...] = yc * lax.rsqrt(var + RW_GN_EPS) * par[1:2] + par[2:3] + bonus


def _rw_scan(r, k, v, lw, kk, a, par, ones64, batch, seq):
    n = batch * seq
    nc = seq // RW_CHUNK
    rows = lambda b, c: (b * nc + c, 0)
    const = lambda b, c: (0, 0)
    return pl.pallas_call(
        _rw_scan_kernel,
        grid=(batch, nc),
        in_specs=[pl.BlockSpec((RW_CHUNK, 512), rows)] * 6
        + [pl.BlockSpec((8, 512), const), pl.BlockSpec((512, 512), const)],
        out_specs=pl.BlockSpec((RW_CHUNK, 512), rows),
        out_shape=jax.ShapeDtypeStruct((n, 512), F32),
        scratch_shapes=[pltpu.VMEM((4, 128, 128), F32)],
        compiler_params=_params("arbitrary", "arbitrary"),
        name="rw_scan",
    )(r, k, v, lw, kk, a, par, ones64)


def _ml_kernel(q_ref, k_ref, v_ref, o_ref, g_ref, cwq_ref, cwk_ref, vec_ref, gb_ref, y_ref,
               hq_ref, hk_ref, c_ref, n_ref, m_ref):
    L = ML_CHUNK

    @pl.when(pl.program_id(1) == 0)
    def _():
        hq_ref[...] = jnp.zeros_like(hq_ref)
        hk_ref[...] = jnp.zeros_like(hk_ref)
        c_ref[...] = jnp.zeros_like(c_ref)
        n_ref[...] = jnp.zeros_like(n_ref)
        m_ref[...] = jnp.zeros_like(m_ref)

    vec = vec_ref[...]
    row8 = lax.broadcasted_iota(jnp.int32, (8, GROUP_W), 0)

    def conv(x_ref, halo_ref, w_ref, bias):
        x = x_ref[...]
        w = w_ref[...]
        halo = halo_ref[...]
        acc = x * w[3:4] + bias
        for s in (1, 2, 3):
            xs = pltpu.roll(x, s, 0)
            top = jnp.where(row8 < s, pltpu.roll(halo, s, 0), xs[0:8])
            xs = jnp.concatenate([top, xs[8:]], axis=0)
            acc = acc + xs * w[3 - s:4 - s]
        halo_ref[...] = x[L - 8:L]
        return acc * _sigmoid(acc)

    q = conv(q_ref, hq_ref, cwq_ref, vec[0:1])
    k = conv(k_ref, hk_ref, cwk_ref, vec[1:2]) * (ML_HEAD ** -0.5)
    v = v_ref[...]
    og = _sigmoid(o_ref[...])

    g = g_ref[...] + gb_ref[0:1]
    lane = lax.broadcasted_iota(jnp.int32, (L, 128), 1)
    tri_b = lax.broadcasted_iota(jnp.int32, (L, L), 0) >= lax.broadcasted_iota(jnp.int32, (L, L), 1)
    cum = _dot_exact_lhs(tri_b.astype(BF16), _log_sigmoid(g))
    comb_t = jnp.where(lane < ML_HEADS, g, cum).T

    outs = []
    for h in range(ML_HEADS):
        sl = slice(128 * h, 128 * h + 128)
        qh, kh, vh = q[:, sl], k[:, sl], v[:, sl]
        b_col = cum[:, ML_HEADS + h:ML_HEADS + h + 1]
        ig_col = g[:, h:h + 1]
        b_row = comb_t[ML_HEADS + h:ML_HEADS + h + 1, :]
        ig_row = comb_t[h:h + 1, :]
        dm = jnp.where(tri_b, b_col - b_row + ig_row, -jnp.inf)
        m_prev = m_ref[h:h + 1, 0:1]
        inter = b_col + m_prev
        mj = jnp.maximum(inter, jnp.max(dm, axis=-1, keepdims=True))
        s = _dot_nt(qh, kh) * jnp.exp(dm - mj)
        iw = jnp.exp(inter - mj)
        c_h = c_ref[h]
        n_h = n_ref[h:h + 1, :]
        num = _dot(s, vh) + iw * _dot_nt(qh, c_h)
        den = jnp.sum(s, axis=-1, keepdims=True) + iw * jnp.sum(qh * n_h, axis=-1, keepdims=True)
        hh = num / jnp.maximum(jnp.abs(den), jnp.exp(-mj))

        b_last = b_col[L - 1:L, :]
        g_col = b_last - b_col + ig_col
        m_new = jnp.maximum(b_last + m_prev, jnp.max(g_col, axis=0, keepdims=True))
        a_old = jnp.exp(b_last + m_prev - m_new)
        wg = jnp.exp(g_col - m_new)
        c_ref[h] = a_old * c_h + _dot_tn(vh * wg, kh)
        n_ref[h:h + 1, :] = a_old * n_h + jnp.sum(kh * wg, axis=0, keepdims=True)
        m_ref[h:h + 1, :] = jnp.broadcast_to(m_new, (1, 128))

        hn = hh * lax.rsqrt(jnp.mean(hh * hh, axis=-1, keepdims=True) + NORM_EPS) * vec[2:3, sl]
        outs.append(og[:, sl] * hn)
    y_ref[...] = jnp.concatenate(outs, axis=1)


def _ml(p_all, cwq, cwk, vec, gb, batch, seq):
    n = batch * seq
    nc = seq // ML_CHUNK

    def rows(c):
        return lambda b, i: (b * nc + i, c)

    const = lambda b, i: (0, 0)
    base = COL_ML // 512
    return pl.pallas_call(
        _ml_kernel,
        grid=(batch, nc),
        in_specs=[pl.BlockSpec((ML_CHUNK, 512), rows(base)),
                  pl.BlockSpec((ML_CHUNK, 512), rows(base + 1)),
                  pl.BlockSpec((ML_CHUNK, 512), rows(base + 2)),
                  pl.BlockSpec((ML_CHUNK, 512), rows(base + 3)),
                  pl.BlockSpec((ML_CHUNK, 128), rows(COL_ML_GATE // 128)),
                  pl.BlockSpec((8, 512), const), pl.BlockSpec((8, 512), const),
                  pl.BlockSpec((8, 512), const), pl.BlockSpec((8, 128), const)],
        out_specs=pl.BlockSpec((ML_CHUNK, 512), rows(0)),
        out_shape=jax.ShapeDtypeStruct((n, 512), F32),
        scratch_shapes=[pltpu.VMEM((8, 512), F32), pltpu.VMEM((8, 512), F32),
                        pltpu.VMEM((ML_HEADS, 128, 128), F32), pltpu.VMEM((8, 128), F32),
                        pltpu.VMEM((8, 128), F32)],
        compiler_params=_params("arbitrary", "arbitrary"),
        name="mlstm",
    )(p_all, p_all, p_all, p_all, p_all, cwq, cwk, vec, gb)


def _rope128(x, cos_t, sin_t, half):
    lane = lax.broadcasted_iota(jnp.int32, x.shape, 1)
    seg = 128 if half == MLA_ROPE // 2 else DA_QK
    swapped = jnp.where((lane & (seg - 1)) < half, pltpu.roll(x, 128 - half, 1), pltpu.roll(x, half, 1))
    return x * cos_t + swapped * sin_t


def _mla_prep_kernel(p_ref, cos_ref, sin_ref, vec_ref, wq_ref, wkv_ref, q_ref, k_ref, v_ref):
    p = p_ref[...]
    vec = vec_ref[...]
    cos_t = cos_ref[...]
    sin_t = sin_ref[...]
    q_lat = p[:, 0:MLA_Q_LORA]
    kv_lat = p[:, MLA_Q_LORA:MLA_Q_LORA + MLA_KV_LORA]
    k_pe = p[:, 640:768]
    qn = q_lat * lax.rsqrt(jnp.mean(q_lat * q_lat, axis=-1, keepdims=True) + NORM_EPS) * vec[0:1, 0:MLA_Q_LORA]
    kvn = kv_lat * lax.rsqrt(jnp.mean(kv_lat * kv_lat, axis=-1, keepdims=True) + NORM_EPS) * vec[1:2, 0:MLA_KV_LORA]
    q4 = _dot(qn, wq_ref[...])
    kv4 = _dot(kvn, wkv_ref[...])
    pe_ss = jnp.sum(k_pe * k_pe, axis=-1, keepdims=True)
    qs, ks = [], []
    for h in range(MLA_HEADS):
        qh = q4[:, MLA_PAD * h:MLA_PAD * (h + 1)]
        inv = lax.rsqrt(jnp.sum(qh * qh, axis=-1, keepdims=True) * (1.0 / MLA_QK) + NORM_EPS)
        qh = qh * inv * vec[2:3, MLA_PAD * h:MLA_PAD * (h + 1)]
        qs += [qh[:, 0:128], _rope128(qh[:, 128:256], cos_t, sin_t, MLA_ROPE // 2)]
        kn = kv4[:, 128 * h:128 * (h + 1)]
        inv = lax.rsqrt((jnp.sum(kn * kn, axis=-1, keepdims=True) + pe_ss) * (1.0 / MLA_QK) + NORM_EPS)
        ks += [kn * inv * vec[3:4, 0:128], _rope128(k_pe * inv * vec[4:5, 0:128], cos_t, sin_t, MLA_ROPE // 2)]
    q_ref[...] = jnp.concatenate(qs, axis=1).astype(BF16)
    k_ref[...] = jnp.concatenate(ks, axis=1).astype(BF16)
    v_ref[...] = kv4[:, 512:1024].astype(BF16)


def _mla_prep(p_all, cos_t, sin_t, vec, wq, wkv, batch, seq, tr=256):
    n = batch * seq
    nt = seq // tr
    const = lambda i: (0, 0)
    return pl.pallas_call(
        _mla_prep_kernel,
        grid=(n // tr,),
        in_specs=[pl.BlockSpec((tr, 768), lambda i: (i, COL_MLA // 768)),
                  pl.BlockSpec((tr, 128), lambda i: (i % nt, 0)),
                  pl.BlockSpec((tr, 128), lambda i: (i % nt, 0)),
                  pl.BlockSpec((8, 1024), const),
                  pl.BlockSpec((MLA_Q_LORA, 1024), const),
                  pl.BlockSpec((MLA_KV_LORA, 1024), const)],
        out_specs=[pl.BlockSpec((tr, 1024), lambda i: (i, 0)),
                   pl.BlockSpec((tr, 1024), lambda i: (i, 0)),
                   pl.BlockSpec((tr, 512), lambda i: (i, 0))],
        out_shape=[jax.ShapeDtypeStruct((n, 1024), BF16), jax.ShapeDtypeStruct((n, 1024), BF16),
                   jax.ShapeDtypeStruct((n, 512), BF16)],
        compiler_params=_params("arbitrary"),
        name="mla_prep",
    )(p_all, cos_t, sin_t, vec, wq, wkv)


def _online_softmax_step(s, vb, m_ref, l_ref, acc_ref):
    m_old = m_ref[...]
    m_new = jnp.maximum(m_old, jnp.max(s, axis=-1, keepdims=True))
    p = jnp.exp(s - m_new)
    alpha = jnp.exp(m_old - m_new)
    l_ref[...] = alpha * l_ref[...] + jnp.sum(p, axis=-1, keepdims=True)
    acc_ref[...] = alpha * acc_ref[...] + _dot(p, vb)
    m_ref[...] = m_new


def _mla_attn_kernel(q_ref, k_ref, v_ref, o_ref, m_ref, l_ref, acc_ref):
    tb = ATT_BLOCK
    i = pl.program_id(2)
    q = q_ref[...]
    m_ref[...] = jnp.full_like(m_ref, -jnp.inf)
    l_ref[...] = jnp.zeros_like(l_ref)
    acc_ref[...] = jnp.zeros_like(acc_ref)

    def block(j, masked):
        start = pl.multiple_of(j * tb, tb)
        s = _dot_nt(q, k_ref[pl.ds(start, tb), :])
        if masked:
            keep = lax.broadcasted_iota(jnp.int32, (tb, tb), 1) <= lax.broadcasted_iota(jnp.int32, (tb, tb), 0)
            s = jnp.where(keep, s, -jnp.inf)
        _online_softmax_step(s, v_ref[pl.ds(start, tb), :], m_ref, l_ref, acc_ref)

    def body(j, carry):
        block(j, False)
        return carry

    lax.fori_loop(0, i, body, 0)
    block(i, True)
    o_ref[...] = acc_ref[...] / l_ref[...]


def _mla_attn(q, k, v, batch, seq):
    n = batch * seq
    tb = ATT_BLOCK
    nq = seq // tb
    return pl.pallas_call(
        _mla_attn_kernel,
        grid=(batch, MLA_HEADS, nq),
        in_specs=[pl.BlockSpec((tb, MLA_PAD), lambda b, h, i: (b * nq + i, h)),
                  pl.BlockSpec((seq, MLA_PAD), lambda b, h, i: (b, h)),
                  pl.BlockSpec((seq, 128), lambda b, h, i: (b, h))],
        out_specs=pl.BlockSpec((tb, 128), lambda b, h, i: (b * nq + i, h)),
        out_shape=jax.ShapeDtypeStruct((n, 512), F32),
        scratch_shapes=[pltpu.VMEM((tb, 1), F32), pltpu.VMEM((tb, 1), F32), pltpu.VMEM((tb, 128), F32)],
        compiler_params=_params("arbitrary", "arbitrary", "arbitrary"),
        name="mla_attn",
    )(q, k, v)


def _da_prep_kernel(q_ref, k_ref, v_ref, cos_ref, sin_ref, vec_ref, ones_ref, qo_ref, ko_ref, vo_ref):
    vec = vec_ref[...]
    ones = ones_ref[...]
    cos_t = cos_ref[...]
    sin_t = sin_ref[...]

    def norm_rope(x, gain):
        ss = _seg_sum(x * x, ones) * (1.0 / DA_QK)
        x = x * lax.rsqrt(ss + NORM_EPS) * gain
        parts = [_rope128(x[:, 128 * c:128 * (c + 1)], cos_t, sin_t, DA_ROT // 2) for c in range(4)]
        return jnp.concatenate(parts, axis=1)

    qo_ref[...] = norm_rope(q_ref[...], vec[0:1]).astype(BF16)
    ko_ref[...] = norm_rope(k_ref[...], vec[1:2]).astype(BF16)
    vo_ref[...] = v_ref[...].astype(BF16)


def _da_prep(p_all, cos_t, sin_t, vec, ones64, batch, seq, tr=256):
    n = batch * seq
    nt = seq // tr
    const = lambda i: (0, 0)
    base = COL_DA // 512
    out = jax.ShapeDtypeStruct((n, 512), BF16)
    return pl.pallas_call(
        _da_prep_kernel,
        grid=(n // tr,),
        in_specs=[pl.BlockSpec((tr, 512), lambda i: (i, base)),
                  pl.BlockSpec((tr, 512), lambda i: (i, base + 1)),
                  pl.BlockSpec((tr, 512), lambda i: (i, base + 2)),
                  pl.BlockSpec((tr, 128), lambda i: (i % nt, 0)),
                  pl.BlockSpec((tr, 128), lambda i: (i % nt, 0)),
                  pl.BlockSpec((8, 512), const),
                  pl.BlockSpec((512, 512), const)],
        out_specs=[pl.BlockSpec((tr, 512), lambda i: (i, 0))] * 3,
        out_shape=[out] * 3,
        compiler_params=_params("arbitrary"),
        name="da_prep",
    )(p_all, p_all, p_all, cos_t, sin_t, vec, ones64)


def _da_attn_kernel(lam_init, q_ref, k_ref, v_ref, lam_ref, g_ref, o_ref,
                    m1_ref, l1_ref, a1_ref, m2_ref, l2_ref, a2_ref):
    tb = ATT_BLOCK
    i = pl.program_id(2)
    q = q_ref[...]
    lane = lax.broadcasted_iota(jnp.int32, q.shape, 1)
    q1 = jnp.where(lane < DA_QK, q, jnp.zeros_like(q))
    q2 = jnp.where(lane < DA_QK, jnp.zeros_like(q), q)
    for m_ref, l_ref, a_ref in ((m1_ref, l1_ref, a1_ref), (m2_ref, l2_ref, a2_ref)):
        m_ref[...] = jnp.full_like(m_ref, -jnp.inf)
        l_ref[...] = jnp.zeros_like(l_ref)
        a_ref[...] = jnp.zeros_like(a_ref)

    def block(j, masked):
        start = pl.multiple_of(j * tb, tb)
        kb = k_ref[pl.ds(start, tb), :]
        vb = v_ref[pl.ds(start, tb), :]
        for qm, m_ref, l_ref, a_ref in ((q1, m1_ref, l1_ref, a1_ref), (q2, m2_ref, l2_ref, a2_ref)):
            s = _dot_nt(qm, kb)
            if masked:
                keep = lax.broadcasted_iota(jnp.int32, (tb, tb), 1) <= lax.broadcasted_iota(jnp.int32, (tb, tb), 0)
                s = jnp.where(keep, s, -jnp.inf)
            _online_softmax_step(s, vb, m_ref, l_ref, a_ref)

    def body(j, carry):
        block(j, False)
        return carry

    lax.fori_loop(0, i, body, 0)
    block(i, True)

    lv = lam_ref[...]
    lam = (jnp.exp(jnp.sum(lv[0:1] * lv[1:2], axis=-1, keepdims=True))
           - jnp.exp(jnp.sum(lv[2:3] * lv[3:4], axis=-1, keepdims=True)) + lam_init)
    o = a1_ref[...] / l1_ref[...] - lam * (a2_ref[...] / l2_ref[...])
    o = o * lax.rsqrt(jnp.mean(o * o, axis=-1, keepdims=True) + NORM_EPS) * g_ref[0:1]
    o_ref[...] = o * (1.0 - lam_init)


def _da_attn(q, k, v, lam_vec, sub_g, lam_init, batch, seq):
    n = batch * seq
    tb = ATT_BLOCK
    nq = seq // tb
    const = lambda b, h, i: (0, 0)
    col = pltpu.VMEM((tb, 1), F32)
    acc = pltpu.VMEM((tb, 128), F32)
    return pl.pallas_call(
        functools.partial(_da_attn_kernel, lam_init),
        grid=(batch, DA_HEADS, nq),
        in_specs=[pl.BlockSpec((tb, 128), lambda b, h, i: (b * nq + i, h)),
                  pl.BlockSpec((seq, 128), lambda b, h, i: (b, h)),
                  pl.BlockSpec((seq, 128), lambda b, h, i: (b, h)),
                  pl.BlockSpec((8, 128), const),
                  pl.BlockSpec((8, 128), const)],
        out_specs=pl.BlockSpec((tb, 128), lambda b, h, i: (b * nq + i, h)),
        out_shape=jax.ShapeDtypeStruct((n, 512), F32),
        scratch_shapes=[col, col, acc, col, col, acc],
        compiler_params=_params("arbitrary", "arbitrary", "arbitrary"),
        name="da_attn",
    )(q, k, v, lam_vec, sub_g)


def _outproj_kernel(ya_ref, yb_ref, yc_ref, yd_ref, z_ref, beta_ref, x_ref, w_ref, o_ref, g_ref):
    @pl.when(pl.program_id(1) == 0)
    def _():
        z = z_ref[...]
        y = jnp.concatenate([ya_ref[...], yb_ref[...], yc_ref[...], yd_ref[...]], axis=1) * beta_ref[0:1]
        g_ref[...] = (y * (z * _sigmoid(z))).astype(BF16)

    o_ref[...] = x_ref[...] + jnp.dot(g_ref[...], w_ref[...], preferred_element_type=F32)


def _outproj(ys, p_all, beta_row, x2, w, tm=512, tn=1024):
    n = x2.shape[0]
    yspec = pl.BlockSpec((tm, 512), lambda i, j: (i, 0))
    return pl.pallas_call(
        _outproj_kernel,
        grid=(n // tm, D_MODEL // tn),
        in_specs=[yspec] * 4 + [pl.BlockSpec((tm, D_MODEL), lambda i, j: (i, COL_Z // D_MODEL))]
        + [pl.BlockSpec((8, D_MODEL), lambda i, j: (0, 0)),
           pl.BlockSpec((tm, tn), lambda i, j: (i, j)),
           pl.BlockSpec((D_MODEL, tn), lambda i, j: (0, j))],
        out_specs=pl.BlockSpec((tm, tn), lambda i, j: (i, j)),
        out_shape=jax.ShapeDtypeStruct((n, D_MODEL), F32),
        scratch_shapes=[pltpu.VMEM((tm, D_MODEL), BF16)],
        compiler_params=_params("arbitrary", "arbitrary"),
        name="outproj",
    )(*ys, p_all, beta_row, x2, w)


def _rows(rows, width, n_rows=8):
    out = jnp.zeros((n_rows, width), F32)
    for i, r in enumerate(rows):
        out = out.at[i, :r.shape[0]].set(r.astype(F32))
    return out


def _pad_to(a, shape):
    return jnp.zeros(shape, a.dtype).at[tuple(slice(0, s) for s in a.shape)].set(a)


def _arrange_w_in(w):
    zeros = lambda c: jnp.zeros((D_MODEL, c), w.dtype)
    return jnp.concatenate([
        w[:, 5704:6408], zeros(64),
        w[:, 3584:3648], zeros(64),
        w[:, 5696:5704], zeros(120),
        w[:, 2048:3584],
        w[:, 3648:5696],
        w[:, 6408:7944],
        w[:, 0:2048],
    ], axis=1).astype(BF16)


def _rope_tables(seq, rot, seg, fill):
    half = rot // 2
    inv = ROPE_THETA ** (-jnp.arange(0, rot, 2, dtype=F32) / rot)
    ang = jnp.arange(seq, dtype=F32)[:, None] * inv[None, :]
    cos, sin = jnp.cos(ang), jnp.sin(ang)
    rest = seg - rot
    cos_s = jnp.concatenate([cos, cos, jnp.full((seq, rest), fill, F32)], axis=1)
    sin_s = jnp.concatenate([-sin, sin, jnp.zeros((seq, rest), F32)], axis=1)
    reps = 128 // seg
    return jnp.tile(cos_s, (1, reps)), jnp.tile(sin_s, (1, reps))


def kernel(x, norm_g, w_in, w_out, branch_beta, rw_mu, rw_w0, rw_w_up, rw_a0, rw_a_up, rw_v0, rw_v_dn, rw_v_up, rw_k_k, rw_k_a, rw_r_k, rw_ln_w, rw_ln_b, ml_conv_w, ml_conv_b, ml_i_b, ml_f_b, ml_norm_g, mla_q_norm_g, mla_kv_norm_g, mla_w_q_b, mla_w_kv_b, mla_q_g, mla_k_g, da_q_g, da_k_g, da_lq1, da_lk1, da_lq2, da_lk2, da_sub_g):
    batch, seq, _ = x.shape
    depth = w_in.shape[0]
    n = batch * seq
    x2 = x.reshape(n, D_MODEL)

    seg_id = jnp.arange(GROUP_W) // RW_HEAD
    ones64 = (seg_id[:, None] == seg_id[None, :]).astype(BF16)
    mla_cos, mla_sin = _rope_tables(seq, MLA_ROPE, 128, 0.0)
    da_cos, da_sin = _rope_tables(seq, DA_ROT, DA_QK, 1.0)

    v_first = None
    for l in range(depth):
        p_all = _inproj(x2, norm_g[l][None, :], _arrange_w_in(w_in[l]))

        mu = rw_mu[l]
        vec = _rows([mu[0:512], mu[512:1024], mu[1024:1536], rw_w0[l], rw_a0[l], rw_k_k[l], rw_k_a[l]]
                    + ([rw_v0[l - 1]] if l > 0 else []), 512)
        mulo = _rows([mu[1536:1600]], 128)
        wup = _pad_to(rw_w_up[l], (128, 512)).astype(BF16)
        aup = jnp.zeros((128, 512), F32).at[32:64].set(rw_a_up[l]).astype(BF16)
        if l > 0:
            vdn = _pad_to(rw_v_dn[l - 1], (512, 128)).astype(BF16)
            vup = _pad_to(rw_v_up[l - 1], (128, 512)).astype(BF16)
        else:
            vdn = vup = None
        r, k, v, lw, kk, a = _rw_prep(p_all, v_first, vec, mulo, wup, aup, vdn, vup, ones64, batch, seq)
        if l == 0:
            v_first = v
        par = _rows([rw_r_k[l].reshape(-1), rw_ln_w[l], rw_ln_b[l]], 512)
        y_rw = _rw_scan(r, k, v, lw, kk, a, par, ones64, batch, seq)

        cw = ml_conv_w[l]
        cb = ml_conv_b[l]
        ml_vec = _rows([cb[0:512], cb[512:1024], ml_norm_g[l]], 512)
        gb = _rows([jnp.concatenate([ml_i_b[l], ml_f_b[l]])], 128)
        y_ml = _ml(p_all, _pad_to(cw[:, 0:512], (8, 512)), _pad_to(cw[:, 512:1024], (8, 512)),
                   ml_vec, gb, batch, seq)

        scale = MLA_QK ** -0.5
        qg = jnp.tile(jnp.concatenate([mla_q_g[l] * scale, jnp.zeros((64,), F32)]), MLA_HEADS)
        mla_vec = _rows([mla_q_norm_g[l], mla_kv_norm_g[l], qg, mla_k_g[l][0:128], mla_k_g[l][128:192]], 1024)
        wq = _pad_to(mla_w_q_b[l].reshape(MLA_Q_LORA, MLA_HEADS, MLA_QK),
                     (MLA_Q_LORA, MLA_HEADS, MLA_PAD)).reshape(MLA_Q_LORA, 1024).astype(BF16)
        wkv = mla_w_kv_b[l].reshape(MLA_KV_LORA, MLA_HEADS, 2, 128).transpose(0, 2, 1, 3)
        wkv = wkv.reshape(MLA_KV_LORA, 1024).astype(BF16)
        mq, mk, mv = _mla_prep(p_all, mla_cos, mla_sin, mla_vec, wq, wkv, batch, seq)
        y_mla = _mla_attn(mq, mk, mv, batch, seq)

        da_vec = _rows([jnp.tile(da_q_g[l], 8) * (DA_QK ** -0.5), jnp.tile(da_k_g[l], 8)], 512)
        dq, dk, dv = _da_prep(p_all, da_cos, da_sin, da_vec, ones64, batch, seq)
        lam_vec = _rows([da_lq1[l], da_lk1[l], da_lq2[l], da_lk2[l]], 128)
        lam_init = 0.8 - 0.6 * math.exp(-0.3 * l)
        y_da = _da_attn(dq, dk, dv, lam_vec, _rows([da_sub_g[l]], 128), lam_init, batch, seq)

        beta_row = _rows([jnp.repeat(branch_beta[l], GROUP_W)], D_MODEL)
        x2 = _outproj((y_rw, y_ml, y_mla, y_da), p_all, beta_row, x2, w_out[l].astype(BF16))
    return x2.reshape(batch, seq, D_MODEL)
```

```python
import functools
import math

import jax
import jax.numpy as jnp
from jax import lax
from jax.experimental import pallas as pl
from jax.experimental.pallas import tpu as pltpu

F32 = jnp.float32
BF16 = jnp.bfloat16

D_MODEL = 2048
GROUP_W = 512
NORM_EPS = 1e-6
ROPE_THETA = 500000.0

RW_HEAD = 64
RW_GN_EPS = 64e-5
RW_CHUNK = 64
ML_HEADS = 4
ML_HEAD = 128
ML_CHUNK = 256
MLA_HEADS = 4
MLA_NOPE = 128
MLA_ROPE = 64
MLA_QK = 192
MLA_Q_LORA = 384
MLA_KV_LORA = 256
MLA_PAD = 256
DA_HEADS = 4
DA_QK = 64
DA_ROT = 16
ATT_BLOCK = 512

N_PROJ = 8192
COL_MLA = 0
COL_RW_LORA = 768
COL_ML_GATE = 896
COL_RW = 1024
COL_ML = 2560
COL_DA = 4608
COL_Z = 6144

VMEM_LIMIT = 48 * 1024 * 1024


def _dot(a, b):
    return jnp.dot(a.astype(BF16), b.astype(BF16), preferred_element_type=F32)


def _dot_nt(a, b):
    return lax.dot_general(a.astype(BF16), b.astype(BF16), (((1,), (1,)), ((), ())),
                           preferred_element_type=F32)


def _dot_tn(a, b):
    return lax.dot_general(a.astype(BF16), b.astype(BF16), (((0,), (0,)), ((), ())),
                           preferred_element_type=F32)


def _split3(x):
    h1 = x.astype(BF16)
    r1 = x - h1.astype(F32)
    h2 = r1.astype(BF16)
    h3 = (r1 - h2.astype(F32)).astype(BF16)
    return h1, h2, h3


def _dot_exact_lhs(a_bf16, x):
    h1, h2, h3 = _split3(x)
    f = functools.partial(jnp.dot, preferred_element_type=F32)
    return f(a_bf16, h1) + f(a_bf16, h2) + f(a_bf16, h3)


def _seg_sum(x, ones_bf16):
    hi = x.astype(BF16)
    lo = (x - hi.astype(F32)).astype(BF16)
    f = functools.partial(jnp.dot, preferred_element_type=F32)
    return f(hi, ones_bf16) + f(lo, ones_bf16)


def _sigmoid(x):
    return 1.0 / (1.0 + jnp.exp(-x))


def _log_sigmoid(x):
    return jnp.minimum(x, 0.0) - jnp.log(1.0 + jnp.exp(-jnp.abs(x)))


def _params(*sem):
    return pltpu.CompilerParams(dimension_semantics=sem, vmem_limit_bytes=VMEM_LIMIT)


def _inproj_kernel(x_ref, g_ref, w_ref, o_ref, h_ref):
    @pl.when(pl.program_id(1) == 0)
    def _():
        x = x_ref[...]
        ms = jnp.mean(x * x, axis=-1, keepdims=True)
        h_ref[...] = (x * lax.rsqrt(ms + NORM_EPS) * g_ref[...]).astype(BF16)

    o_ref[...] = jnp.dot(h_ref[...], w_ref[...], preferred_element_type=F32)


def _inproj(x2, g, w, tm=512, tn=1024):
    n = x2.shape[0]
    return pl.pallas_call(
        _inproj_kernel,
        grid=(n // tm, N_PROJ // tn),
        in_specs=[pl.BlockSpec((tm, D_MODEL), lambda i, j: (i, 0)),
                  pl.BlockSpec((1, D_MODEL), lambda i, j: (0, 0)),
                  pl.BlockSpec((D_MODEL, tn), lambda i, j: (0, j))],
        out_specs=pl.BlockSpec((tm, tn), lambda i, j: (i, j)),
        out_shape=jax.ShapeDtypeStruct((n, N_PROJ), F32),
        scratch_shapes=[pltpu.VMEM((tm, D_MODEL), BF16)],
        compiler_params=_params("arbitrary", "arbitrary"),
        name="inproj",
    )(x2, g, w)


def _rw_prep_kernel(has_vres, *refs):
    if has_vres:
        (r_ref, k_ref, v_ref, lo_ref, vf_ref, vec_ref, mulo_ref, wup_ref, aup_ref, vdn_ref, vup_ref,
         ones_ref, ro_ref, ko_ref, vo_ref, lw_ref, kk_ref, a_ref, cr_ref, ck_ref, cv_ref, cl_ref) = refs
    else:
        (r_ref, k_ref, v_ref, lo_ref, vec_ref, mulo_ref, wup_ref, aup_ref,
         ones_ref, ro_ref, ko_ref, vo_ref, lw_ref, kk_ref, a_ref, cr_ref, ck_ref, cv_ref, cl_ref) = refs

    @pl.when(pl.program_id(1) == 0)
    def _():
        cr_ref[...] = jnp.zeros_like(cr_ref)
        ck_ref[...] = jnp.zeros_like(ck_ref)
        cv_ref[...] = jnp.zeros_like(cv_ref)
        cl_ref[...] = jnp.zeros_like(cl_ref)

    def shift(x_ref, carry_ref, mu):
        x = x_ref[...]
        rows = x.shape[0]
        row = lax.broadcasted_iota(jnp.int32, x.shape, 0)
        prev = jnp.where(row == 0, carry_ref[0:1, :], pltpu.roll(x, 1, 0))
        carry_ref[0:1, :] = x[rows - 1:rows, :]
        return x + (prev - x) * mu

    vec = vec_ref[...]
    r = shift(r_ref, cr_ref, vec[0:1])
    k = shift(k_ref, ck_ref, vec[1:2])
    v = shift(v_ref, cv_ref, vec[2:3])
    lo = shift(lo_ref, cl_ref, mulo_ref[0:1])

    u = vec[3:4] + _dot(jnp.tanh(lo), wup_ref[...])
    lw_ref[...] = -math.exp(-0.5) * _sigmoid(u)
    a = _sigmoid(vec[4:5] + _dot(lo, aup_ref[...]))
    if has_vres:
        gate = _sigmoid(vec[7:8] + _dot(_dot(v, vdn_ref[...]), vup_ref[...]))
        v = v + (vf_ref[...] - v) * gate
    kk = k * vec[5:6]
    ss = _seg_sum(kk * kk, ones_ref[...])
    kk = kk / jnp.maximum(jnp.sqrt(ss), 1e-12)
    ro_ref[...] = r
    ko_ref[...] = k * (1.0 + (a - 1.0) * vec[6:7])
    vo_ref[...] = v
    kk_ref[...] = kk
    a_ref[...] = a


def _rw_prep(p_all, v_first, vec, mulo, wup, aup, vdn, vup, ones64, batch, seq, tr=256):
    n = batch * seq
    nt = seq // tr
    has_vres = v_first is not None

    def rows(c):
        return lambda b, i: (b * nt + i, c)

    const = lambda b, i: (0, 0)
    in_specs = [pl.BlockSpec((tr, 512), rows(COL_RW // 512)),
                pl.BlockSpec((tr, 512), rows(COL_RW // 512 + 1)),
                pl.BlockSpec((tr, 512), rows(COL_RW // 512 + 2)),
                pl.BlockSpec((tr, 128), rows(COL_RW_LORA // 128))]
    args = [p_all, p_all, p_all, p_all]
    if has_vres:
        in_specs.append(pl.BlockSpec((tr, 512), rows(0)))
        args.append(v_first)
    in_specs += [pl.BlockSpec((8, 512), const), pl.BlockSpec((8, 128), const),
                 pl.BlockSpec((128, 512), const), pl.BlockSpec((128, 512), const)]
    args += [vec, mulo, wup, aup]
    if has_vres:
        in_specs += [pl.BlockSpec((512, 128), const), pl.BlockSpec((128, 512), const)]
        args += [vdn, vup]
    in_specs.append(pl.BlockSpec((512, 512), const))
    args.append(ones64)
    out = jax.ShapeDtypeStruct((n, 512), F32)
    return pl.pallas_call(
        functools.partial(_rw_prep_kernel, has_vres),
        grid=(batch, nt),
        in_specs=in_specs,
        out_specs=[pl.BlockSpec((tr, 512), rows(0))] * 6,
        out_shape=[out] * 6,
        scratch_shapes=[pltpu.VMEM((8, 512), F32)] * 3 + [pltpu.VMEM((8, 128), F32)],
        compiler_params=_params("arbitrary", "arbitrary"),
        name="rw_prep",
    )(*args)


def _rw_scan_kernel(r_ref, k_ref, v_ref, lw_ref, kk_ref, a_ref, par_ref, ones_ref, y_ref, s_ref):
    L = RW_CHUNK
    batch = r_ref.shape[0]

    @pl.when(pl.program_id(0) == 0)
    def _():
        s_ref[...] = jnp.zeros_like(s_ref)

    row = lax.broadcasted_iota(jnp.int32, (L, 128), 0)
    lane = lax.broadcasted_iota(jnp.int32, (L, 128), 1)
    col = lane & (RW_HEAD - 1)
    strict = row > col
    incl = row >= col
    eye = (row == col).astype(F32)
    first = lane < RW_HEAD
    r128 = lax.broadcasted_iota(jnp.int32, (128, 128), 0)
    c128 = lax.broadcasted_iota(jnp.int32, (128, 128), 1)
    same_head = (r128 < RW_HEAD) == (c128 < RW_HEAD)
    tri = (lax.broadcasted_iota(jnp.int32, (L, L), 0) >= lax.broadcasted_iota(jnp.int32, (L, L), 1)).astype(BF16)

    def bd(x):
        return jnp.concatenate([jnp.where(first, x, 0.0), jnp.where(first, 0.0, x)], axis=0)

    n_pairs = GROUP_W // 128
    units = [(bi, pr) for bi in range(batch) for pr in range(n_pairs)]
    idx = range(len(units))
    al, rt, bet, kt, beh, kh, p_chunk, vs, rs, ks = [], [], [], [], [], [], [], [], [], []
    for bi in range(batch):
        lw = lw_ref[bi]
        r = r_ref[bi]
        k = k_ref[bi]
        kk = kk_ref[bi]
        b = _dot_exact_lhs(tri, lw)
        b_last = b[L - 1:L, :]
        p_inv = jnp.exp(-b)
        p_end = jnp.exp(b_last - b)
        be = kk * a_ref[bi]
        al.append(-(kk * jnp.exp(b - lw)))
        rt.append(r * jnp.exp(b))
        bet.append(be * p_inv)
        kt.append(k * p_inv)
        beh.append(be * p_end)
        kh.append(k * p_end)
        p_chunk.append(jnp.exp(b_last))
        vs.append(v_ref[bi])
        rs.append(r)
        ks.append(k)

    def part(xs, i):
        bi, pr = units[i]
        return xs[bi][:, 128 * pr:128 * pr + 128]

    lhs = [jnp.concatenate([part(al, i), part(rt, i)], axis=0).astype(BF16) for i in idx]
    rhs = [jnp.concatenate([bd(part(bet, i)), bd(part(kt, i))], axis=0).astype(BF16) for i in idx]
    g = [_dot_nt(lhs[i], rhs[i]) for i in idx]
    a_ab = [jnp.where(strict, g[i][0:L, 0:128], 0.0) for i in idx]
    a_ak = [jnp.where(strict, g[i][0:L, 128:256], 0.0) for i in idx]
    b_rb = [jnp.where(incl, g[i][L:2 * L, 0:128], 0.0) for i in idx]
    b_rk = [jnp.where(incl, g[i][L:2 * L, 128:256], 0.0) for i in idx]
    s = [s_ref[i] for i in idx]
    sa = [_dot_nt(lhs[i], s[i]) for i in idx]
    vv = [_dot(jnp.concatenate([a_ak[i], b_rk[i]], axis=0), bd(part(vs, i))) for i in idx]

    t_inv = [eye + a_ab[i] for i in idx]
    x = [_dot(a_ab[i], bd(a_ab[i])) for i in idx]
    for _ in range(4):
        xt = [_dot(jnp.concatenate([x[i], t_inv[i]], axis=0), bd(x[i])) for i in idx]
        x = [xt[i][0:L] for i in idx]
        t_inv = [t_inv[i] + xt[i][L:2 * L] for i in idx]
    t_inv = [t_inv[i] + _dot(t_inv[i], bd(x[i])) for i in idx]

    u = [_dot(t_inv[i], bd(sa[i][0:L] + vv[i][0:L])) for i in idx]
    ys = [sa[i][L:2 * L] + vv[i][L:2 * L] + _dot(b_rb[i], bd(u[i])) for i in idx]
    for i in idx:
        upd = _dot_tn(jnp.concatenate([u[i], part(vs, i)], axis=0),
                      jnp.concatenate([part(beh, i), part(kh, i)], axis=0))
        s_ref[i] = s[i] * part(p_chunk, i) + jnp.where(same_head, upd, 0.0)

    par = par_ref[...]
    ones = ones_ref[...]
    inv_n = 1.0 / RW_HEAD
    for bi in range(batch):
        y = jnp.concatenate(ys[n_pairs * bi:n_pairs * (bi + 1)], axis=1)
        mu = _seg_sum(y, ones) * inv_n
        yc = y - mu
        var = _seg_sum(yc * yc, ones) * inv_n
        bonus = _seg_sum(rs[bi] * ks[bi] * par[0:1], ones) * vs[bi]
        y_ref[bi] = yc * lax.rsqrt(var + RW_GN_EPS) * par[1:2] + par[2:3] + bonus


def _rw_scan(r, k, v, lw, kk, a, par, ones64, batch, seq):
    nc = seq // RW_CHUNK
    blk = pl.BlockSpec((batch, RW_CHUNK, 512), lambda c: (0, c, 0))
    const = lambda c: (0, 0)
    as3 = lambda t: t.reshape(batch, seq, 512)
    y = pl.pallas_call(
        _rw_scan_kernel,
        grid=(nc,),
        in_specs=[blk] * 6 + [pl.BlockSpec((8, 512), const), pl.BlockSpec((512, 512), const)],
        out_specs=blk,
        out_shape=jax.ShapeDtypeStruct((batch, seq, 512), F32),
        scratch_shapes=[pltpu.VMEM((batch * (GROUP_W // 128), 128, 128), F32)],
        compiler_params=_params("arbitrary"),
        name="rw_scan",
    )(as3(r), as3(k), as3(v), as3(lw), as3(kk), as3(a), par, ones64)
    return y.reshape(batch * seq, 512)


def _ml_kernel(q_ref, k_ref, v_ref, o_ref, g_ref, cwq_ref, cwk_ref, vec_ref, gb_ref, y_ref,
               hq_ref, hk_ref, c_ref, n_ref, m_ref):
    L = ML_CHUNK

    @pl.when(pl.program_id(1) == 0)
    def _():
        hq_ref[...] = jnp.zeros_like(hq_ref)
        hk_ref[...] = jnp.zeros_like(hk_ref)
        c_ref[...] = jnp.zeros_like(c_ref)
        n_ref[...] = jnp.zeros_like(n_ref)
        m_ref[...] = jnp.zeros_like(m_ref)

    vec = vec_ref[...]
    row8 = lax.broadcasted_iota(jnp.int32, (8, GROUP_W), 0)

    def conv(x_ref, halo_ref, w_ref, bias):
        x = x_ref[...]
        w = w_ref[...]
        halo = halo_ref[...]
        acc = x * w[3:4] + bias
        for s in (1, 2, 3):
            xs = pltpu.roll(x, s, 0)
            top = jnp.where(row8 < s, pltpu.roll(halo, s, 0), xs[0:8])
            xs = jnp.concatenate([top, xs[8:]], axis=0)
            acc = acc + xs * w[3 - s:4 - s]
        halo_ref[...] = x[L - 8:L]
        return acc * _sigmoid(acc)

    q = conv(q_ref, hq_ref, cwq_ref, vec[0:1])
    k = conv(k_ref, hk_ref, cwk_ref, vec[1:2]) * (ML_HEAD ** -0.5)
    v = v_ref[...]
    og = _sigmoid(o_ref[...])

    g = g_ref[...] + gb_ref[0:1]
    lane = lax.broadcasted_iota(jnp.int32, (L, 128), 1)
    tri_b = lax.broadcasted_iota(jnp.int32, (L, L), 0) >= lax.broadcasted_iota(jnp.int32, (L, L), 1)
    cum = _dot_exact_lhs(tri_b.astype(BF16), _log_sigmoid(g))
    comb_t = jnp.where(lane < ML_HEADS, g, cum).T

    outs = []
    for h in range(ML_HEADS):
        sl = slice(128 * h, 128 * h + 128)
        qh, kh, vh = q[:, sl], k[:, sl], v[:, sl]
        b_col = cum[:, ML_HEADS + h:ML_HEADS + h + 1]
        ig_col = g[:, h:h + 1]
        b_row = comb_t[ML_HEADS + h:ML_HEADS + h + 1, :]
        ig_row = comb_t[h:h + 1, :]
        dm = jnp.where(tri_b, b_col - b_row + ig_row, -jnp.inf)
        m_prev = m_ref[h:h + 1, 0:1]
        inter = b_col + m_prev
        mj = jnp.maximum(inter, jnp.max(dm, axis=-1, keepdims=True))
        s = _dot_nt(qh, kh) * jnp.exp(dm - mj)
        iw = jnp.exp(inter - mj)
        c_h = c_ref[h]
        n_h = n_ref[h:h + 1, :]
        num = _dot(s, vh) + iw * _dot_nt(qh, c_h)
        den = jnp.sum(s, axis=-1, keepdims=True) + iw * jnp.sum(qh * n_h, axis=-1, keepdims=True)
        hh = num / jnp.maximum(jnp.abs(den), jnp.exp(-mj))

        b_last = b_col[L - 1:L, :]
        g_col = b_last - b_col + ig_col
        m_new = jnp.maximum(b_last + m_prev, jnp.max(g_col, axis=0, keepdims=True))
        a_old = jnp.exp(b_last + m_prev - m_new)
        wg = jnp.exp(g_col - m_new)
        c_ref[h] = a_old * c_h + _dot_tn(vh * wg, kh)
        n_ref[h:h + 1, :] = a_old * n_h + jnp.sum(kh * wg, axis=0, keepdims=True)
        m_ref[h:h + 1, :] = jnp.broadcast_to(m_new, (1, 128))

        hn = hh * lax.rsqrt(jnp.mean(hh * hh, axis=-1, keepdims=True) + NORM_EPS) * vec[2:3, sl]
        outs.append(og[:, sl] * hn)
    y_ref[...] = jnp.concatenate(outs, axis=1)


def _ml(p_all, cwq, cwk, vec, gb, batch, seq):
    n = batch * seq
    nc = seq // ML_CHUNK

    def rows(c):
        return lambda b, i: (b * nc + i, c)

    const = lambda b, i: (0, 0)
    base = COL_ML // 512
    return pl.pallas_call(
        _ml_kernel,
        grid=(batch, nc),
        in_specs=[pl.BlockSpec((ML_CHUNK, 512), rows(base)),
                  pl.BlockSpec((ML_CHUNK, 512), rows(base + 1)),
                  pl.BlockSpec((ML_CHUNK, 512), rows(base + 2)),
                  pl.BlockSpec((ML_CHUNK, 512), rows(base + 3)),
                  pl.BlockSpec((ML_CHUNK, 128), rows(COL_ML_GATE // 128)),
                  pl.BlockSpec((8, 512), const), pl.BlockSpec((8, 512), const),
                  pl.BlockSpec((8, 512), const), pl.BlockSpec((8, 128), const)],
        out_specs=pl.BlockSpec((ML_CHUNK, 512), rows(0)),
        out_shape=jax.ShapeDtypeStruct((n, 512), F32),
        scratch_shapes=[pltpu.VMEM((8, 512), F32), pltpu.VMEM((8, 512), F32),
                        pltpu.VMEM((ML_HEADS, 128, 128), F32), pltpu.VMEM((8, 128), F32),
                        pltpu.VMEM((8, 128), F32)],
        compiler_params=_params("arbitrary", "arbitrary"),
        name="mlstm",
    )(p_all, p_all, p_all, p_all, p_all, cwq, cwk, vec, gb)


def _rope128(x, cos_t, sin_t, half):
    lane = lax.broadcasted_iota(jnp.int32, x.shape, 1)
    seg = 128 if half == MLA_ROPE // 2 else DA_QK
    swapped = jnp.where((lane & (seg - 1)) < half, pltpu.roll(x, 128 - half, 1), pltpu.roll(x, half, 1))
    return x * cos_t + swapped * sin_t


def _mla_prep_kernel(p_ref, cos_ref, sin_ref, vec_ref, wq_ref, wkv_ref, q_ref, k_ref, vt_ref):
    p = p_ref[...]
    vec = vec_ref[...]
    cos_t = cos_ref[...]
    sin_t = sin_ref[...]
    q_lat = p[:, 0:MLA_Q_LORA]
    kv_lat = p[:, MLA_Q_LORA:MLA_Q_LORA + MLA_KV_LORA]
    k_pe = p[:, 640:768]
    qn = q_lat * lax.rsqrt(jnp.mean(q_lat * q_lat, axis=-1, keepdims=True) + NORM_EPS) * vec[0:1, 0:MLA_Q_LORA]
    kvn = kv_lat * lax.rsqrt(jnp.mean(kv_lat * kv_lat, axis=-1, keepdims=True) + NORM_EPS) * vec[1:2, 0:MLA_KV_LORA]
    q4 = _dot(qn, wq_ref[...])
    kv4 = _dot(kvn, wkv_ref[...])
    pe_ss = jnp.sum(k_pe * k_pe, axis=-1, keepdims=True)
    qs, ks = [], []
    for h in range(MLA_HEADS):
        qh = q4[:, MLA_PAD * h:MLA_PAD * (h + 1)]
        inv = lax.rsqrt(jnp.sum(qh * qh, axis=-1, keepdims=True) * (1.0 / MLA_QK) + NORM_EPS)
        qh = qh * inv * vec[2:3, MLA_PAD * h:MLA_PAD * (h + 1)]
        qs += [qh[:, 0:128], _rope128(qh[:, 128:256], cos_t, sin_t, MLA_ROPE // 2)]
        kn = kv4[:, 128 * h:128 * (h + 1)]
        inv = lax.rsqrt((jnp.sum(kn * kn, axis=-1, keepdims=True) + pe_ss) * (1.0 / MLA_QK) + NORM_EPS)
        ks += [kn * inv * vec[3:4, 0:128], _rope128(k_pe * inv * vec[4:5, 0:128], cos_t, sin_t, MLA_ROPE // 2)]
    q_ref[...] = jnp.concatenate(qs, axis=1).astype(BF16)
    k_ref[...] = jnp.concatenate(ks, axis=1).astype(BF16)
    vt_ref[...] = kv4[:, 512:1024].T.reshape(vt_ref.shape).astype(BF16)


def _vt_spec(heads, nt):
    return pl.BlockSpec((None, heads, None, 128, ATT_BLOCK), lambda i: (i // nt, 0, i % nt, 0, 0))


def _mla_prep(p_all, cos_t, sin_t, vec, wq, wkv, batch, seq):
    n = batch * seq
    tr = ATT_BLOCK
    nt = seq // tr
    const = lambda i: (0, 0)
    return pl.pallas_call(
        _mla_prep_kernel,
        grid=(n // tr,),
        in_specs=[pl.BlockSpec((tr, 768), lambda i: (i, COL_MLA // 768)),
                  pl.BlockSpec((tr, 128), lambda i: (i % nt, 0)),
                  pl.BlockSpec((tr, 128), lambda i: (i % nt, 0)),
                  pl.BlockSpec((8, 1024), const),
                  pl.BlockSpec((MLA_Q_LORA, 1024), const),
                  pl.BlockSpec((MLA_KV_LORA, 1024), const)],
        out_specs=[pl.BlockSpec((tr, 1024), lambda i: (i, 0)),
                   pl.BlockSpec((tr, 1024), lambda i: (i, 0)),
                   _vt_spec(MLA_HEADS, nt)],
        out_shape=[jax.ShapeDtypeStruct((n, 1024), BF16), jax.ShapeDtypeStruct((n, 1024), BF16),
                   jax.ShapeDtypeStruct((batch, MLA_HEADS, nt, 128, tr), BF16)],
        compiler_params=_params("arbitrary"),
        name="mla_prep",
    )(p_all, cos_t, sin_t, vec, wq, wkv)


def _online_softmax_step_t(s_t, vt, masked, m_ref, l_ref, acc_ref):
    if masked:
        keep = lax.broadcasted_iota(jnp.int32, s_t.shape, 0) <= lax.broadcasted_iota(jnp.int32, s_t.shape, 1)
        s_t = jnp.where(keep, s_t, -jnp.inf)
    m_old = m_ref[...]
    m_new = jnp.maximum(m_old, jnp.max(s_t, axis=0, keepdims=True))
    p = jnp.exp(s_t - m_new)
    alpha = jnp.exp(m_old - m_new)
    l_ref[...] = alpha * l_ref[...] + jnp.sum(p, axis=0, keepdims=True)
    acc_ref[...] = alpha * acc_ref[...] + _dot(vt, p)
    m_ref[...] = m_new


def _init_stats(m_ref, l_ref, acc_ref):
    m_ref[...] = jnp.full_like(m_ref, -jnp.inf)
    l_ref[...] = jnp.zeros_like(l_ref)
    acc_ref[...] = jnp.zeros_like(acc_ref)


def _mla_attn_kernel(q_ref, k_ref, vt_ref, o_ref, m_ref, l_ref, acc_ref):
    tb = ATT_BLOCK
    i = pl.program_id(2)
    q = q_ref[...]
    _init_stats(m_ref, l_ref, acc_ref)

    def block(j, masked):
        start = pl.multiple_of(j * tb, tb)
        s_t = _dot_nt(k_ref[pl.ds(start, tb), :], q)
        _online_softmax_step_t(s_t, vt_ref[j], masked, m_ref, l_ref, acc_ref)

    def body(j, carry):
        block(j, False)
        return carry

    lax.fori_loop(0, i, body, 0)
    block(i, True)
    o_ref[...] = (acc_ref[...] / l_ref[...]).T


def _mla_attn(q, k, vt, batch, seq):
    n = batch * seq
    tb = ATT_BLOCK
    nq = seq // tb
    return pl.pallas_call(
        _mla_attn_kernel,
        grid=(batch, MLA_HEADS, nq),
        in_specs=[pl.BlockSpec((tb, MLA_PAD), lambda b, h, i: (b * nq + i, h)),
                  pl.BlockSpec((seq, MLA_PAD), lambda b, h, i: (b, h)),
                  pl.BlockSpec((None, None, nq, 128, tb), lambda b, h, i: (b, h, 0, 0, 0))],
        out_specs=pl.BlockSpec((tb, 128), lambda b, h, i: (b * nq + i, h)),
        out_shape=jax.ShapeDtypeStruct((n, 512), F32),
        scratch_shapes=[pltpu.VMEM((1, tb), F32), pltpu.VMEM((1, tb), F32), pltpu.VMEM((128, tb), F32)],
        compiler_params=_params("arbitrary", "arbitrary", "arbitrary"),
        name="mla_attn",
    )(q, k, vt)


def _da_prep_kernel(q_ref, k_ref, v_ref, cos_ref, sin_ref, vec_ref, ones_ref, qo_ref, ko_ref, vo_ref):
    vec = vec_ref[...]
    ones = ones_ref[...]
    cos_t = cos_ref[...]
    sin_t = sin_ref[...]

    def norm_rope(x, gain):
        ss = _seg_sum(x * x, ones) * (1.0 / DA_QK)
        x = x * lax.rsqrt(ss + NORM_EPS) * gain
        parts = [_rope128(x[:, 128 * c:128 * (c + 1)], cos_t, sin_t, DA_ROT // 2) for c in range(4)]
        return jnp.concatenate(parts, axis=1)

    qo_ref[...] = norm_rope(q_ref[...], vec[0:1]).astype(BF16)
    ko_ref[...] = norm_rope(k_ref[...], vec[1:2]).astype(BF16)
    vo_ref[...] = v_ref[...].T.reshape(vo_ref.shape).astype(BF16)


def _da_prep(p_all, cos_t, sin_t, vec, ones64, batch, seq):
    n = batch * seq
    tr = ATT_BLOCK
    nt = seq // tr
    const = lambda i: (0, 0)
    base = COL_DA // 512
    out = jax.ShapeDtypeStruct((n, 512), BF16)
    return pl.pallas_call(
        _da_prep_kernel,
        grid=(n // tr,),
        in_specs=[pl.BlockSpec((tr, 512), lambda i: (i, base)),
                  pl.BlockSpec((tr, 512), lambda i: (i, base + 1)),
                  pl.BlockSpec((tr, 512), lambda i: (i, base + 2)),
                  pl.BlockSpec((tr, 128), lambda i: (i % nt, 0)),
                  pl.BlockSpec((tr, 128), lambda i: (i % nt, 0)),
                  pl.BlockSpec((8, 512), const),
                  pl.BlockSpec((512, 512), const)],
        out_specs=[pl.BlockSpec((tr, 512), lambda i: (i, 0))] * 2 + [_vt_spec(DA_HEADS, nt)],
        out_shape=[out, out, jax.ShapeDtypeStruct((batch, DA_HEADS, nt, 128, tr), BF16)],
        compiler_params=_params("arbitrary"),
        name="da_prep",
    )(p_all, p_all, p_all, cos_t, sin_t, vec, ones64)


def _da_attn_kernel(lam_init, q_ref, k_ref, vt_ref, lam_ref, g_ref, o_ref,
                    m1_ref, l1_ref, a1_ref, m2_ref, l2_ref, a2_ref):
    tb = ATT_BLOCK
    i = pl.program_id(2)
    q = q_ref[...]
    lane = lax.broadcasted_iota(jnp.int32, q.shape, 1)
    q1 = jnp.where(lane < DA_QK, q, jnp.zeros_like(q))
    q2 = jnp.where(lane < DA_QK, jnp.zeros_like(q), q)
    maps = ((q1, m1_ref, l1_ref, a1_ref), (q2, m2_ref, l2_ref, a2_ref))
    for _, m_ref, l_ref, a_ref in maps:
        _init_stats(m_ref, l_ref, a_ref)

    def block(j, masked):
        start = pl.multiple_of(j * tb, tb)
        kb = k_ref[pl.ds(start, tb), :]
        vt = vt_ref[j]
        for qm, m_ref, l_ref, a_ref in maps:
            _online_softmax_step_t(_dot_nt(kb, qm), vt, masked, m_ref, l_ref, a_ref)

    def body(j, carry):
        block(j, False)
        return carry

    lax.fori_loop(0, i, body, 0)
    block(i, True)

    lv = lam_ref[...]
    lam = (jnp.exp(jnp.sum(lv[0:1] * lv[1:2], axis=-1, keepdims=True))
           - jnp.exp(jnp.sum(lv[2:3] * lv[3:4], axis=-1, keepdims=True)) + lam_init)
    o = (a1_ref[...] / l1_ref[...] - lam * (a2_ref[...] / l2_ref[...])).T
    o = o * lax.rsqrt(jnp.mean(o * o, axis=-1, keepdims=True) + NORM_EPS) * g_ref[0:1]
    o_ref[...] = o * (1.0 - lam_init)


def _da_attn(q, k, vt, lam_vec, sub_g, lam_init, batch, seq):
    n = batch * seq
    tb = ATT_BLOCK
    nq = seq // tb
    const = lambda b, h, i: (0, 0)
    row = pltpu.VMEM((1, tb), F32)
    acc = pltpu.VMEM((128, tb), F32)
    return pl.pallas_call(
        functools.partial(_da_attn_kernel, lam_init),
        grid=(batch, DA_HEADS, nq),
        in_specs=[pl.BlockSpec((tb, 128), lambda b, h, i: (b * nq + i, h)),
                  pl.BlockSpec((seq, 128), lambda b, h, i: (b, h)),
                  pl.BlockSpec((None, None, nq, 128, tb), lambda b, h, i: (b, h, 0, 0, 0)),
                  pl.BlockSpec((8, 128), const),
                  pl.BlockSpec((8, 128), const)],
        out_specs=pl.BlockSpec((tb, 128), lambda b, h, i: (b * nq + i, h)),
        out_shape=jax.ShapeDtypeStruct((n, 512), F32),
        scratch_shapes=[row, row, acc, row, row, acc],
        compiler_params=_params("arbitrary", "arbitrary", "arbitrary"),
        name="da_attn",
    )(q, k, vt, lam_vec, sub_g)


def _outproj_kernel(ya_ref, yb_ref, yc_ref, yd_ref, z_ref, beta_ref, x_ref, w_ref, o_ref, g_ref):
    @pl.when(pl.program_id(1) == 0)
    def _():
        z = z_ref[...]
        y = jnp.concatenate([ya_ref[...], yb_ref[...], yc_ref[...], yd_ref[...]], axis=1) * beta_ref[0:1]
        g_ref[...] = (y * (z * _sigmoid(z))).astype(BF16)

    o_ref[...] = x_ref[...] + jnp.dot(g_ref[...], w_ref[...], preferred_element_type=F32)


def _outproj(ys, p_all, beta_row, x2, w, tm=512, tn=1024):
    n = x2.shape[0]
    yspec = pl.BlockSpec((tm, 512), lambda i, j: (i, 0))
    return pl.pallas_call(
        _outproj_kernel,
        grid=(n // tm, D_MODEL // tn),
        in_specs=[yspec] * 4 + [pl.BlockSpec((tm, D_MODEL), lambda i, j: (i, COL_Z // D_MODEL))]
        + [pl.BlockSpec((8, D_MODEL), lambda i, j: (0, 0)),
           pl.BlockSpec((tm, tn), lambda i, j: (i, j)),
           pl.BlockSpec((D_MODEL, tn), lambda i, j: (0, j))],
        out_specs=pl.BlockSpec((tm, tn), lambda i, j: (i, j)),
        out_shape=jax.ShapeDtypeStruct((n, D_MODEL), F32),
        scratch_shapes=[pltpu.VMEM((tm, D_MODEL), BF16)],
        compiler_params=_params("arbitrary", "arbitrary"),
        name="outproj",
    )(*ys, p_all, beta_row, x2, w)


def _rows(rows, width, n_rows=8):
    out = jnp.zeros((n_rows, width), F32)
    for i, r in enumerate(rows):
        out = out.at[i, :r.shape[0]].set(r.astype(F32))
    return out


def _pad_to(a, shape):
    return jnp.zeros(shape, a.dtype).at[tuple(slice(0, s) for s in a.shape)].set(a)


def _arrange_w_in(w):
    zeros = lambda c: jnp.zeros((D_MODEL, c), w.dtype)
    return jnp.concatenate([
        w[:, 5704:6408], zeros(64),
        w[:, 3584:3648], zeros(64),
        w[:, 5696:5704], zeros(120),
        w[:, 2048:3584],
        w[:, 3648:5696],
        w[:, 6408:7944],
        w[:, 0:2048],
    ], axis=1).astype(BF16)


def _rope_tables(seq, rot, seg, fill):
    half = rot // 2
    inv = ROPE_THETA ** (-jnp.arange(0, rot, 2, dtype=F32) / rot)
    ang = jnp.arange(seq, dtype=F32)[:, None] * inv[None, :]
    cos, sin = jnp.cos(ang), jnp.sin(ang)
    rest = seg - rot
    cos_s = jnp.concatenate([cos, cos, jnp.full((seq, rest), fill, F32)], axis=1)
    sin_s = jnp.concatenate([-sin, sin, jnp.zeros((seq, rest), F32)], axis=1)
    reps = 128 // seg
    return jnp.tile(cos_s, (1, reps)), jnp.tile(sin_s, (1, reps))


def kernel(x, norm_g, w_in, w_out, branch_beta, rw_mu, rw_w0, rw_w_up, rw_a0, rw_a_up, rw_v0, rw_v_dn, rw_v_up, rw_k_k, rw_k_a, rw_r_k, rw_ln_w, rw_ln_b, ml_conv_w, ml_conv_b, ml_i_b, ml_f_b, ml_norm_g, mla_q_norm_g, mla_kv_norm_g, mla_w_q_b, mla_w_kv_b, mla_q_g, mla_k_g, da_q_g, da_k_g, da_lq1, da_lk1, da_lq2, da_lk2, da_sub_g):
    batch, seq, _ = x.shape
    depth = w_in.shape[0]
    n = batch * seq
    x2 = x.reshape(n, D_MODEL)

    seg_id = jnp.arange(GROUP_W) // RW_HEAD
    ones64 = (seg_id[:, None] == seg_id[None, :]).astype(BF16)
    mla_cos, mla_sin = _rope_tables(seq, MLA_ROPE, 128, 0.0)
    da_cos, da_sin = _rope_tables(seq, DA_ROT, DA_QK, 1.0)

    v_first = None
    for l in range(depth):
        p_all = _inproj(x2, norm_g[l][None, :], _arrange_w_in(w_in[l]))

        mu = rw_mu[l]
        vec = _rows([mu[0:512], mu[512:1024], mu[1024:1536], rw_w0[l], rw_a0[l], rw_k_k[l], rw_k_a[l]]
                    + ([rw_v0[l - 1]] if l > 0 else []), 512)
        mulo = _rows([mu[1536:1600]], 128)
        wup = _pad_to(rw_w_up[l], (128, 512)).astype(BF16)
        aup = jnp.zeros((128, 512), F32).at[32:64].set(rw_a_up[l]).astype(BF16)
        if l > 0:
            vdn = _pad_to(rw_v_dn[l - 1], (512, 128)).astype(BF16)
            vup = _pad_to(rw_v_up[l - 1], (128, 512)).astype(BF16)
        else:
            vdn = vup = None
        r, k, v, lw, kk, a = _rw_prep(p_all, v_first, vec, mulo, wup, aup, vdn, vup, ones64, batch, seq)
        if l == 0:
            v_first = v
        par = _rows([rw_r_k[l].reshape(-1), rw_ln_w[l], rw_ln_b[l]], 512)
        y_rw = _rw_scan(r, k, v, lw, kk, a, par, ones64, batch, seq)

        cw = ml_conv_w[l]
        cb = ml_conv_b[l]
        ml_vec = _rows([cb[0:512], cb[512:1024], ml_norm_g[l]], 512)
        gb = _rows([jnp.concatenate([ml_i_b[l], ml_f_b[l]])], 128)
        y_ml = _ml(p_all, _pad_to(cw[:, 0:512], (8, 512)), _pad_to(cw[:, 512:1024], (8, 512)),
                   ml_vec, gb, batch, seq)

        scale = MLA_QK ** -0.5
        qg = jnp.tile(jnp.concatenate([mla_q_g[l] * scale, jnp.zeros((64,), F32)]), MLA_HEADS)
        mla_vec = _rows([mla_q_norm_g[l], mla_kv_norm_g[l], qg, mla_k_g[l][0:128], mla_k_g[l][128:192]], 1024)
        wq = _pad_to(mla_w_q_b[l].reshape(MLA_Q_LORA, MLA_HEADS, MLA_QK),
                     (MLA_Q_LORA, MLA_HEADS, MLA_PAD)).reshape(MLA_Q_LORA, 1024).astype(BF16)
        wkv = mla_w_kv_b[l].reshape(MLA_KV_LORA, MLA_HEADS, 2, 128).transpose(0, 2, 1, 3)
        wkv = wkv.reshape(MLA_KV_LORA, 1024).astype(BF16)
        mq, mk, mv = _mla_prep(p_all, mla_cos, mla_sin, mla_vec, wq, wkv, batch, seq)
        y_mla = _mla_attn(mq, mk, mv, batch, seq)

        da_vec = _rows([jnp.tile(da_q_g[l], 8) * (DA_QK ** -0.5), jnp.tile(da_k_g[l], 8)], 512)
        dq, dk, dv = _da_prep(p_all, da_cos, da_sin, da_vec, ones64, batch, seq)
        lam_vec = _rows([da_lq1[l], da_lk1[l], da_lq2[l], da_lk2[l]], 128)
        lam_init = 0.8 - 0.6 * math.exp(-0.3 * l)
        y_da = _da_attn(dq, dk, dv, lam_vec, _rows([da_sub_g[l]], 128), lam_init, batch, seq)

        beta_row = _rows([jnp.repeat(branch_beta[l], GROUP_W)], D_MODEL)
        x2 = _outproj((y_rw, y_ml, y_mla, y_da), p_all, beta_row, x2, w_out[l].astype(BF16))
    return x2.reshape(batch, seq, D_MODEL)
```

```python
import functools
import math

import jax
import jax.numpy as jnp
from jax import lax
from jax.experimental import pallas as pl
from jax.experimental.pallas import tpu as pltpu

F32 = jnp.float32
BF16 = jnp.bfloat16

D_MODEL = 2048
GROUP_W = 512
NORM_EPS = 1e-6
ROPE_THETA = 500000.0

RW_HEAD = 64
RW_GN_EPS = 64e-5
RW_CHUNK = 64
ML_HEADS = 4
ML_HEAD = 128
ML_CHUNK = 256
MLA_HEADS = 4
MLA_NOPE = 128
MLA_ROPE = 64
MLA_QK = 192
MLA_Q_LORA = 384
MLA_KV_LORA = 256
MLA_PAD = 256
DA_HEADS = 4
DA_QK = 64
DA_ROT = 16
ATT_BLOCK = 512
ATT_SUB = 256
MLA_ATT_HEADS = 4
DA_ATT_HEADS = 2
DV_AUG = 144
LOG2E = math.log2(math.e)

N_PROJ = 8192
COL_MLA = 0
COL_RW_LORA = 768
COL_ML_GATE = 896
COL_RW = 1024
COL_ML = 2560
COL_DA = 4608
COL_Z = 6144

VMEM_LIMIT = 48 * 1024 * 1024


def _dot(a, b):
    return jnp.dot(a.astype(BF16), b.astype(BF16), preferred_element_type=F32)


def _dot_nt(a, b):
    return lax.dot_general(a.astype(BF16), b.astype(BF16), (((1,), (1,)), ((), ())),
                           preferred_element_type=F32)


def _dot_tn(a, b):
    return lax.dot_general(a.astype(BF16), b.astype(BF16), (((0,), (0,)), ((), ())),
                           preferred_element_type=F32)


def _split3(x):
    h1 = x.astype(BF16)
    r1 = x - h1.astype(F32)
    h2 = r1.astype(BF16)
    h3 = (r1 - h2.astype(F32)).astype(BF16)
    return h1, h2, h3


def _dot_exact_lhs(a_bf16, x):
    h1, h2, h3 = _split3(x)
    f = functools.partial(jnp.dot, preferred_element_type=F32)
    return f(a_bf16, h1) + f(a_bf16, h2) + f(a_bf16, h3)


def _seg_sum(x, ones_bf16):
    hi = x.astype(BF16)
    lo = (x - hi.astype(F32)).astype(BF16)
    f = functools.partial(jnp.dot, preferred_element_type=F32)
    return f(hi, ones_bf16) + f(lo, ones_bf16)


def _sigmoid(x):
    return 1.0 / (1.0 + jnp.exp(-x))


def _log_sigmoid(x):
    return jnp.minimum(x, 0.0) - jnp.log(1.0 + jnp.exp(-jnp.abs(x)))


def _params(*sem):
    return pltpu.CompilerParams(dimension_semantics=sem, vmem_limit_bytes=VMEM_LIMIT)


def _inproj_kernel(x_ref, g_ref, w_ref, o_ref, h_ref):
    @pl.when(pl.program_id(1) == 0)
    def _():
        x = x_ref[...]
        ms = jnp.mean(x * x, axis=-1, keepdims=True)
        h_ref[...] = (x * lax.rsqrt(ms + NORM_EPS) * g_ref[...]).astype(BF16)

    o_ref[...] = jnp.dot(h_ref[...], w_ref[...], preferred_element_type=F32)


def _inproj(x2, g, w, tm=1024, tn=1024):
    n = x2.shape[0]
    return pl.pallas_call(
        _inproj_kernel,
        grid=(n // tm, N_PROJ // tn),
        in_specs=[pl.BlockSpec((tm, D_MODEL), lambda i, j: (i, 0)),
                  pl.BlockSpec((1, D_MODEL), lambda i, j: (0, 0)),
                  pl.BlockSpec((D_MODEL, tn), lambda i, j: (0, j))],
        out_specs=pl.BlockSpec((tm, tn), lambda i, j: (i, j)),
        out_shape=jax.ShapeDtypeStruct((n, N_PROJ), F32),
        scratch_shapes=[pltpu.VMEM((tm, D_MODEL), BF16)],
        compiler_params=_params("arbitrary", "arbitrary"),
        name="inproj",
    )(x2, g, w)


def _rw_prep_kernel(has_vres, *refs):
    if has_vres:
        (r_ref, k_ref, v_ref, lo_ref, vf_ref, vec_ref, mulo_ref, wup_ref, aup_ref, vdn_ref, vup_ref,
         ones_ref, ro_ref, ko_ref, vo_ref, lw_ref, kk_ref, a_ref, cr_ref, ck_ref, cv_ref, cl_ref) = refs
    else:
        (r_ref, k_ref, v_ref, lo_ref, vec_ref, mulo_ref, wup_ref, aup_ref,
         ones_ref, ro_ref, ko_ref, vo_ref, lw_ref, kk_ref, a_ref, cr_ref, ck_ref, cv_ref, cl_ref) = refs

    @pl.when(pl.program_id(1) == 0)
    def _():
        cr_ref[...] = jnp.zeros_like(cr_ref)
        ck_ref[...] = jnp.zeros_like(ck_ref)
        cv_ref[...] = jnp.zeros_like(cv_ref)
        cl_ref[...] = jnp.zeros_like(cl_ref)

    def shift(x_ref, carry_ref, mu):
        x = x_ref[...]
        rows = x.shape[0]
        row = lax.broadcasted_iota(jnp.int32, x.shape, 0)
        prev = jnp.where(row == 0, carry_ref[0:1, :], pltpu.roll(x, 1, 0))
        carry_ref[0:1, :] = x[rows - 1:rows, :]
        return x + (prev - x) * mu

    vec = vec_ref[...]
    r = shift(r_ref, cr_ref, vec[0:1])
    k = shift(k_ref, ck_ref, vec[1:2])
    v = shift(v_ref, cv_ref, vec[2:3])
    lo = shift(lo_ref, cl_ref, mulo_ref[0:1])

    u = vec[3:4] + _dot(jnp.tanh(lo), wup_ref[...])
    lw_ref[...] = -math.exp(-0.5) * _sigmoid(u)
    a = _sigmoid(vec[4:5] + _dot(lo, aup_ref[...]))
    if has_vres:
        gate = _sigmoid(vec[7:8] + _dot(_dot(v, vdn_ref[...]), vup_ref[...]))
        v = v + (vf_ref[...] - v) * gate
    kk = k * vec[5:6]
    ss = _seg_sum(kk * kk, ones_ref[...])
    kk = kk / jnp.maximum(jnp.sqrt(ss), 1e-12)
    ro_ref[...] = r
    ko_ref[...] = k * (1.0 + (a - 1.0) * vec[6:7])
    vo_ref[...] = v
    kk_ref[...] = kk
    a_ref[...] = a


def _rw_prep(p_all, v_first, vec, mulo, wup, aup, vdn, vup, ones64, batch, seq, tr=256):
    n = batch * seq
    nt = seq // tr
    has_vres = v_first is not None

    def rows(c):
        return lambda b, i: (b * nt + i, c)

    const = lambda b, i: (0, 0)
    in_specs = [pl.BlockSpec((tr, 512), rows(COL_RW // 512)),
                pl.BlockSpec((tr, 512), rows(COL_RW // 512 + 1)),
                pl.BlockSpec((tr, 512), rows(COL_RW // 512 + 2)),
                pl.BlockSpec((tr, 128), rows(COL_RW_LORA // 128))]
    args = [p_all, p_all, p_all, p_all]
    if has_vres:
        in_specs.append(pl.BlockSpec((tr, 512), rows(0)))
        args.append(v_first)
    in_specs += [pl.BlockSpec((8, 512), const), pl.BlockSpec((8, 128), const),
                 pl.BlockSpec((128, 512), const), pl.BlockSpec((128, 512), const)]
    args += [vec, mulo, wup, aup]
    if has_vres:
        in_specs += [pl.BlockSpec((512, 128), const), pl.BlockSpec((128, 512), const)]
        args += [vdn, vup]
    in_specs.append(pl.BlockSpec((512, 512), const))
    args.append(ones64)
    out = jax.ShapeDtypeStruct((n, 512), F32)
    return pl.pallas_call(
        functools.partial(_rw_prep_kernel, has_vres),
        grid=(batch, nt),
        in_specs=in_specs,
        out_specs=[pl.BlockSpec((tr, 512), rows(0))] * 6,
        out_shape=[out] * 6,
        scratch_shapes=[pltpu.VMEM((8, 512), F32)] * 3 + [pltpu.VMEM((8, 128), F32)],
        compiler_params=_params("arbitrary", "arbitrary"),
        name="rw_prep",
    )(*args)


def _rw_scan_kernel(r_ref, k_ref, v_ref, lw_ref, kk_ref, a_ref, par_ref, ones_ref, y_ref, s_ref):
    L = RW_CHUNK
    batch = r_ref.shape[0]

    @pl.when(pl.program_id(0) == 0)
    def _():
        s_ref[...] = jnp.zeros_like(s_ref)

    row = lax.broadcasted_iota(jnp.int32, (L, 128), 0)
    lane = lax.broadcasted_iota(jnp.int32, (L, 128), 1)
    col = lane & (RW_HEAD - 1)
    strict = row > col
    incl = row >= col
    eye = (row == col).astype(F32)
    first = lane < RW_HEAD
    r128 = lax.broadcasted_iota(jnp.int32, (128, 128), 0)
    c128 = lax.broadcasted_iota(jnp.int32, (128, 128), 1)
    same_head = (r128 < RW_HEAD) == (c128 < RW_HEAD)
    tri = (lax.broadcasted_iota(jnp.int32, (L, L), 0) >= lax.broadcasted_iota(jnp.int32, (L, L), 1)).astype(BF16)

    def bd(x):
        return jnp.concatenate([jnp.where(first, x, 0.0), jnp.where(first, 0.0, x)], axis=0)

    n_pairs = GROUP_W // 128
    units = [(bi, pr) for bi in range(batch) for pr in range(n_pairs)]
    idx = range(len(units))
    al, rt, bet, kt, beh, kh, p_chunk, vs, rs, ks = [], [], [], [], [], [], [], [], [], []
    for bi in range(batch):
        lw = lw_ref[bi]
        r = r_ref[bi]
        k = k_ref[bi]
        kk = kk_ref[bi]
        b = _dot_exact_lhs(tri, lw)
        b_last = b[L - 1:L, :]
        p_inv = jnp.exp(-b)
        p_end = jnp.exp(b_last - b)
        be = kk * a_ref[bi]
        al.append(-(kk * jnp.exp(b - lw)))
        rt.append(r * jnp.exp(b))
        bet.append(be * p_inv)
        kt.append(k * p_inv)
        beh.append(be * p_end)
        kh.append(k * p_end)
        p_chunk.append(jnp.exp(b_last))
        vs.append(v_ref[bi])
        rs.append(r)
        ks.append(k)

    def part(xs, i):
        bi, pr = units[i]
        return xs[bi][:, 128 * pr:128 * pr + 128]

    lhs = [jnp.concatenate([part(al, i), part(rt, i)], axis=0).astype(BF16) for i in idx]
    rhs = [jnp.concatenate([bd(part(bet, i)), bd(part(kt, i))], axis=0).astype(BF16) for i in idx]
    g = [_dot_nt(lhs[i], rhs[i]) for i in idx]
    a_ab = [jnp.where(strict, g[i][0:L, 0:128], 0.0) for i in idx]
    a_ak = [jnp.where(strict, g[i][0:L, 128:256], 0.0) for i in idx]
    b_rb = [jnp.where(incl, g[i][L:2 * L, 0:128], 0.0) for i in idx]
    b_rk = [jnp.where(incl, g[i][L:2 * L, 128:256], 0.0) for i in idx]
    s = [s_ref[i] for i in idx]
    sa = [_dot_nt(lhs[i], s[i]) for i in idx]
    vv = [_dot(jnp.concatenate([a_ak[i], b_rk[i]], axis=0), bd(part(vs, i))) for i in idx]

    t_inv = [eye + a_ab[i] for i in idx]
    x = [_dot(a_ab[i], bd(a_ab[i])) for i in idx]
    for _ in range(4):
        xt = [_dot(jnp.concatenate([x[i], t_inv[i]], axis=0), bd(x[i])) for i in idx]
        x = [xt[i][0:L] for i in idx]
        t_inv = [t_inv[i] + xt[i][L:2 * L] for i in idx]
    t_inv = [t_inv[i] + _dot(t_inv[i], bd(x[i])) for i in idx]

    u = [_dot(t_inv[i], bd(sa[i][0:L] + vv[i][0:L])) for i in idx]
    ys = [sa[i][L:2 * L] + vv[i][L:2 * L] + _dot(b_rb[i], bd(u[i])) for i in idx]
    for i in idx:
        upd = _dot_tn(jnp.concatenate([u[i], part(vs, i)], axis=0),
                      jnp.concatenate([part(beh, i), part(kh, i)], axis=0))
        s_ref[i] = s[i] * part(p_chunk, i) + jnp.where(same_head, upd, 0.0)

    par = par_ref[...]
    ones = ones_ref[...]
    inv_n = 1.0 / RW_HEAD
    for bi in range(batch):
        y = jnp.concatenate(ys[n_pairs * bi:n_pairs * (bi + 1)], axis=1)
        mu = _seg_sum(y, ones) * inv_n
        yc = y - mu
        var = _seg_sum(yc * yc, ones) * inv_n
        bonus = _seg_sum(rs[bi] * ks[bi] * par[0:1], ones) * vs[bi]
        y_ref[bi] = yc * lax.rsqrt(var + RW_GN_EPS) * par[1:2] + par[2:3] + bonus


def _rw_scan(r, k, v, lw, kk, a, par, ones64, batch, seq):
    nc = seq // RW_CHUNK
    blk = pl.BlockSpec((batch, RW_CHUNK, 512), lambda c: (0, c, 0))
    const = lambda c: (0, 0)
    as3 = lambda t: t.reshape(batch, seq, 512)
    y = pl.pallas_call(
        _rw_scan_kernel,
        grid=(nc,),
        in_specs=[blk] * 6 + [pl.BlockSpec((8, 512), const), pl.BlockSpec((512, 512), const)],
        out_specs=blk,
        out_shape=jax.ShapeDtypeStruct((batch, seq, 512), F32),
        scratch_shapes=[pltpu.VMEM((batch * (GROUP_W // 128), 128, 128), F32)],
        compiler_params=_params("arbitrary"),
        name="rw_scan",
    )(as3(r), as3(k), as3(v), as3(lw), as3(kk), as3(a), par, ones64)
    return y.reshape(batch * seq, 512)


def _ml_kernel(q_ref, k_ref, v_ref, o_ref, g_ref, cwq_ref, cwk_ref, vec_ref, gb_ref, y_ref,
               hq_ref, hk_ref, c_ref, n_ref, m_ref):
    L = ML_CHUNK

    @pl.when(pl.program_id(1) == 0)
    def _():
        hq_ref[...] = jnp.zeros_like(hq_ref)
        hk_ref[...] = jnp.zeros_like(hk_ref)
        c_ref[...] = jnp.zeros_like(c_ref)
        n_ref[...] = jnp.zeros_like(n_ref)
        m_ref[...] = jnp.zeros_like(m_ref)

    vec = vec_ref[...]
    row8 = lax.broadcasted_iota(jnp.int32, (8, GROUP_W), 0)

    def conv(x_ref, halo_ref, w_ref, bias):
        x = x_ref[...]
        w = w_ref[...]
        halo = halo_ref[...]
        acc = x * w[3:4] + bias
        for s in (1, 2, 3):
            xs = pltpu.roll(x, s, 0)
            top = jnp.where(row8 < s, pltpu.roll(halo, s, 0), xs[0:8])
            xs = jnp.concatenate([top, xs[8:]], axis=0)
            acc = acc + xs * w[3 - s:4 - s]
        halo_ref[...] = x[L - 8:L]
        return acc * _sigmoid(acc)

    q = conv(q_ref, hq_ref, cwq_ref, vec[0:1])
    k = conv(k_ref, hk_ref, cwk_ref, vec[1:2]) * (ML_HEAD ** -0.5)
    v = v_ref[...]
    og = _sigmoid(o_ref[...])

    g = g_ref[...] + gb_ref[0:1]
    lane = lax.broadcasted_iota(jnp.int32, (L, 128), 1)
    tri_b = lax.broadcasted_iota(jnp.int32, (L, L), 0) >= lax.broadcasted_iota(jnp.int32, (L, L), 1)
    cum = _dot_exact_lhs(tri_b.astype(BF16), _log_sigmoid(g))
    comb_t = jnp.where(lane < ML_HEADS, g, cum).T

    outs = []
    for h in range(ML_HEADS):
        sl = slice(128 * h, 128 * h + 128)
        qh, kh, vh = q[:, sl], k[:, sl], v[:, sl]
        b_col = cum[:, ML_HEADS + h:ML_HEADS + h + 1]
        ig_col = g[:, h:h + 1]
        b_row = comb_t[ML_HEADS + h:ML_HEADS + h + 1, :]
        ig_row = comb_t[h:h + 1, :]
        dm = jnp.where(tri_b, b_col - b_row + ig_row, -jnp.inf)
        m_prev = m_ref[h:h + 1, 0:1]
        inter = b_col + m_prev
        mj = jnp.maximum(inter, jnp.max(dm, axis=-1, keepdims=True))
        s = _dot_nt(qh, kh) * jnp.exp(dm - mj)
        iw = jnp.exp(inter - mj)
        c_h = c_ref[h]
        n_h = n_ref[h:h + 1, :]
        num = _dot(s, vh) + iw * _dot_nt(qh, c_h)
        den = jnp.sum(s, axis=-1, keepdims=True) + iw * jnp.sum(qh * n_h, axis=-1, keepdims=True)
        hh = num / jnp.maximum(jnp.abs(den), jnp.exp(-mj))

        b_last = b_col[L - 1:L, :]
        g_col = b_last - b_col + ig_col
        m_new = jnp.maximum(b_last + m_prev, jnp.max(g_col, axis=0, keepdims=True))
        a_old = jnp.exp(b_last + m_prev - m_new)
        wg = jnp.exp(g_col - m_new)
        c_ref[h] = a_old * c_h + _dot_tn(vh * wg, kh)
        n_ref[h:h + 1, :] = a_old * n_h + jnp.sum(kh * wg, axis=0, keepdims=True)
        m_ref[h:h + 1, :] = jnp.broadcast_to(m_new, (1, 128))

        hn = hh * lax.rsqrt(jnp.mean(hh * hh, axis=-1, keepdims=True) + NORM_EPS) * vec[2:3, sl]
        outs.append(og[:, sl] * hn)
    y_ref[...] = jnp.concatenate(outs, axis=1)


def _ml(p_all, cwq, cwk, vec, gb, batch, seq):
    n = batch * seq
    nc = seq // ML_CHUNK

    def rows(c):
        return lambda b, i: (b * nc + i, c)

    const = lambda b, i: (0, 0)
    base = COL_ML // 512
    return pl.pallas_call(
        _ml_kernel,
        grid=(batch, nc),
        in_specs=[pl.BlockSpec((ML_CHUNK, 512), rows(base)),
                  pl.BlockSpec((ML_CHUNK, 512), rows(base + 1)),
                  pl.BlockSpec((ML_CHUNK, 512), rows(base + 2)),
                  pl.BlockSpec((ML_CHUNK, 512), rows(base + 3)),
                  pl.BlockSpec((ML_CHUNK, 128), rows(COL_ML_GATE // 128)),
                  pl.BlockSpec((8, 512), const), pl.BlockSpec((8, 512), const),
                  pl.BlockSpec((8, 512), const), pl.BlockSpec((8, 128), const)],
        out_specs=pl.BlockSpec((ML_CHUNK, 512), rows(0)),
        out_shape=jax.ShapeDtypeStruct((n, 512), F32),
        scratch_shapes=[pltpu.VMEM((8, 512), F32), pltpu.VMEM((8, 512), F32),
                        pltpu.VMEM((ML_HEADS, 128, 128), F32), pltpu.VMEM((8, 128), F32),
                        pltpu.VMEM((8, 128), F32)],
        compiler_params=_params("arbitrary", "arbitrary"),
        name="mlstm",
    )(p_all, p_all, p_all, p_all, p_all, cwq, cwk, vec, gb)


def _rope128(x, cos_t, sin_t, half):
    lane = lax.broadcasted_iota(jnp.int32, x.shape, 1)
    seg = 128 if half == MLA_ROPE // 2 else DA_QK
    swapped = jnp.where((lane & (seg - 1)) < half, pltpu.roll(x, 128 - half, 1), pltpu.roll(x, half, 1))
    return x * cos_t + swapped * sin_t


def _mla_prep_kernel(p_ref, cos_ref, sin_ref, vec_ref, wq_ref, wkv_ref, q_ref, k_ref, vt_ref):
    p = p_ref[...]
    vec = vec_ref[...]
    cos_t = cos_ref[...]
    sin_t = sin_ref[...]
    q_lat = p[:, 0:MLA_Q_LORA]
    kv_lat = p[:, MLA_Q_LORA:MLA_Q_LORA + MLA_KV_LORA]
    k_pe = p[:, 640:768]
    qn = q_lat * lax.rsqrt(jnp.mean(q_lat * q_lat, axis=-1, keepdims=True) + NORM_EPS) * vec[0:1, 0:MLA_Q_LORA]
    kvn = kv_lat * lax.rsqrt(jnp.mean(kv_lat * kv_lat, axis=-1, keepdims=True) + NORM_EPS) * vec[1:2, 0:MLA_KV_LORA]
    q4 = _dot(qn, wq_ref[...])
    kv4 = _dot(kvn, wkv_ref[...])
    pe_ss = jnp.sum(k_pe * k_pe, axis=-1, keepdims=True)
    qs, ks = [], []
    for h in range(MLA_HEADS):
        qh = q4[:, MLA_PAD * h:MLA_PAD * (h + 1)]
        inv = lax.rsqrt(jnp.sum(qh * qh, axis=-1, keepdims=True) * (1.0 / MLA_QK) + NORM_EPS)
        qh = qh * inv * vec[2:3, MLA_PAD * h:MLA_PAD * (h + 1)]
        qs += [qh[:, 0:128], _rope128(qh[:, 128:256], cos_t, sin_t, MLA_ROPE // 2)]
        kn = kv4[:, 128 * h:128 * (h + 1)]
        inv = lax.rsqrt((jnp.sum(kn * kn, axis=-1, keepdims=True) + pe_ss) * (1.0 / MLA_QK) + NORM_EPS)
        ks += [kn * inv * vec[3:4, 0:128], _rope128(k_pe * inv * vec[4:5, 0:128], cos_t, sin_t, MLA_ROPE // 2)]
    q_ref[...] = jnp.concatenate(qs, axis=1).astype(BF16)
    k_ref[...] = jnp.concatenate(ks, axis=1).astype(BF16)
    _store_vt(vt_ref, kv4[:, 512:1024])


def _store_vt(vt_ref, v):
    heads, _, rows = vt_ref.shape
    vt = v.T.reshape(heads, 128, rows)
    vt_ref[...] = jnp.concatenate([vt, jnp.ones((heads, DV_AUG - 128, rows), F32)], axis=1).astype(BF16)


def _vt_spec(heads, nt):
    return pl.BlockSpec((None, heads, None, DV_AUG, ATT_BLOCK), lambda i: (i // nt, 0, i % nt, 0, 0))


def _mla_prep(p_all, cos_t, sin_t, vec, wq, wkv, batch, seq):
    n = batch * seq
    tr = ATT_BLOCK
    nt = seq // tr
    const = lambda i: (0, 0)
    return pl.pallas_call(
        _mla_prep_kernel,
        grid=(n // tr,),
        in_specs=[pl.BlockSpec((tr, 768), lambda i: (i, COL_MLA // 768)),
                  pl.BlockSpec((tr, 128), lambda i: (i % nt, 0)),
                  pl.BlockSpec((tr, 128), lambda i: (i % nt, 0)),
                  pl.BlockSpec((8, 1024), const),
                  pl.BlockSpec((MLA_Q_LORA, 1024), const),
                  pl.BlockSpec((MLA_KV_LORA, 1024), const)],
        out_specs=[pl.BlockSpec((tr, 1024), lambda i: (i, 0)),
                   pl.BlockSpec((tr, 1024), lambda i: (i, 0)),
                   _vt_spec(MLA_HEADS, nt)],
        out_shape=[jax.ShapeDtypeStruct((n, 1024), BF16), jax.ShapeDtypeStruct((n, 1024), BF16),
                   jax.ShapeDtypeStruct((batch, MLA_HEADS, nt, DV_AUG, tr), BF16)],
        compiler_params=_params("arbitrary"),
        name="mla_prep",
    )(p_all, cos_t, sin_t, vec, wq, wkv)


def _attn_block_t(kbs, vts, units, masked):
    sub = ATT_SUB
    s, cols = [], []
    for q, _, _, c0, h in units:
        keys = c0 + sub if masked else kbs[h].shape[0]
        st = _dot_nt(kbs[h][0:keys], q)
        if masked:
            keep = (lax.broadcasted_iota(jnp.int32, st.shape, 0)
                    <= lax.broadcasted_iota(jnp.int32, st.shape, 1) + c0)
            st = jnp.where(keep, st, -jnp.inf)
        s.append(st)
        cols.append(slice(c0, c0 + sub))
    idx = range(len(units))
    m_old = [units[i][1][:, cols[i]] for i in idx]
    m_new = [jnp.maximum(m_old[i], jnp.max(s[i], axis=0, keepdims=True)) for i in idx]
    p = [jnp.exp2(s[i] - m_new[i]).astype(BF16) for i in idx]
    alpha = [jnp.exp2(m_old[i] - m_new[i]) for i in idx]
    for i in idx:
        _, m_ref, acc_ref, _, h = units[i]
        pv = jnp.dot(vts[h][:, 0:s[i].shape[0]], p[i], preferred_element_type=F32)
        acc_ref[:, cols[i]] = alpha[i] * acc_ref[:, cols[i]] + pv
        m_ref[:, cols[i]] = m_new[i]


def _init_stats(m_ref, acc_ref):
    m_ref[...] = jnp.full_like(m_ref, -jnp.inf)
    acc_ref[...] = jnp.zeros_like(acc_ref)


def _attn_sweep(i, k_ref, vt_ref, dk, units):
    tb = ATT_BLOCK
    heads = vt_ref.shape[0]

    def block(j, masked):
        start = pl.multiple_of(j * tb, tb)
        kbs = [k_ref[pl.ds(start, tb), dk * h:dk * (h + 1)] for h in range(heads)]
        _attn_block_t(kbs, [vt_ref[h, j] for h in range(heads)], units, masked)

    def body(j, carry):
        block(j, False)
        return carry

    lax.fori_loop(0, i, body, 0)
    block(i, True)


def _attn_result(acc_ref):
    acc = acc_ref[...]
    return acc[0:128] / acc[128:129]


def _attn_specs(dk, nq, seq, heads):
    tb = ATT_BLOCK
    return [pl.BlockSpec((tb, dk * heads), lambda b, h, i: (b * nq + i, h)),
            pl.BlockSpec((seq, dk * heads), lambda b, h, i: (b, h)),
            pl.BlockSpec((None, heads, nq, DV_AUG, tb), lambda b, h, i: (b, h, 0, 0, 0))]


def _mla_attn_kernel(q_ref, k_ref, vt_ref, o_ref, *stats):
    heads = vt_ref.shape[0]
    units = []
    for h in range(heads):
        m_ref, acc_ref = stats[2 * h:2 * h + 2]
        _init_stats(m_ref, acc_ref)
        units += [(q_ref[c0:c0 + ATT_SUB, MLA_PAD * h:MLA_PAD * (h + 1)], m_ref, acc_ref, c0, h)
                  for c0 in range(0, ATT_BLOCK, ATT_SUB)]
    _attn_sweep(pl.program_id(2), k_ref, vt_ref, MLA_PAD, units)
    o_ref[...] = jnp.concatenate([_attn_result(stats[2 * h + 1]).T for h in range(heads)], axis=1)


def _mla_attn(q, k, vt, batch, seq):
    n = batch * seq
    tb = ATT_BLOCK
    nq = seq // tb
    return pl.pallas_call(
        _mla_attn_kernel,
        grid=(batch, MLA_HEADS // MLA_ATT_HEADS, nq),
        in_specs=_attn_specs(MLA_PAD, nq, seq, MLA_ATT_HEADS),
        out_specs=pl.BlockSpec((tb, 128 * MLA_ATT_HEADS), lambda b, h, i: (b * nq + i, h)),
        out_shape=jax.ShapeDtypeStruct((n, 512), F32),
        scratch_shapes=[pltpu.VMEM((1, tb), F32), pltpu.VMEM((DV_AUG, tb), F32)] * MLA_ATT_HEADS,
        compiler_params=_params("arbitrary", "arbitrary", "arbitrary"),
        name="mla_attn",
    )(q, k, vt)


def _da_prep_kernel(q_ref, k_ref, v_ref, cos_ref, sin_ref, vec_ref, ones_ref, qo_ref, ko_ref, vo_ref):
    vec = vec_ref[...]
    ones = ones_ref[...]
    cos_t = cos_ref[...]
    sin_t = sin_ref[...]

    def norm_rope(x, gain):
        ss = _seg_sum(x * x, ones) * (1.0 / DA_QK)
        x = x * lax.rsqrt(ss + NORM_EPS) * gain
        parts = [_rope128(x[:, 128 * c:128 * (c + 1)], cos_t, sin_t, DA_ROT // 2) for c in range(4)]
        return jnp.concatenate(parts, axis=1)

    qo_ref[...] = norm_rope(q_ref[...], vec[0:1]).astype(BF16)
    ko_ref[...] = norm_rope(k_ref[...], vec[1:2]).astype(BF16)
    _store_vt(vo_ref, v_ref[...])


def _da_prep(p_all, cos_t, sin_t, vec, ones64, batch, seq):
    n = batch * seq
    tr = ATT_BLOCK
    nt = seq // tr
    const = lambda i: (0, 0)
    base = COL_DA // 512
    out = jax.ShapeDtypeStruct((n, 512), BF16)
    return pl.pallas_call(
        _da_prep_kernel,
        grid=(n // tr,),
        in_specs=[pl.BlockSpec((tr, 512), lambda i: (i, base)),
                  pl.BlockSpec((tr, 512), lambda i: (i, base + 1)),
                  pl.BlockSpec((tr, 512), lambda i: (i, base + 2)),
                  pl.BlockSpec((tr, 128), lambda i: (i % nt, 0)),
                  pl.BlockSpec((tr, 128), lambda i: (i % nt, 0)),
                  pl.BlockSpec((8, 512), const),
                  pl.BlockSpec((512, 512), const)],
        out_specs=[pl.BlockSpec((tr, 512), lambda i: (i, 0))] * 2 + [_vt_spec(DA_HEADS, nt)],
        out_shape=[out, out, jax.ShapeDtypeStruct((batch, DA_HEADS, nt, DV_AUG, tr), BF16)],
        compiler_params=_params("arbitrary"),
        name="da_prep",
    )(p_all, p_all, p_all, cos_t, sin_t, vec, ones64)


def _da_attn_kernel(lam_init, q_ref, k_ref, vt_ref, lam_ref, g_ref, o_ref, *stats):
    heads = vt_ref.shape[0]
    lane = lax.broadcasted_iota(jnp.int32, (ATT_BLOCK, 128), 1)
    units = []
    for h in range(heads):
        q = q_ref[:, 128 * h:128 * (h + 1)]
        for mp in range(2):
            qm = jnp.where((lane < DA_QK) == (mp == 0), q, jnp.zeros_like(q))
            m_ref, acc_ref = stats[4 * h + 2 * mp:4 * h + 2 * mp + 2]
            _init_stats(m_ref, acc_ref)
            units += [(qm[c0:c0 + ATT_SUB], m_ref, acc_ref, c0, h) for c0 in range(0, ATT_BLOCK, ATT_SUB)]
    _attn_sweep(pl.program_id(2), k_ref, vt_ref, 128, units)

    lv = lam_ref[...]
    lam = (jnp.exp(jnp.sum(lv[0:1] * lv[1:2], axis=-1, keepdims=True))
           - jnp.exp(jnp.sum(lv[2:3] * lv[3:4], axis=-1, keepdims=True)) + lam_init)
    outs = []
    for h in range(heads):
        o = (_attn_result(stats[4 * h + 1]) - lam * _attn_result(stats[4 * h + 3])).T
        o = o * lax.rsqrt(jnp.mean(o * o, axis=-1, keepdims=True) + NORM_EPS) * g_ref[0:1]
        outs.append(o * (1.0 - lam_init))
    o_ref[...] = jnp.concatenate(outs, axis=1)


def _da_attn(q, k, vt, lam_vec, sub_g, lam_init, batch, seq):
    n = batch * seq
    tb = ATT_BLOCK
    nq = seq // tb
    const = lambda b, h, i: (0, 0)
    row = pltpu.VMEM((1, tb), F32)
    acc = pltpu.VMEM((DV_AUG, tb), F32)
    return pl.pallas_call(
        functools.partial(_da_attn_kernel, lam_init),
        grid=(batch, DA_HEADS // DA_ATT_HEADS, nq),
        in_specs=_attn_specs(128, nq, seq, DA_ATT_HEADS)
        + [pl.BlockSpec((8, 128), const), pl.BlockSpec((8, 128), const)],
        out_specs=pl.BlockSpec((tb, 128 * DA_ATT_HEADS), lambda b, h, i: (b * nq + i, h)),
        out_shape=jax.ShapeDtypeStruct((n, 512), F32),
        scratch_shapes=[row, acc] * (2 * DA_ATT_HEADS),
        compiler_params=_params("arbitrary", "arbitrary", "arbitrary"),
        name="da_attn",
    )(q, k, vt, lam_vec, sub_g)


def _outproj_kernel(ya_ref, yb_ref, yc_ref, yd_ref, z_ref, beta_ref, x_ref, w_ref, o_ref, g_ref):
    @pl.when(pl.program_id(1) == 0)
    def _():
        z = z_ref[...]
        y = jnp.concatenate([ya_ref[...], yb_ref[...], yc_ref[...], yd_ref[...]], axis=1) * beta_ref[0:1]
        g_ref[...] = (y * (z * _sigmoid(z))).astype(BF16)

    o_ref[...] = x_ref[...] + jnp.dot(g_ref[...], w_ref[...], preferred_element_type=F32)


def _outproj(ys, p_all, beta_row, x2, w, tm=512, tn=1024):
    n = x2.shape[0]
    yspec = pl.BlockSpec((tm, 512), lambda i, j: (i, 0))
    return pl.pallas_call(
        _outproj_kernel,
        grid=(n // tm, D_MODEL // tn),
        in_specs=[yspec] * 4 + [pl.BlockSpec((tm, D_MODEL), lambda i, j: (i, COL_Z // D_MODEL))]
        + [pl.BlockSpec((8, D_MODEL), lambda i, j: (0, 0)),
           pl.BlockSpec((tm, tn), lambda i, j: (i, j)),
           pl.BlockSpec((D_MODEL, tn), lambda i, j: (0, j))],
        out_specs=pl.BlockSpec((tm, tn), lambda i, j: (i, j)),
        out_shape=jax.ShapeDtypeStruct((n, D_MODEL), F32),
        scratch_shapes=[pltpu.VMEM((tm, D_MODEL), BF16)],
        compiler_params=_params("arbitrary", "arbitrary"),
        name="outproj",
    )(*ys, p_all, beta_row, x2, w)


def _rows(rows, width, n_rows=8):
    out = jnp.zeros((n_rows, width), F32)
    for i, r in enumerate(rows):
        out = out.at[i, :r.shape[0]].set(r.astype(F32))
    return out


def _pad_to(a, shape):
    return jnp.zeros(shape, a.dtype).at[tuple(slice(0, s) for s in a.shape)].set(a)


def _arrange_w_in(w):
    w = w.astype(BF16)
    zeros = lambda c: jnp.zeros((D_MODEL, c), BF16)
    return jnp.concatenate([
        w[:, 5704:6408], zeros(64),
        w[:, 3584:3648], zeros(64),
        w[:, 5696:5704], zeros(120),
        w[:, 2048:3584],
        w[:, 3648:5696],
        w[:, 6408:7944],
        w[:, 0:2048],
    ], axis=1)


def _rope_tables(seq, rot, seg, fill):
    half = rot // 2
    inv = ROPE_THETA ** (-jnp.arange(0, rot, 2, dtype=F32) / rot)
    ang = jnp.arange(seq, dtype=F32)[:, None] * inv[None, :]
    cos, sin = jnp.cos(ang), jnp.sin(ang)
    rest = seg - rot
    cos_s = jnp.concatenate([cos, cos, jnp.full((seq, rest), fill, F32)], axis=1)
    sin_s = jnp.concatenate([-sin, sin, jnp.zeros((seq, rest), F32)], axis=1)
    reps = 128 // seg
    return jnp.tile(cos_s, (1, reps)), jnp.tile(sin_s, (1, reps))


def kernel(x, norm_g, w_in, w_out, branch_beta, rw_mu, rw_w0, rw_w_up, rw_a0, rw_a_up, rw_v0, rw_v_dn, rw_v_up, rw_k_k, rw_k_a, rw_r_k, rw_ln_w, rw_ln_b, ml_conv_w, ml_conv_b, ml_i_b, ml_f_b, ml_norm_g, mla_q_norm_g, mla_kv_norm_g, mla_w_q_b, mla_w_kv_b, mla_q_g, mla_k_g, da_q_g, da_k_g, da_lq1, da_lk1, da_lq2, da_lk2, da_sub_g):
    batch, seq, _ = x.shape
    depth = w_in.shape[0]
    n = batch * seq
    x2 = x.reshape(n, D_MODEL)

    seg_id = jnp.arange(GROUP_W) // RW_HEAD
    ones64 = (seg_id[:, None] == seg_id[None, :]).astype(BF16)
    mla_cos, mla_sin = _rope_tables(seq, MLA_ROPE, 128, 0.0)
    da_cos, da_sin = _rope_tables(seq, DA_ROT, DA_QK, 1.0)

    v_first = None
    for l in range(depth):
        p_all = _inproj(x2, norm_g[l][None, :], _arrange_w_in(w_in[l]))

        mu = rw_mu[l]
        vec = _rows([mu[0:512], mu[512:1024], mu[1024:1536], rw_w0[l], rw_a0[l], rw_k_k[l], rw_k_a[l]]
                    + ([rw_v0[l - 1]] if l > 0 else []), 512)
        mulo = _rows([mu[1536:1600]], 128)
        wup = _pad_to(rw_w_up[l], (128, 512)).astype(BF16)
        aup = jnp.zeros((128, 512), F32).at[32:64].set(rw_a_up[l]).astype(BF16)
        if l > 0:
            vdn = _pad_to(rw_v_dn[l - 1], (512, 128)).astype(BF16)
            vup = _pad_to(rw_v_up[l - 1], (128, 512)).astype(BF16)
        else:
            vdn = vup = None
        r, k, v, lw, kk, a = _rw_prep(p_all, v_first, vec, mulo, wup, aup, vdn, vup, ones64, batch, seq)
        if l == 0:
            v_first = v
        par = _rows([rw_r_k[l].reshape(-1), rw_ln_w[l], rw_ln_b[l]], 512)
        y_rw = _rw_scan(r, k, v, lw, kk, a, par, ones64, batch, seq)

        cw = ml_conv_w[l]
        cb = ml_conv_b[l]
        ml_vec = _rows([cb[0:512], cb[512:1024], ml_norm_g[l]], 512)
        gb = _rows([jnp.concatenate([ml_i_b[l], ml_f_b[l]])], 128)
        y_ml = _ml(p_all, _pad_to(cw[:, 0:512], (8, 512)), _pad_to(cw[:, 512:1024], (8, 512)),
                   ml_vec, gb, batch, seq)

        scale = MLA_QK ** -0.5 * LOG2E
        qg = jnp.tile(jnp.concatenate([mla_q_g[l] * scale, jnp.zeros((64,), F32)]), MLA_HEADS)
        mla_vec = _rows([mla_q_norm_g[l], mla_kv_norm_g[l], qg, mla_k_g[l][0:128], mla_k_g[l][128:192]], 1024)
        wq = _pad_to(mla_w_q_b[l].reshape(MLA_Q_LORA, MLA_HEADS, MLA_QK),
                     (MLA_Q_LORA, MLA_HEADS, MLA_PAD)).reshape(MLA_Q_LORA, 1024).astype(BF16)
        wkv = mla_w_kv_b[l].reshape(MLA_KV_LORA, MLA_HEADS, 2, 128).transpose(0, 2, 1, 3)
        wkv = wkv.reshape(MLA_KV_LORA, 1024).astype(BF16)
        mq, mk, mv = _mla_prep(p_all, mla_cos, mla_sin, mla_vec, wq, wkv, batch, seq)
        y_mla = _mla_attn(mq, mk, mv, batch, seq)

        da_vec = _rows([jnp.tile(da_q_g[l], 8) * (DA_QK ** -0.5 * LOG2E), jnp.tile(da_k_g[l], 8)], 512)
        dq, dk, dv = _da_prep(p_all, da_cos, da_sin, da_vec, ones64, batch, seq)
        lam_vec = _rows([da_lq1[l], da_lk1[l], da_lq2[l], da_lk2[l]], 128)
        lam_init = 0.8 - 0.6 * math.exp(-0.3 * l)
        y_da = _da_attn(dq, dk, dv, lam_vec, _rows([da_sub_g[l]], 128), lam_init, batch, seq)

        beta_row = _rows([jnp.repeat(branch_beta[l], GROUP_W)], D_MODEL)
        x2 = _outproj((y_rw, y_ml, y_mla, y_da), p_all, beta_row, x2, w_out[l].astype(BF16))
    return x2.reshape(batch, seq, D_MODEL)
```

```python
import functools
import math

import jax
import jax.numpy as jnp
from jax import lax
from jax.experimental import pallas as pl
from jax.experimental.pallas import tpu as pltpu

F32 = jnp.float32
BF16 = jnp.bfloat16

D_MODEL = 2048
GROUP_W = 512
NORM_EPS = 1e-6
ROPE_THETA = 500000.0

RW_HEAD = 64
RW_GN_EPS = 64e-5
RW_CHUNK = 64
ML_HEADS = 4
ML_HEAD = 128
ML_CHUNK = 256
MLA_HEADS = 4
MLA_NOPE = 128
MLA_ROPE = 64
MLA_QK = 192
MLA_Q_LORA = 384
MLA_KV_LORA = 256
MLA_PAD = 256
DA_HEADS = 4
DA_QK = 64
DA_ROT = 16
ATT_BLOCK = 512
ATT_SUB = 256
MLA_ATT_HEADS = 4
DA_ATT_HEADS = 2
DV_AUG = 144
LOG2E = math.log2(math.e)

N_PROJ = 8192
COL_MLA = 0
COL_RW_LORA = 768
COL_ML_GATE = 896
COL_RW = 1024
COL_ML = 2560
COL_DA = 4608
COL_Z = 6144

INPROJ_TN = 1024

VMEM_LIMIT = 48 * 1024 * 1024


def _dot(a, b):
    return jnp.dot(a.astype(BF16), b.astype(BF16), preferred_element_type=F32)


def _dot_nt(a, b):
    return lax.dot_general(a.astype(BF16), b.astype(BF16), (((1,), (1,)), ((), ())),
                           preferred_element_type=F32)


def _dot_tn(a, b):
    return lax.dot_general(a.astype(BF16), b.astype(BF16), (((0,), (0,)), ((), ())),
                           preferred_element_type=F32)


def _split3(x):
    h1 = x.astype(BF16)
    r1 = x - h1.astype(F32)
    h2 = r1.astype(BF16)
    h3 = (r1 - h2.astype(F32)).astype(BF16)
    return h1, h2, h3


def _dot_exact_lhs(a_bf16, x):
    h1, h2, h3 = _split3(x)
    f = functools.partial(jnp.dot, preferred_element_type=F32)
    return f(a_bf16, h1) + f(a_bf16, h2) + f(a_bf16, h3)


def _seg_sum(x, ones_bf16):
    hi = x.astype(BF16)
    lo = (x - hi.astype(F32)).astype(BF16)
    f = functools.partial(jnp.dot, preferred_element_type=F32)
    return f(hi, ones_bf16) + f(lo, ones_bf16)


def _sigmoid(x):
    return 1.0 / (1.0 + jnp.exp(-x))


def _log_sigmoid(x):
    return jnp.minimum(x, 0.0) - jnp.log(1.0 + jnp.exp(-jnp.abs(x)))


def _params(*sem):
    return pltpu.CompilerParams(dimension_semantics=sem, vmem_limit_bytes=VMEM_LIMIT)


def _inproj_kernel(x_ref, g_ref, w_ref, o_ref, h_ref):
    @pl.when(pl.program_id(1) == 0)
    def _():
        x = x_ref[...]
        ms = jnp.mean(x * x, axis=-1, keepdims=True)
        h_ref[...] = (x * lax.rsqrt(ms + NORM_EPS) * g_ref[...]).astype(BF16)

    o_ref[...] = jnp.dot(h_ref[...], w_ref[...], preferred_element_type=F32)


def _inproj(x2, g, w, tm=1024):
    n = x2.shape[0]
    tn = w.shape[2]
    return pl.pallas_call(
        _inproj_kernel,
        grid=(n // tm, N_PROJ // tn),
        in_specs=[pl.BlockSpec((tm, D_MODEL), lambda i, j: (i, 0)),
                  pl.BlockSpec((1, D_MODEL), lambda i, j: (0, 0)),
                  pl.BlockSpec((None, D_MODEL, tn), lambda i, j: (j, 0, 0))],
        out_specs=pl.BlockSpec((tm, tn), lambda i, j: (i, j)),
        out_shape=jax.ShapeDtypeStruct((n, N_PROJ), F32),
        scratch_shapes=[pltpu.VMEM((tm, D_MODEL), BF16)],
        compiler_params=_params("arbitrary", "arbitrary"),
        name="inproj",
    )(x2, g, w)


def _rw_prep_kernel(has_vres, *refs):
    if has_vres:
        (r_ref, k_ref, v_ref, lo_ref, vf_ref, vec_ref, mulo_ref, wup_ref, aup_ref, vdn_ref, vup_ref,
         ones_ref, ro_ref, ko_ref, vo_ref, lw_ref, kk_ref, a_ref, cr_ref, ck_ref, cv_ref, cl_ref) = refs
    else:
        (r_ref, k_ref, v_ref, lo_ref, vec_ref, mulo_ref, wup_ref, aup_ref,
         ones_ref, ro_ref, ko_ref, vo_ref, lw_ref, kk_ref, a_ref, cr_ref, ck_ref, cv_ref, cl_ref) = refs

    @pl.when(pl.program_id(1) == 0)
    def _():
        cr_ref[...] = jnp.zeros_like(cr_ref)
        ck_ref[...] = jnp.zeros_like(ck_ref)
        cv_ref[...] = jnp.zeros_like(cv_ref)
        cl_ref[...] = jnp.zeros_like(cl_ref)

    def shift(x_ref, carry_ref, mu):
        x = x_ref[...]
        rows = x.shape[0]
        row = lax.broadcasted_iota(jnp.int32, x.shape, 0)
        prev = jnp.where(row == 0, carry_ref[0:1, :], pltpu.roll(x, 1, 0))
        carry_ref[0:1, :] = x[rows - 1:rows, :]
        return x + (prev - x) * mu

    vec = vec_ref[...]
    r = shift(r_ref, cr_ref, vec[0:1])
    k = shift(k_ref, ck_ref, vec[1:2])
    v = shift(v_ref, cv_ref, vec[2:3])
    lo = shift(lo_ref, cl_ref, mulo_ref[0:1])

    u = vec[3:4] + _dot(jnp.tanh(lo), wup_ref[...])
    lw_ref[...] = -math.exp(-0.5) * _sigmoid(u)
    a = _sigmoid(vec[4:5] + _dot(lo, aup_ref[...]))
    if has_vres:
        gate = _sigmoid(vec[7:8] + _dot(_dot(v, vdn_ref[...]), vup_ref[...]))
        v = v + (vf_ref[...] - v) * gate
    kk = k * vec[5:6]
    ss = _seg_sum(kk * kk, ones_ref[...])
    kk = kk / jnp.maximum(jnp.sqrt(ss), 1e-12)
    ro_ref[...] = r
    ko_ref[...] = k * (1.0 + (a - 1.0) * vec[6:7])
    vo_ref[...] = v
    kk_ref[...] = kk
    a_ref[...] = a


def _rw_prep(p_all, v_first, vec, mulo, wup, aup, vdn, vup, ones64, batch, seq, tr=256):
    n = batch * seq
    nt = seq // tr
    has_vres = v_first is not None

    def rows(c):
        return lambda b, i: (b * nt + i, c)

    const = lambda b, i: (0, 0)
    in_specs = [pl.BlockSpec((tr, 512), rows(COL_RW // 512)),
                pl.BlockSpec((tr, 512), rows(COL_RW // 512 + 1)),
                pl.BlockSpec((tr, 512), rows(COL_RW // 512 + 2)),
                pl.BlockSpec((tr, 128), rows(COL_RW_LORA // 128))]
    args = [p_all, p_all, p_all, p_all]
    if has_vres:
        in_specs.append(pl.BlockSpec((tr, 512), rows(0)))
        args.append(v_first)
    in_specs += [pl.BlockSpec((8, 512), const), pl.BlockSpec((8, 128), const),
                 pl.BlockSpec((128, 512), const), pl.BlockSpec((128, 512), const)]
    args += [vec, mulo, wup, aup]
    if has_vres:
        in_specs += [pl.BlockSpec((512, 128), const), pl.BlockSpec((128, 512), const)]
        args += [vdn, vup]
    in_specs.append(pl.BlockSpec((512, 512), const))
    args.append(ones64)
    out = jax.ShapeDtypeStruct((n, 512), F32)
    return pl.pallas_call(
        functools.partial(_rw_prep_kernel, has_vres),
        grid=(batch, nt),
        in_specs=in_specs,
        out_specs=[pl.BlockSpec((tr, 512), rows(0))] * 6,
        out_shape=[out] * 6,
        scratch_shapes=[pltpu.VMEM((8, 512), F32)] * 3 + [pltpu.VMEM((8, 128), F32)],
        compiler_params=_params("arbitrary", "arbitrary"),
        name="rw_prep",
    )(*args)


def _rw_scan_kernel(r_ref, k_ref, v_ref, lw_ref, kk_ref, a_ref, par_ref, ones_ref, y_ref, s_ref):
    L = RW_CHUNK
    batch = r_ref.shape[0]

    @pl.when(pl.program_id(0) == 0)
    def _():
        s_ref[...] = jnp.zeros_like(s_ref)

    row = lax.broadcasted_iota(jnp.int32, (L, 128), 0)
    lane = lax.broadcasted_iota(jnp.int32, (L, 128), 1)
    col = lane & (RW_HEAD - 1)
    strict = row > col
    incl = row >= col
    eye = (row == col).astype(F32)
    first = lane < RW_HEAD
    r128 = lax.broadcasted_iota(jnp.int32, (128, 128), 0)
    c128 = lax.broadcasted_iota(jnp.int32, (128, 128), 1)
    same_head = (r128 < RW_HEAD) == (c128 < RW_HEAD)
    tri = (lax.broadcasted_iota(jnp.int32, (L, L), 0) >= lax.broadcasted_iota(jnp.int32, (L, L), 1)).astype(BF16)

    def bd(x):
        return jnp.concatenate([jnp.where(first, x, 0.0), jnp.where(first, 0.0, x)], axis=0)

    n_pairs = GROUP_W // 128
    units = [(bi, pr) for bi in range(batch) for pr in range(n_pairs)]
    idx = range(len(units))
    al, rt, bet, kt, beh, kh, p_chunk, vs, rs, ks = [], [], [], [], [], [], [], [], [], []
    for bi in range(batch):
        lw = lw_ref[bi]
        r = r_ref[bi]
        k = k_ref[bi]
        kk = kk_ref[bi]
        b = _dot_exact_lhs(tri, lw)
        b_last = b[L - 1:L, :]
        p_inv = jnp.exp(-b)
        p_end = jnp.exp(b_last - b)
        be = kk * a_ref[bi]
        al.append(-(kk * jnp.exp(b - lw)))
        rt.append(r * jnp.exp(b))
        bet.append(be * p_inv)
        kt.append(k * p_inv)
        beh.append(be * p_end)
        kh.append(k * p_end)
        p_chunk.append(jnp.exp(b_last))
        vs.append(v_ref[bi])
        rs.append(r)
        ks.append(k)

    def part(xs, i):
        bi, pr = units[i]
        return xs[bi][:, 128 * pr:128 * pr + 128]

    lhs = [jnp.concatenate([part(al, i), part(rt, i)], axis=0).astype(BF16) for i in idx]
    rhs = [jnp.concatenate([bd(part(bet, i)), bd(part(kt, i))], axis=0).astype(BF16) for i in idx]
    g = [_dot_nt(lhs[i], rhs[i]) for i in idx]
    a_ab = [jnp.where(strict, g[i][0:L, 0:128], 0.0) for i in idx]
    a_ak = [jnp.where(strict, g[i][0:L, 128:256], 0.0) for i in idx]
    b_rb = [jnp.where(incl, g[i][L:2 * L, 0:128], 0.0) for i in idx]
    b_rk = [jnp.where(incl, g[i][L:2 * L, 128:256], 0.0) for i in idx]
    s = [s_ref[i] for i in idx]
    sa = [_dot_nt(lhs[i], s[i]) for i in idx]
    vv = [_dot(jnp.concatenate([a_ak[i], b_rk[i]], axis=0), bd(part(vs, i))) for i in idx]

    t_inv = [eye + a_ab[i] for i in idx]
    x = [_dot(a_ab[i], bd(a_ab[i])) for i in idx]
    for _ in range(4):
        xt = [_dot(jnp.concatenate([x[i], t_inv[i]], axis=0), bd(x[i])) for i in idx]
        x = [xt[i][0:L] for i in idx]
        t_inv = [t_inv[i] + xt[i][L:2 * L] for i in idx]
    t_inv = [t_inv[i] + _dot(t_inv[i], bd(x[i])) for i in idx]

    u = [_dot(t_inv[i], bd(sa[i][0:L] + vv[i][0:L])) for i in idx]
    ys = [sa[i][L:2 * L] + vv[i][L:2 * L] + _dot(b_rb[i], bd(u[i])) for i in idx]
    for i in idx:
        upd = _dot_tn(jnp.concatenate([u[i], part(vs, i)], axis=0),
                      jnp.concatenate([part(beh, i), part(kh, i)], axis=0))
        s_ref[i] = s[i] * part(p_chunk, i) + jnp.where(same_head, upd, 0.0)

    par = par_ref[...]
    ones = ones_ref[...]
    inv_n = 1.0 / RW_HEAD
    for bi in range(batch):
        y = jnp.concatenate(ys[n_pairs * bi:n_pairs * (bi + 1)], axis=1)
        mu = _seg_sum(y, ones) * inv_n
        yc = y - mu
        var = _seg_sum(yc * yc, ones) * inv_n
        bonus = _seg_sum(rs[bi] * ks[bi] * par[0:1], ones) * vs[bi]
        y_ref[bi] = yc * lax.rsqrt(var + RW_GN_EPS) * par[1:2] + par[2:3] + bonus


def _rw_scan(r, k, v, lw, kk, a, par, ones64, batch, seq):
    nc = seq // RW_CHUNK
    blk = pl.BlockSpec((batch, RW_CHUNK, 512), lambda c: (0, c, 0))
    const = lambda c: (0, 0)
    as3 = lambda t: t.reshape(batch, seq, 512)
    y = pl.pallas_call(
        _rw_scan_kernel,
        grid=(nc,),
        in_specs=[blk] * 6 + [pl.BlockSpec((8, 512), const), pl.BlockSpec((512, 512), const)],
        out_specs=blk,
        out_shape=jax.ShapeDtypeStruct((batch, seq, 512), F32),
        scratch_shapes=[pltpu.VMEM((batch * (GROUP_W // 128), 128, 128), F32)],
        compiler_params=_params("arbitrary"),
        name="rw_scan",
    )(as3(r), as3(k), as3(v), as3(lw), as3(kk), as3(a), par, ones64)
    return y.reshape(batch * seq, 512)


def _ml_kernel(q_ref, k_ref, v_ref, o_ref, g_ref, cwq_ref, cwk_ref, vec_ref, gb_ref, y_ref,
               hq_ref, hk_ref, c_ref, n_ref, m_ref):
    L = ML_CHUNK

    @pl.when(pl.program_id(1) == 0)
    def _():
        hq_ref[...] = jnp.zeros_like(hq_ref)
        hk_ref[...] = jnp.zeros_like(hk_ref)
        c_ref[...] = jnp.zeros_like(c_ref)
        n_ref[...] = jnp.zeros_like(n_ref)
        m_ref[...] = jnp.zeros_like(m_ref)

    vec = vec_ref[...]
    row8 = lax.broadcasted_iota(jnp.int32, (8, GROUP_W), 0)

    def conv(x_ref, halo_ref, w_ref, bias):
        x = x_ref[...]
        w = w_ref[...]
        halo = halo_ref[...]
        acc = x * w[3:4] + bias
        for s in (1, 2, 3):
            xs = pltpu.roll(x, s, 0)
            top = jnp.where(row8 < s, pltpu.roll(halo, s, 0), xs[0:8])
            xs = jnp.concatenate([top, xs[8:]], axis=0)
            acc = acc + xs * w[3 - s:4 - s]
        halo_ref[...] = x[L - 8:L]
        return acc * _sigmoid(acc)

    q = conv(q_ref, hq_ref, cwq_ref, vec[0:1])
    k = conv(k_ref, hk_ref, cwk_ref, vec[1:2]) * (ML_HEAD ** -0.5)
    v = v_ref[...]
    og = _sigmoid(o_ref[...])

    g = g_ref[...] + gb_ref[0:1]
    lane = lax.broadcasted_iota(jnp.int32, (L, 128), 1)
    tri_b = lax.broadcasted_iota(jnp.int32, (L, L), 0) >= lax.broadcasted_iota(jnp.int32, (L, L), 1)
    cum = _dot_exact_lhs(tri_b.astype(BF16), _log_sigmoid(g))
    comb_t = jnp.where(lane < ML_HEADS, g, cum).T

    H = range(ML_HEADS)
    sls = [slice(128 * h, 128 * h + 128) for h in H]
    qs = [q[:, sl] for sl in sls]
    ks = [k[:, sl] for sl in sls]
    vs = [v[:, sl] for sl in sls]
    qk = [_dot_nt(qs[h], ks[h]) for h in H]
    c_old = [c_ref[h] for h in H]
    n_old = [n_ref[h:h + 1, :] for h in H]
    m_prev = [m_ref[h:h + 1, 0:1] for h in H]
    qc = [_dot_nt(qs[h], c_old[h]) for h in H]
    b_col = [cum[:, ML_HEADS + h:ML_HEADS + h + 1] for h in H]
    ig_col = [g[:, h:h + 1] for h in H]
    dm = [jnp.where(tri_b, b_col[h] - comb_t[ML_HEADS + h:ML_HEADS + h + 1, :] + comb_t[h:h + 1, :], -jnp.inf)
          for h in H]
    inter = [b_col[h] + m_prev[h] for h in H]
    mj = [jnp.maximum(inter[h], jnp.max(dm[h], axis=-1, keepdims=True)) for h in H]
    s = [qk[h] * jnp.exp(dm[h] - mj[h]) for h in H]
    iw = [jnp.exp(inter[h] - mj[h]) for h in H]
    num = [_dot(s[h], vs[h]) + iw[h] * qc[h] for h in H]
    den = [jnp.sum(s[h], axis=-1, keepdims=True) + iw[h] * jnp.sum(qs[h] * n_old[h], axis=-1, keepdims=True)
           for h in H]
    hh = [num[h] / jnp.maximum(jnp.abs(den[h]), jnp.exp(-mj[h])) for h in H]

    b_last = [b_col[h][L - 1:L, :] for h in H]
    g_col = [b_last[h] - b_col[h] + ig_col[h] for h in H]
    m_new = [jnp.maximum(b_last[h] + m_prev[h], jnp.max(g_col[h], axis=0, keepdims=True)) for h in H]
    a_old = [jnp.exp(b_last[h] + m_prev[h] - m_new[h]) for h in H]
    wg = [jnp.exp(g_col[h] - m_new[h]) for h in H]
    kv = [_dot_tn(vs[h] * wg[h], ks[h]) for h in H]
    outs = []
    for h in H:
        c_ref[h] = a_old[h] * c_old[h] + kv[h]
        n_ref[h:h + 1, :] = a_old[h] * n_old[h] + jnp.sum(ks[h] * wg[h], axis=0, keepdims=True)
        m_ref[h:h + 1, :] = jnp.broadcast_to(m_new[h], (1, 128))
        hn = hh[h] * lax.rsqrt(jnp.mean(hh[h] * hh[h], axis=-1, keepdims=True) + NORM_EPS) * vec[2:3, sls[h]]
        outs.append(og[:, sls[h]] * hn)
    y_ref[...] = jnp.concatenate(outs, axis=1)


def _ml(p_all, cwq, cwk, vec, gb, batch, seq):
    n = batch * seq
    nc = seq // ML_CHUNK

    def rows(c):
        return lambda b, i: (b * nc + i, c)

    const = lambda b, i: (0, 0)
    base = COL_ML // 512
    return pl.pallas_call(
        _ml_kernel,
        grid=(batch, nc),
        in_specs=[pl.BlockSpec((ML_CHUNK, 512), rows(base)),
                  pl.BlockSpec((ML_CHUNK, 512), rows(base + 1)),
                  pl.BlockSpec((ML_CHUNK, 512), rows(base + 2)),
                  pl.BlockSpec((ML_CHUNK, 512), rows(base + 3)),
                  pl.BlockSpec((ML_CHUNK, 128), rows(COL_ML_GATE // 128)),
                  pl.BlockSpec((8, 512), const), pl.BlockSpec((8, 512), const),
                  pl.BlockSpec((8, 512), const), pl.BlockSpec((8, 128), const)],
        out_specs=pl.BlockSpec((ML_CHUNK, 512), rows(0)),
        out_shape=jax.ShapeDtypeStruct((n, 512), F32),
        scratch_shapes=[pltpu.VMEM((8, 512), F32), pltpu.VMEM((8, 512), F32),
                        pltpu.VMEM((ML_HEADS, 128, 128), F32), pltpu.VMEM((8, 128), F32),
                        pltpu.VMEM((8, 128), F32)],
        compiler_params=_params("arbitrary", "arbitrary"),
        name="mlstm",
    )(p_all, p_all, p_all, p_all, p_all, cwq, cwk, vec, gb)


def _rope128(x, cos_t, sin_t, half):
    lane = lax.broadcasted_iota(jnp.int32, x.shape, 1)
    seg = 128 if half == MLA_ROPE // 2 else DA_QK
    swapped = jnp.where((lane & (seg - 1)) < half, pltpu.roll(x, 128 - half, 1), pltpu.roll(x, half, 1))
    return x * cos_t + swapped * sin_t


def _mla_prep_kernel(p_ref, cos_ref, sin_ref, vec_ref, wq_ref, wkv_ref, q_ref, k_ref, vt_ref):
    p = p_ref[...]
    vec = vec_ref[...]
    cos_t = cos_ref[...]
    sin_t = sin_ref[...]
    q_lat = p[:, 0:MLA_Q_LORA]
    kv_lat = p[:, MLA_Q_LORA:MLA_Q_LORA + MLA_KV_LORA]
    k_pe = p[:, 640:768]
    qn = q_lat * lax.rsqrt(jnp.mean(q_lat * q_lat, axis=-1, keepdims=True) + NORM_EPS) * vec[0:1, 0:MLA_Q_LORA]
    kvn = kv_lat * lax.rsqrt(jnp.mean(kv_lat * kv_lat, axis=-1, keepdims=True) + NORM_EPS) * vec[1:2, 0:MLA_KV_LORA]
    q4 = _dot(qn, wq_ref[...])
    kv4 = _dot(kvn, wkv_ref[...])
    pe_ss = jnp.sum(k_pe * k_pe, axis=-1, keepdims=True)
    qs, ks = [], []
    for h in range(MLA_HEADS):
        qh = q4[:, MLA_PAD * h:MLA_PAD * (h + 1)]
        inv = lax.rsqrt(jnp.sum(qh * qh, axis=-1, keepdims=True) * (1.0 / MLA_QK) + NORM_EPS)
        qh = qh * inv * vec[2:3, MLA_PAD * h:MLA_PAD * (h + 1)]
        qs += [qh[:, 0:128], _rope128(qh[:, 128:256], cos_t, sin_t, MLA_ROPE // 2)]
        kn = kv4[:, 128 * h:128 * (h + 1)]
        inv = lax.rsqrt((jnp.sum(kn * kn, axis=-1, keepdims=True) + pe_ss) * (1.0 / MLA_QK) + NORM_EPS)
        ks += [kn * inv * vec[3:4, 0:128], _rope128(k_pe * inv * vec[4:5, 0:128], cos_t, sin_t, MLA_ROPE // 2)]
    q_ref[...] = jnp.concatenate(qs, axis=1).astype(BF16)
    k_ref[...] = jnp.concatenate(ks, axis=1).astype(BF16)
    _store_vt(vt_ref, kv4[:, 512:1024])


def _store_vt(vt_ref, v):
    heads, _, rows = vt_ref.shape
    vt = v.T.reshape(heads, 128, rows)
    vt_ref[...] = jnp.concatenate([vt, jnp.ones((heads, DV_AUG - 128, rows), F32)], axis=1).astype(BF16)


def _vt_spec(heads, nt):
    return pl.BlockSpec((None, heads, None, DV_AUG, ATT_BLOCK), lambda i: (i // nt, 0, i % nt, 0, 0))


def _mla_prep(p_all, cos_t, sin_t, vec, wq, wkv, batch, seq):
    n = batch * seq
    tr = ATT_BLOCK
    nt = seq // tr
    const = lambda i: (0, 0)
    return pl.pallas_call(
        _mla_prep_kernel,
        grid=(n // tr,),
        in_specs=[pl.BlockSpec((tr, 768), lambda i: (i, COL_MLA // 768)),
                  pl.BlockSpec((tr, 128), lambda i: (i % nt, 0)),
                  pl.BlockSpec((tr, 128), lambda i: (i % nt, 0)),
                  pl.BlockSpec((8, 1024), const),
                  pl.BlockSpec((MLA_Q_LORA, 1024), const),
                  pl.BlockSpec((MLA_KV_LORA, 1024), const)],
        out_specs=[pl.BlockSpec((tr, 1024), lambda i: (i, 0)),
                   pl.BlockSpec((tr, 1024), lambda i: (i, 0)),
                   _vt_spec(MLA_HEADS, nt)],
        out_shape=[jax.ShapeDtypeStruct((n, 1024), BF16), jax.ShapeDtypeStruct((n, 1024), BF16),
                   jax.ShapeDtypeStruct((batch, MLA_HEADS, nt, DV_AUG, tr), BF16)],
        compiler_params=_params("arbitrary"),
        name="mla_prep",
    )(p_all, cos_t, sin_t, vec, wq, wkv)


def _attn_block_t(kbs, vts, units, masked):
    sub = ATT_SUB
    s, cols = [], []
    for q, _, _, c0, h in units:
        keys = c0 + sub if masked else kbs[h].shape[0]
        st = _dot_nt(kbs[h][0:keys], q)
        if masked:
            keep = (lax.broadcasted_iota(jnp.int32, st.shape, 0)
                    <= lax.broadcasted_iota(jnp.int32, st.shape, 1) + c0)
            st = jnp.where(keep, st, -jnp.inf)
        s.append(st)
        cols.append(slice(c0, c0 + sub))
    idx = range(len(units))
    m_old = [units[i][1][:, cols[i]] for i in idx]
    m_new = [jnp.maximum(m_old[i], jnp.max(s[i], axis=0, keepdims=True)) for i in idx]
    p = [jnp.exp2(s[i] - m_new[i]).astype(BF16) for i in idx]
    alpha = [jnp.exp2(m_old[i] - m_new[i]) for i in idx]
    for i in idx:
        _, m_ref, acc_ref, _, h = units[i]
        pv = jnp.dot(vts[h][:, 0:s[i].shape[0]], p[i], preferred_element_type=F32)
        acc_ref[:, cols[i]] = alpha[i] * acc_ref[:, cols[i]] + pv
        m_ref[:, cols[i]] = m_new[i]


def _init_stats(m_ref, acc_ref):
    m_ref[...] = jnp.full_like(m_ref, -jnp.inf)
    acc_ref[...] = jnp.zeros_like(acc_ref)


def _attn_sweep(i, k_ref, vt_ref, dk, units):
    tb = ATT_BLOCK
    heads = vt_ref.shape[0]

    def block(j, masked):
        start = pl.multiple_of(j * tb, tb)
        kbs = [k_ref[pl.ds(start, tb), dk * h:dk * (h + 1)] for h in range(heads)]
        _attn_block_t(kbs, [vt_ref[h, j] for h in range(heads)], units, masked)

    def body(j, carry):
        block(j, False)
        return carry

    lax.fori_loop(0, i, body, 0)
    block(i, True)


def _attn_result(acc_ref):
    acc = acc_ref[...]
    return acc[0:128] / acc[128:129]


def _attn_specs(dk, nq, seq, heads):
    tb = ATT_BLOCK
    return [pl.BlockSpec((tb, dk * heads), lambda b, h, i: (b * nq + i, h)),
            pl.BlockSpec((seq, dk * heads), lambda b, h, i: (b, h)),
            pl.BlockSpec((None, heads, nq, DV_AUG, tb), lambda b, h, i: (b, h, 0, 0, 0))]


def _mla_attn_kernel(q_ref, k_ref, vt_ref, o_ref, *stats):
    heads = vt_ref.shape[0]
    units = []
    for h in range(heads):
        m_ref, acc_ref = stats[2 * h:2 * h + 2]
        _init_stats(m_ref, acc_ref)
        units += [(q_ref[c0:c0 + ATT_SUB, MLA_PAD * h:MLA_PAD * (h + 1)], m_ref, acc_ref, c0, h)
                  for c0 in range(0, ATT_BLOCK, ATT_SUB)]
    _attn_sweep(pl.program_id(2), k_ref, vt_ref, MLA_PAD, units)
    o_ref[...] = jnp.concatenate([_attn_result(stats[2 * h + 1]).T for h in range(heads)], axis=1)


def _mla_attn(q, k, vt, batch, seq):
    n = batch * seq
    tb = ATT_BLOCK
    nq = seq // tb
    return pl.pallas_call(
        _mla_attn_kernel,
        grid=(batch, MLA_HEADS // MLA_ATT_HEADS, nq),
        in_specs=_attn_specs(MLA_PAD, nq, seq, MLA_ATT_HEADS),
        out_specs=pl.BlockSpec((tb, 128 * MLA_ATT_HEADS), lambda b, h, i: (b * nq + i, h)),
        out_shape=jax.ShapeDtypeStruct((n, 512), F32),
        scratch_shapes=[pltpu.VMEM((1, tb), F32), pltpu.VMEM((DV_AUG, tb), F32)] * MLA_ATT_HEADS,
        compiler_params=_params("arbitrary", "arbitrary", "arbitrary"),
        name="mla_attn",
    )(q, k, vt)


def _da_prep_kernel(q_ref, k_ref, v_ref, cos_ref, sin_ref, vec_ref, ones_ref, qo_ref, ko_ref, vo_ref):
    vec = vec_ref[...]
    ones = ones_ref[...]
    cos_t = cos_ref[...]
    sin_t = sin_ref[...]

    def norm_rope(x, gain):
        ss = _seg_sum(x * x, ones) * (1.0 / DA_QK)
        x = x * lax.rsqrt(ss + NORM_EPS) * gain
        parts = [_rope128(x[:, 128 * c:128 * (c + 1)], cos_t, sin_t, DA_ROT // 2) for c in range(4)]
        return jnp.concatenate(parts, axis=1)

    qo_ref[...] = norm_rope(q_ref[...], vec[0:1]).astype(BF16)
    ko_ref[...] = norm_rope(k_ref[...], vec[1:2]).astype(BF16)
    _store_vt(vo_ref, v_ref[...])


def _da_prep(p_all, cos_t, sin_t, vec, ones64, batch, seq):
    n = batch * seq
    tr = ATT_BLOCK
    nt = seq // tr
    const = lambda i: (0, 0)
    base = COL_DA // 512
    out = jax.ShapeDtypeStruct((n, 512), BF16)
    return pl.pallas_call(
        _da_prep_kernel,
        grid=(n // tr,),
        in_specs=[pl.BlockSpec((tr, 512), lambda i: (i, base)),
                  pl.BlockSpec((tr, 512), lambda i: (i, base + 1)),
                  pl.BlockSpec((tr, 512), lambda i: (i, base + 2)),
                  pl.BlockSpec((tr, 128), lambda i: (i % nt, 0)),
                  pl.BlockSpec((tr, 128), lambda i: (i % nt, 0)),
                  pl.BlockSpec((8, 512), const),
                  pl.BlockSpec((512, 512), const)],
        out_specs=[pl.BlockSpec((tr, 512), lambda i: (i, 0))] * 2 + [_vt_spec(DA_HEADS, nt)],
        out_shape=[out, out, jax.ShapeDtypeStruct((batch, DA_HEADS, nt, DV_AUG, tr), BF16)],
        compiler_params=_params("arbitrary"),
        name="da_prep",
    )(p_all, p_all, p_all, cos_t, sin_t, vec, ones64)


def _da_attn_kernel(lam_init, q_ref, k_ref, vt_ref, lam_ref, g_ref, o_ref, *stats):
    heads = vt_ref.shape[0]
    lane = lax.broadcasted_iota(jnp.int32, (ATT_BLOCK, 128), 1)
    units = []
    for h in range(heads):
        q = q_ref[:, 128 * h:128 * (h + 1)]
        for mp in range(2):
            qm = jnp.where((lane < DA_QK) == (mp == 0), q, jnp.zeros_like(q))
            m_ref, acc_ref = stats[4 * h + 2 * mp:4 * h + 2 * mp + 2]
            _init_stats(m_ref, acc_ref)
            units += [(qm[c0:c0 + ATT_SUB], m_ref, acc_ref, c0, h) for c0 in range(0, ATT_BLOCK, ATT_SUB)]
    _attn_sweep(pl.program_id(2), k_ref, vt_ref, 128, units)

    lv = lam_ref[...]
    lam = (jnp.exp(jnp.sum(lv[0:1] * lv[1:2], axis=-1, keepdims=True))
           - jnp.exp(jnp.sum(lv[2:3] * lv[3:4], axis=-1, keepdims=True)) + lam_init)
    outs = []
    for h in range(heads):
        o = (_attn_result(stats[4 * h + 1]) - lam * _attn_result(stats[4 * h + 3])).T
        o = o * lax.rsqrt(jnp.mean(o * o, axis=-1, keepdims=True) + NORM_EPS) * g_ref[0:1]
        outs.append(o * (1.0 - lam_init))
    o_ref[...] = jnp.concatenate(outs, axis=1)


def _da_attn(q, k, vt, lam_vec, sub_g, lam_init, batch, seq):
    n = batch * seq
    tb = ATT_BLOCK
    nq = seq // tb
    const = lambda b, h, i: (0, 0)
    row = pltpu.VMEM((1, tb), F32)
    acc = pltpu.VMEM((DV_AUG, tb), F32)
    return pl.pallas_call(
        functools.partial(_da_attn_kernel, lam_init),
        grid=(batch, DA_HEADS // DA_ATT_HEADS, nq),
        in_specs=_attn_specs(128, nq, seq, DA_ATT_HEADS)
        + [pl.BlockSpec((8, 128), const), pl.BlockSpec((8, 128), const)],
        out_specs=pl.BlockSpec((tb, 128 * DA_ATT_HEADS), lambda b, h, i: (b * nq + i, h)),
        out_shape=jax.ShapeDtypeStruct((n, 512), F32),
        scratch_shapes=[row, acc] * (2 * DA_ATT_HEADS),
        compiler_params=_params("arbitrary", "arbitrary", "arbitrary"),
        name="da_attn",
    )(q, k, vt, lam_vec, sub_g)


def _outproj_kernel(ya_ref, yb_ref, yc_ref, yd_ref, z_ref, beta_ref, x_ref, w_ref, o_ref):
    y_refs = (ya_ref, yb_ref, yc_ref, yd_ref)
    acc = x_ref[...]
    for c in range(4):
        sl = slice(GROUP_W * c, GROUP_W * (c + 1))
        z = z_ref[:, sl]
        gated = (y_refs[c][...] * beta_ref[0:1, sl] * (z * _sigmoid(z))).astype(BF16)
        acc = acc + jnp.dot(gated, w_ref[sl, :], preferred_element_type=F32)
    o_ref[...] = acc


def _outproj(ys, p_all, beta_row, x2, w, tm=512):
    n = x2.shape[0]
    yspec = pl.BlockSpec((tm, 512), lambda i: (i, 0))
    resident = dict(pipeline_mode=pl.Buffered(1))
    return pl.pallas_call(
        _outproj_kernel,
        grid=(n // tm,),
        in_specs=[yspec] * 4 + [pl.BlockSpec((tm, D_MODEL), lambda i: (i, COL_Z // D_MODEL))]
        + [pl.BlockSpec((8, D_MODEL), lambda i: (0, 0), **resident),
           pl.BlockSpec((tm, D_MODEL), lambda i: (i, 0)),
           pl.BlockSpec((D_MODEL, D_MODEL), lambda i: (0, 0), **resident)],
        out_specs=pl.BlockSpec((tm, D_MODEL), lambda i: (i, 0)),
        out_shape=jax.ShapeDtypeStruct((n, D_MODEL), F32),
        compiler_params=_params("arbitrary"),
        name="outproj",
    )(*ys, p_all, beta_row, x2, w)


def _rows(rows, width, n_rows=8):
    out = jnp.zeros((n_rows, width), F32)
    for i, r in enumerate(rows):
        out = out.at[i, :r.shape[0]].set(r.astype(F32))
    return out


def _pad_to(a, shape):
    return jnp.zeros(shape, a.dtype).at[tuple(slice(0, s) for s in a.shape)].set(a)


def _arrange_w_in(w):
    w = w.astype(BF16)
    zeros = lambda c: jnp.zeros((D_MODEL, c), BF16)
    return jnp.concatenate([
        w[:, 5704:6408], zeros(64),
        w[:, 3584:3648], zeros(64),
        w[:, 5696:5704], zeros(120),
        w[:, 2048:3584],
        w[:, 3648:5696],
        w[:, 6408:7944],
        w[:, 0:2048],
    ], axis=1).reshape(D_MODEL, N_PROJ // INPROJ_TN, INPROJ_TN).transpose(1, 0, 2)


def _rope_tables(seq, rot, seg, fill):
    half = rot // 2
    inv = ROPE_THETA ** (-jnp.arange(0, rot, 2, dtype=F32) / rot)
    ang = jnp.arange(seq, dtype=F32)[:, None] * inv[None, :]
    cos, sin = jnp.cos(ang), jnp.sin(ang)
    rest = seg - rot
    cos_s = jnp.concatenate([cos, cos, jnp.full((seq, rest), fill, F32)], axis=1)
    sin_s = jnp.concatenate([-sin, sin, jnp.zeros((seq, rest), F32)], axis=1)
    reps = 128 // seg
    return jnp.tile(cos_s, (1, reps)), jnp.tile(sin_s, (1, reps))


def kernel(x, norm_g, w_in, w_out, branch_beta, rw_mu, rw_w0, rw_w_up, rw_a0, rw_a_up, rw_v0, rw_v_dn, rw_v_up, rw_k_k, rw_k_a, rw_r_k, rw_ln_w, rw_ln_b, ml_conv_w, ml_conv_b, ml_i_b, ml_f_b, ml_norm_g, mla_q_norm_g, mla_kv_norm_g, mla_w_q_b, mla_w_kv_b, mla_q_g, mla_k_g, da_q_g, da_k_g, da_lq1, da_lk1, da_lq2, da_lk2, da_sub_g):
    batch, seq, _ = x.shape
    depth = w_in.shape[0]
    n = batch * seq
    x2 = x.reshape(n, D_MODEL)

    seg_id = jnp.arange(GROUP_W) // RW_HEAD
    ones64 = (seg_id[:, None] == seg_id[None, :]).astype(BF16)
    mla_cos, mla_sin = _rope_tables(seq, MLA_ROPE, 128, 0.0)
    da_cos, da_sin = _rope_tables(seq, DA_ROT, DA_QK, 1.0)

    v_first = None
    for l in range(depth):
        p_all = _inproj(x2, norm_g[l][None, :], _arrange_w_in(w_in[l]))

        mu = rw_mu[l]
        vec = _rows([mu[0:512], mu[512:1024], mu[1024:1536], rw_w0[l], rw_a0[l], rw_k_k[l], rw_k_a[l]]
                    + ([rw_v0[l - 1]] if l > 0 else []), 512)
        mulo = _rows([mu[1536:1600]], 128)
        wup = _pad_to(rw_w_up[l], (128, 512)).astype(BF16)
        aup = jnp.zeros((128, 512), F32).at[32:64].set(rw_a_up[l]).astype(BF16)
        if l > 0:
            vdn = _pad_to(rw_v_dn[l - 1], (512, 128)).astype(BF16)
            vup = _pad_to(rw_v_up[l - 1], (128, 512)).astype(BF16)
        else:
            vdn = vup = None
        r, k, v, lw, kk, a = _rw_prep(p_all, v_first, vec, mulo, wup, aup, vdn, vup, ones64, batch, seq)
        if l == 0:
            v_first = v
        par = _rows([rw_r_k[l].reshape(-1), rw_ln_w[l], rw_ln_b[l]], 512)
        y_rw = _rw_scan(r, k, v, lw, kk, a, par, ones64, batch, seq)

        cw = ml_conv_w[l]
        cb = ml_conv_b[l]
        ml_vec = _rows([cb[0:512], cb[512:1024], ml_norm_g[l]], 512)
        gb = _rows([jnp.concatenate([ml_i_b[l], ml_f_b[l]])], 128)
        y_ml = _ml(p_all, _pad_to(cw[:, 0:512], (8, 512)), _pad_to(cw[:, 512:1024], (8, 512)),
                   ml_vec, gb, batch, seq)

        scale = MLA_QK ** -0.5 * LOG2E
        qg = jnp.tile(jnp.concatenate([mla_q_g[l] * scale, jnp.zeros((64,), F32)]), MLA_HEADS)
        mla_vec = _rows([mla_q_norm_g[l], mla_kv_norm_g[l], qg, mla_k_g[l][0:128], mla_k_g[l][128:192]], 1024)
        wq = _pad_to(mla_w_q_b[l].reshape(MLA_Q_LORA, MLA_HEADS, MLA_QK),
                     (MLA_Q_LORA, MLA_HEADS, MLA_PAD)).reshape(MLA_Q_LORA, 1024).astype(BF16)
        wkv = mla_w_kv_b[l].reshape(MLA_KV_LORA, MLA_HEADS, 2, 128).transpose(0, 2, 1, 3)
        wkv = wkv.reshape(MLA_KV_LORA, 1024).astype(BF16)
        mq, mk, mv = _mla_prep(p_all, mla_cos, mla_sin, mla_vec, wq, wkv, batch, seq)
        y_mla = _mla_attn(mq, mk, mv, batch, seq)

        da_vec = _rows([jnp.tile(da_q_g[l], 8) * (DA_QK ** -0.5 * LOG2E), jnp.tile(da_k_g[l], 8)], 512)
        dq, dk, dv = _da_prep(p_all, da_cos, da_sin, da_vec, ones64, batch, seq)
        lam_vec = _rows([da_lq1[l], da_lk1[l], da_lq2[l], da_lk2[l]], 128)
        lam_init = 0.8 - 0.6 * math.exp(-0.3 * l)
        y_da = _da_attn(dq, dk, dv, lam_vec, _rows([da_sub_g[l]], 128), lam_init, batch, seq)

        beta_row = _rows([jnp.repeat(branch_beta[l], GROUP_W)], D_MODEL)
        x2 = _outproj((y_rw, y_ml, y_mla, y_da), p_all, beta_row, x2, w_out[l].astype(BF16))
    return x2.reshape(batch, seq, D_MODEL)
```

```python
import functools
import math

import jax
import jax.numpy as jnp
from jax import lax
from jax.experimental import pallas as pl
from jax.experimental.pallas import tpu as pltpu

F32 = jnp.float32
BF16 = jnp.bfloat16

D_MODEL = 2048
GROUP_W = 512
NORM_EPS = 1e-6
ROPE_THETA = 500000.0

RW_HEAD = 64
RW_GN_EPS = 64e-5
RW_CHUNK = 64
ML_HEADS = 4
ML_HEAD = 128
ML_CHUNK = 256
MLA_HEADS = 4
MLA_NOPE = 128
MLA_ROPE = 64
MLA_QK = 192
MLA_Q_LORA = 384
MLA_KV_LORA = 256
MLA_PAD = 256
DA_HEADS = 4
DA_QK = 64
DA_ROT = 16
ATT_BLOCK = 512
ATT_SUB = 256
MLA_ATT_HEADS = 4
DA_ATT_HEADS = 2
DV_AUG = 144
LOG2E = math.log2(math.e)

N_PROJ = 8192
COL_MLA = 0
COL_RW_LORA = 768
COL_ML_GATE = 896
COL_RW = 1024
COL_ML = 2560
COL_DA = 4608
COL_Z = 6144

INPROJ_TN = 1024

VMEM_LIMIT = 48 * 1024 * 1024


def _dot(a, b):
    return jnp.dot(a.astype(BF16), b.astype(BF16), preferred_element_type=F32)


def _dot_nt(a, b):
    return lax.dot_general(a.astype(BF16), b.astype(BF16), (((1,), (1,)), ((), ())),
                           preferred_element_type=F32)


def _dot_tn(a, b):
    return lax.dot_general(a.astype(BF16), b.astype(BF16), (((0,), (0,)), ((), ())),
                           preferred_element_type=F32)


def _split3(x):
    h1 = x.astype(BF16)
    r1 = x - h1.astype(F32)
    h2 = r1.astype(BF16)
    h3 = (r1 - h2.astype(F32)).astype(BF16)
    return h1, h2, h3


def _dot_exact_lhs(a_bf16, x):
    h1, h2, h3 = _split3(x)
    f = functools.partial(jnp.dot, preferred_element_type=F32)
    return f(a_bf16, h1) + f(a_bf16, h2) + f(a_bf16, h3)


def _seg_sum(x, ones_bf16):
    hi = x.astype(BF16)
    lo = (x - hi.astype(F32)).astype(BF16)
    f = functools.partial(jnp.dot, preferred_element_type=F32)
    return f(hi, ones_bf16) + f(lo, ones_bf16)


def _sigmoid(x):
    return 1.0 / (1.0 + jnp.exp(-x))


def _log_sigmoid(x):
    return jnp.minimum(x, 0.0) - jnp.log(1.0 + jnp.exp(-jnp.abs(x)))


def _params(*sem):
    return pltpu.CompilerParams(dimension_semantics=sem, vmem_limit_bytes=VMEM_LIMIT)


def _inproj_kernel(x_ref, g_ref, w_ref, o_ref, h_ref):
    @pl.when(pl.program_id(1) == 0)
    def _():
        x = x_ref[...]
        ms = jnp.mean(x * x, axis=-1, keepdims=True)
        h_ref[...] = (x * lax.rsqrt(ms + NORM_EPS) * g_ref[...]).astype(BF16)

    o_ref[...] = jnp.dot(h_ref[...], w_ref[...], preferred_element_type=F32)


def _inproj(x2, g, w, tm=1024):
    n = x2.shape[0]
    tn = w.shape[2]
    return pl.pallas_call(
        _inproj_kernel,
        grid=(n // tm, N_PROJ // tn),
        in_specs=[pl.BlockSpec((tm, D_MODEL), lambda i, j: (i, 0)),
                  pl.BlockSpec((1, D_MODEL), lambda i, j: (0, 0)),
                  pl.BlockSpec((None, D_MODEL, tn), lambda i, j: (j, 0, 0))],
        out_specs=pl.BlockSpec((tm, tn), lambda i, j: (i, j)),
        out_shape=jax.ShapeDtypeStruct((n, N_PROJ), F32),
        scratch_shapes=[pltpu.VMEM((tm, D_MODEL), BF16)],
        compiler_params=_params("arbitrary", "arbitrary"),
        name="inproj",
    )(x2, g, w)


def _rw_kernel(has_vres, *refs):
    if has_vres:
        (rr_ref, kr_ref, vr_ref, lo_ref, vf_ref, vec_ref, mulo_ref, wup_ref, aup_ref, vdn_ref, vup_ref,
         par_ref, ones_ref, y_ref, s_ref, cr_ref, ck_ref, cv_ref, cl_ref) = refs
    else:
        (rr_ref, kr_ref, vr_ref, lo_ref, vec_ref, mulo_ref, wup_ref, aup_ref,
         par_ref, ones_ref, y_ref, vo_ref, s_ref, cr_ref, ck_ref, cv_ref, cl_ref) = refs
    L = RW_CHUNK
    batch = rr_ref.shape[0]

    @pl.when(pl.program_id(0) == 0)
    def _():
        for ref in (s_ref, cr_ref, ck_ref, cv_ref, cl_ref):
            ref[...] = jnp.zeros_like(ref)

    def shift(x_ref, carry_ref, mu):
        outs = []
        for bi in range(batch):
            x = x_ref[bi]
            row = lax.broadcasted_iota(jnp.int32, x.shape, 0)
            prev = jnp.where(row == 0, carry_ref[bi, 0:1, :], pltpu.roll(x, 1, 0))
            carry_ref[bi, 0:1, :] = x[L - 1:L, :]
            outs.append(x + (prev - x) * mu)
        return jnp.concatenate(outs, axis=0)

    vec = vec_ref[...]
    ones = ones_ref[...]
    r_all = shift(rr_ref, cr_ref, vec[0:1])
    k_all = shift(kr_ref, ck_ref, vec[1:2])
    v_all = shift(vr_ref, cv_ref, vec[2:3])
    lo = shift(lo_ref, cl_ref, mulo_ref[0:1])

    lw_all = -math.exp(-0.5) * _sigmoid(vec[3:4] + _dot(jnp.tanh(lo), wup_ref[...]))
    a_all = _sigmoid(vec[4:5] + _dot(lo, aup_ref[...]))
    if has_vres:
        gate = _sigmoid(vec[7:8] + _dot(_dot(v_all, vdn_ref[...]), vup_ref[...]))
        vf = jnp.concatenate([vf_ref[bi] for bi in range(batch)], axis=0)
        v_all = v_all + (vf - v_all) * gate
    else:
        for bi in range(batch):
            vo_ref[bi] = v_all[L * bi:L * (bi + 1)]
    kk_all = k_all * vec[5:6]
    kk_all = kk_all / jnp.maximum(jnp.sqrt(_seg_sum(kk_all * kk_all, ones)), 1e-12)
    k_all = k_all * (1.0 + (a_all - 1.0) * vec[6:7])

    row = lax.broadcasted_iota(jnp.int32, (L, 128), 0)
    lane = lax.broadcasted_iota(jnp.int32, (L, 128), 1)
    col = lane & (RW_HEAD - 1)
    strict = row > col
    incl = row >= col
    eye = (row == col).astype(F32)
    first = lane < RW_HEAD
    r128 = lax.broadcasted_iota(jnp.int32, (128, 128), 0)
    c128 = lax.broadcasted_iota(jnp.int32, (128, 128), 1)
    same_head = (r128 < RW_HEAD) == (c128 < RW_HEAD)
    tri = (lax.broadcasted_iota(jnp.int32, (L, L), 0) >= lax.broadcasted_iota(jnp.int32, (L, L), 1)).astype(BF16)

    def bd(x):
        return jnp.concatenate([jnp.where(first, x, 0.0), jnp.where(first, 0.0, x)], axis=0)

    n_pairs = GROUP_W // 128
    units = [(bi, pr) for bi in range(batch) for pr in range(n_pairs)]
    idx = range(len(units))
    al, rt, bet, kt, beh, kh, p_chunk, vs = [], [], [], [], [], [], [], []
    for bi in range(batch):
        rows = slice(L * bi, L * (bi + 1))
        lw = lw_all[rows]
        r = r_all[rows]
        k = k_all[rows]
        kk = kk_all[rows]
        b = _dot_exact_lhs(tri, lw)
        b_last = b[L - 1:L, :]
        p_inv = jnp.exp(-b)
        p_end = jnp.exp(b_last - b)
        be = kk * a_all[rows]
        al.append(-(kk * jnp.exp(b - lw)))
        rt.append(r * jnp.exp(b))
        bet.append(be * p_inv)
        kt.append(k * p_inv)
        beh.append(be * p_end)
        kh.append(k * p_end)
        p_chunk.append(jnp.exp(b_last))
        vs.append(v_all[rows])

    def part(xs, i):
        bi, pr = units[i]
        return xs[bi][:, 128 * pr:128 * pr + 128]

    lhs = [jnp.concatenate([part(al, i), part(rt, i)], axis=0).astype(BF16) for i in idx]
    rhs = [jnp.concatenate([bd(part(bet, i)), bd(part(kt, i))], axis=0).astype(BF16) for i in idx]
    g = [_dot_nt(lhs[i], rhs[i]) for i in idx]
    a_ab = [jnp.where(strict, g[i][0:L, 0:128], 0.0) for i in idx]
    a_ak = [jnp.where(strict, g[i][0:L, 128:256], 0.0) for i in idx]
    b_rb = [jnp.where(incl, g[i][L:2 * L, 0:128], 0.0) for i in idx]
    b_rk = [jnp.where(incl, g[i][L:2 * L, 128:256], 0.0) for i in idx]
    s = [s_ref[i] for i in idx]
    sa = [_dot_nt(lhs[i], s[i]) for i in idx]
    vv = [_dot(jnp.concatenate([a_ak[i], b_rk[i]], axis=0), bd(part(vs, i))) for i in idx]

    t_inv = [eye + a_ab[i] for i in idx]
    x = [_dot(a_ab[i], bd(a_ab[i])) for i in idx]
    for _ in range(4):
        xt = [_dot(jnp.concatenate([x[i], t_inv[i]], axis=0), bd(x[i])) for i in idx]
        x = [xt[i][0:L] for i in idx]
        t_inv = [t_inv[i] + xt[i][L:2 * L] for i in idx]
    t_inv = [t_inv[i] + _dot(t_inv[i], bd(x[i])) for i in idx]

    u = [_dot(t_inv[i], bd(sa[i][0:L] + vv[i][0:L])) for i in idx]
    ys = [sa[i][L:2 * L] + vv[i][L:2 * L] + _dot(b_rb[i], bd(u[i])) for i in idx]
    for i in idx:
        upd = _dot_tn(jnp.concatenate([u[i], part(vs, i)], axis=0),
                      jnp.concatenate([part(beh, i), part(kh, i)], axis=0))
        s_ref[i] = s[i] * part(p_chunk, i) + jnp.where(same_head, upd, 0.0)

    par = par_ref[...]
    inv_n = 1.0 / RW_HEAD
    rows_all = batch * L
    y_all = jnp.concatenate([jnp.concatenate(ys[n_pairs * bi:n_pairs * (bi + 1)], axis=1)
                             for bi in range(batch)], axis=0)
    sums = _dot(jnp.concatenate([y_all, r_all * k_all * par[0:1]], axis=0), ones)
    yc = y_all - sums[0:rows_all] * inv_n
    var = _dot(yc * yc, ones) * inv_n
    out = yc * lax.rsqrt(var + RW_GN_EPS) * par[1:2] + par[2:3] + sums[rows_all:2 * rows_all] * v_all
    for bi in range(batch):
        y_ref[bi] = out[L * bi:L * (bi + 1)]


def _rw(p_all, v_first, vec, mulo, wup, aup, vdn, vup, par, ones64, batch, seq):
    nc = seq // RW_CHUNK
    has_vres = v_first is not None
    p3 = p_all.reshape(batch, seq, N_PROJ)

    def cols(width, c):
        return pl.BlockSpec((batch, RW_CHUNK, width), lambda i: (0, i, c))

    const = lambda i: (0, 0)
    blk = cols(512, 0)
    in_specs = [cols(512, COL_RW // 512), cols(512, COL_RW // 512 + 1), cols(512, COL_RW // 512 + 2),
                cols(128, COL_RW_LORA // 128)]
    args = [p3, p3, p3, p3]
    if has_vres:
        in_specs.append(blk)
        args.append(v_first.reshape(batch, seq, 512))
    in_specs += [pl.BlockSpec((8, 512), const), pl.BlockSpec((8, 128), const),
                 pl.BlockSpec((128, 512), const), pl.BlockSpec((128, 512), const)]
    args += [vec, mulo, wup, aup]
    if has_vres:
        in_specs += [pl.BlockSpec((512, 128), const), pl.BlockSpec((128, 512), const)]
        args += [vdn, vup]
    in_specs += [pl.BlockSpec((8, 512), const), pl.BlockSpec((512, 512), const)]
    args += [par, ones64]
    out = jax.ShapeDtypeStruct((batch, seq, 512), F32)
    n_out = 1 if has_vres else 2
    res = pl.pallas_call(
        functools.partial(_rw_kernel, has_vres),
        grid=(nc,),
        in_specs=in_specs,
        out_specs=[blk] * n_out,
        out_shape=[out] * n_out,
        scratch_shapes=[pltpu.VMEM((batch * (GROUP_W // 128), 128, 128), F32)]
        + [pltpu.VMEM((batch, 8, 512), F32)] * 3 + [pltpu.VMEM((batch, 8, 128), F32)],
        compiler_params=_params("arbitrary"),
        name="rwkv7",
    )(*args)
    y = res[0].reshape(batch * seq, 512)
    return y, (v_first if has_vres else res[1].reshape(batch * seq, 512))


def _ml_kernel(q_ref, k_ref, v_ref, o_ref, g_ref, cwq_ref, cwk_ref, vec_ref, gb_ref, y_ref,
               hq_ref, hk_ref, c_ref, n_ref, m_ref):
    L = ML_CHUNK

    @pl.when(pl.program_id(1) == 0)
    def _():
        hq_ref[...] = jnp.zeros_like(hq_ref)
        hk_ref[...] = jnp.zeros_like(hk_ref)
        c_ref[...] = jnp.zeros_like(c_ref)
        n_ref[...] = jnp.zeros_like(n_ref)
        m_ref[...] = jnp.zeros_like(m_ref)

    vec = vec_ref[...]
    row8 = lax.broadcasted_iota(jnp.int32, (8, GROUP_W), 0)

    def conv(x_ref, halo_ref, w_ref, bias):
        x = x_ref[...]
        w = w_ref[...]
        halo = halo_ref[...]
        acc = x * w[3:4] + bias
        for s in (1, 2, 3):
            xs = pltpu.roll(x, s, 0)
            top = jnp.where(row8 < s, pltpu.roll(halo, s, 0), xs[0:8])
            xs = jnp.concatenate([top, xs[8:]], axis=0)
            acc = acc + xs * w[3 - s:4 - s]
        halo_ref[...] = x[L - 8:L]
        return acc * _sigmoid(acc)

    q = conv(q_ref, hq_ref, cwq_ref, vec[0:1])
    k = conv(k_ref, hk_ref, cwk_ref, vec[1:2]) * (ML_HEAD ** -0.5)
    v = v_ref[...]
    og = _sigmoid(o_ref[...])

    g = g_ref[...] + gb_ref[0:1]
    lane = lax.broadcasted_iota(jnp.int32, (L, 128), 1)
    tri_b = lax.broadcasted_iota(jnp.int32, (L, L), 0) >= lax.broadcasted_iota(jnp.int32, (L, L), 1)
    cum = _dot_exact_lhs(tri_b.astype(BF16), _log_sigmoid(g))
    comb_t = jnp.where(lane < ML_HEADS, g, cum).T

    H = range(ML_HEADS)
    sls = [slice(128 * h, 128 * h + 128) for h in H]
    qs = [q[:, sl] for sl in sls]
    ks = [k[:, sl] for sl in sls]
    vs = [v[:, sl] for sl in sls]
    qk = [_dot_nt(qs[h], ks[h]) for h in H]
    c_old = [c_ref[h] for h in H]
    n_old = [n_ref[h:h + 1, :] for h in H]
    m_prev = [m_ref[h:h + 1, 0:1] for h in H]
    qc = [_dot_nt(qs[h], c_old[h]) for h in H]
    b_col = [cum[:, ML_HEADS + h:ML_HEADS + h + 1] for h in H]
    ig_col = [g[:, h:h + 1] for h in H]
    dm = [jnp.where(tri_b, b_col[h] - comb_t[ML_HEADS + h:ML_HEADS + h + 1, :] + comb_t[h:h + 1, :], -jnp.inf)
          for h in H]
    inter = [b_col[h] + m_prev[h] for h in H]
    mj = [jnp.maximum(inter[h], jnp.max(dm[h], axis=-1, keepdims=True)) for h in H]
    s = [qk[h] * jnp.exp(dm[h] - mj[h]) for h in H]
    iw = [jnp.exp(inter[h] - mj[h]) for h in H]
    num = [_dot(s[h], vs[h]) + iw[h] * qc[h] for h in H]
    den = [jnp.sum(s[h], axis=-1, keepdims=True) + iw[h] * jnp.sum(qs[h] * n_old[h], axis=-1, keepdims=True)
           for h in H]
    hh = [num[h] / jnp.maximum(jnp.abs(den[h]), jnp.exp(-mj[h])) for h in H]

    b_last = [b_col[h][L - 1:L, :] for h in H]
    g_col = [b_last[h] - b_col[h] + ig_col[h] for h in H]
    m_new = [jnp.maximum(b_last[h] + m_prev[h], jnp.max(g_col[h], axis=0, keepdims=True)) for h in H]
    a_old = [jnp.exp(b_last[h] + m_prev[h] - m_new[h]) for h in H]
    wg = [jnp.exp(g_col[h] - m_new[h]) for h in H]
    kv = [_dot_tn(vs[h] * wg[h], ks[h]) for h in H]
    outs = []
    for h in H:
        c_ref[h] = a_old[h] * c_old[h] + kv[h]
        n_ref[h:h + 1, :] = a_old[h] * n_old[h] + jnp.sum(ks[h] * wg[h], axis=0, keepdims=True)
        m_ref[h:h + 1, :] = jnp.broadcast_to(m_new[h], (1, 128))
        hn = hh[h] * lax.rsqrt(jnp.mean(hh[h] * hh[h], axis=-1, keepdims=True) + NORM_EPS) * vec[2:3, sls[h]]
        outs.append(og[:, sls[h]] * hn)
    y_ref[...] = jnp.concatenate(outs, axis=1)


def _ml(p_all, cwq, cwk, vec, gb, batch, seq):
    n = batch * seq
    nc = seq // ML_CHUNK

    def rows(c):
        return lambda b, i: (b * nc + i, c)

    const = lambda b, i: (0, 0)
    base = COL_ML // 512
    return pl.pallas_call(
        _ml_kernel,
        grid=(batch, nc),
        in_specs=[pl.BlockSpec((ML_CHUNK, 512), rows(base)),
                  pl.BlockSpec((ML_CHUNK, 512), rows(base + 1)),
                  pl.BlockSpec((ML_CHUNK, 512), rows(base + 2)),
                  pl.BlockSpec((ML_CHUNK, 512), rows(base + 3)),
                  pl.BlockSpec((ML_CHUNK, 128), rows(COL_ML_GATE // 128)),
                  pl.BlockSpec((8, 512), const), pl.BlockSpec((8, 512), const),
                  pl.BlockSpec((8, 512), const), pl.BlockSpec((8, 128), const)],
        out_specs=pl.BlockSpec((ML_CHUNK, 512), rows(0)),
        out_shape=jax.ShapeDtypeStruct((n, 512), F32),
        scratch_shapes=[pltpu.VMEM((8, 512), F32), pltpu.VMEM((8, 512), F32),
                        pltpu.VMEM((ML_HEADS, 128, 128), F32), pltpu.VMEM((8, 128), F32),
                        pltpu.VMEM((8, 128), F32)],
        compiler_params=_params("arbitrary", "arbitrary"),
        name="mlstm",
    )(p_all, p_all, p_all, p_all, p_all, cwq, cwk, vec, gb)


def _rope128(x, cos_t, sin_t, half):
    lane = lax.broadcasted_iota(jnp.int32, x.shape, 1)
    seg = 128 if half == MLA_ROPE // 2 else DA_QK
    swapped = jnp.where((lane & (seg - 1)) < half, pltpu.roll(x, 128 - half, 1), pltpu.roll(x, half, 1))
    return x * cos_t + swapped * sin_t


def _mla_prep_kernel(p_ref, cos_ref, sin_ref, vec_ref, wq_ref, wkv_ref, q_ref, k_ref, vt_ref):
    p = p_ref[...]
    vec = vec_ref[...]
    cos_t = cos_ref[...]
    sin_t = sin_ref[...]
    q_lat = p[:, 0:MLA_Q_LORA]
    kv_lat = p[:, MLA_Q_LORA:MLA_Q_LORA + MLA_KV_LORA]
    k_pe = p[:, 640:768]
    qn = q_lat * lax.rsqrt(jnp.mean(q_lat * q_lat, axis=-1, keepdims=True) + NORM_EPS) * vec[0:1, 0:MLA_Q_LORA]
    kvn = kv_lat * lax.rsqrt(jnp.mean(kv_lat * kv_lat, axis=-1, keepdims=True) + NORM_EPS) * vec[1:2, 0:MLA_KV_LORA]
    q4 = _dot(qn, wq_ref[...])
    kv4 = _dot(kvn, wkv_ref[...])
    pe_ss = jnp.sum(k_pe * k_pe, axis=-1, keepdims=True)
    qs, ks = [], []
    for h in range(MLA_HEADS):
        qh = q4[:, MLA_PAD * h:MLA_PAD * (h + 1)]
        inv = lax.rsqrt(jnp.sum(qh * qh, axis=-1, keepdims=True) * (1.0 / MLA_QK) + NORM_EPS)
        qh = qh * inv * vec[2:3, MLA_PAD * h:MLA_PAD * (h + 1)]
        qs += [qh[:, 0:128], _rope128(qh[:, 128:256], cos_t, sin_t, MLA_ROPE // 2)]
        kn = kv4[:, 128 * h:128 * (h + 1)]
        inv = lax.rsqrt((jnp.sum(kn * kn, axis=-1, keepdims=True) + pe_ss) * (1.0 / MLA_QK) + NORM_EPS)
        ks += [kn * inv * vec[3:4, 0:128], _rope128(k_pe * inv * vec[4:5, 0:128], cos_t, sin_t, MLA_ROPE // 2)]
    q_ref[...] = jnp.concatenate(qs, axis=1).astype(BF16)
    k_ref[...] = jnp.concatenate(ks, axis=1).astype(BF16)
    _store_vt(vt_ref, kv4[:, 512:1024])


def _store_vt(vt_ref, v):
    heads, _, rows = vt_ref.shape
    vt = v.T.reshape(heads, 128, rows)
    vt_ref[...] = jnp.concatenate([vt, jnp.ones((heads, DV_AUG - 128, rows), F32)], axis=1).astype(BF16)


def _vt_spec(heads, nt):
    return pl.BlockSpec((None, heads, None, DV_AUG, ATT_BLOCK), lambda i: (i // nt, 0, i % nt, 0, 0))


def _mla_prep(p_all, cos_t, sin_t, vec, wq, wkv, batch, seq):
    n = batch * seq
    tr = ATT_BLOCK
    nt = seq // tr
    const = lambda i: (0, 0)
    return pl.pallas_call(
        _mla_prep_kernel,
        grid=(n // tr,),
        in_specs=[pl.BlockSpec((tr, 768), lambda i: (i, COL_MLA // 768)),
                  pl.BlockSpec((tr, 128), lambda i: (i % nt, 0)),
                  pl.BlockSpec((tr, 128), lambda i: (i % nt, 0)),
                  pl.BlockSpec((8, 1024), const),
                  pl.BlockSpec((MLA_Q_LORA, 1024), const),
                  pl.BlockSpec((MLA_KV_LORA, 1024), const)],
        out_specs=[pl.BlockSpec((tr, 1024), lambda i: (i, 0)),
                   pl.BlockSpec((tr, 1024), lambda i: (i, 0)),
                   _vt_spec(MLA_HEADS, nt)],
        out_shape=[jax.ShapeDtypeStruct((n, 1024), BF16), jax.ShapeDtypeStruct((n, 1024), BF16),
                   jax.ShapeDtypeStruct((batch, MLA_HEADS, nt, DV_AUG, tr), BF16)],
        compiler_params=_params("arbitrary"),
        name="mla_prep",
    )(p_all, cos_t, sin_t, vec, wq, wkv)


def _attn_block_t(kbs, vts, units, masked):
    sub = ATT_SUB
    s, cols = [], []
    for q, _, _, c0, h in units:
        keys = c0 + sub if masked else kbs[h].shape[0]
        st = _dot_nt(kbs[h][0:keys], q)
        if masked:
            keep = (lax.broadcasted_iota(jnp.int32, st.shape, 0)
                    <= lax.broadcasted_iota(jnp.int32, st.shape, 1) + c0)
            st = jnp.where(keep, st, -jnp.inf)
        s.append(st)
        cols.append(slice(c0, c0 + sub))
    idx = range(len(units))
    m_old = [units[i][1][:, cols[i]] for i in idx]
    m_new = [jnp.maximum(m_old[i], jnp.max(s[i], axis=0, keepdims=True)) for i in idx]
    p = [jnp.exp2(s[i] - m_new[i]).astype(BF16) for i in idx]
    alpha = [jnp.exp2(m_old[i] - m_new[i]) for i in idx]
    for i in idx:
        _, m_ref, acc_ref, _, h = units[i]
        pv = jnp.dot(vts[h][:, 0:s[i].shape[0]], p[i], preferred_element_type=F32)
        acc_ref[:, cols[i]] = alpha[i] * acc_ref[:, cols[i]] + pv
        m_ref[:, cols[i]] = m_new[i]


def _init_stats(m_ref, acc_ref):
    m_ref[...] = jnp.full_like(m_ref, -jnp.inf)
    acc_ref[...] = jnp.zeros_like(acc_ref)


def _attn_sweep(i, k_ref, vt_ref, dk, units):
    tb = ATT_BLOCK
    heads = vt_ref.shape[0]

    def block(j, masked):
        start = pl.multiple_of(j * tb, tb)
        kbs = [k_ref[pl.ds(start, tb), dk * h:dk * (h + 1)] for h in range(heads)]
        _attn_block_t(kbs, [vt_ref[h, j] for h in range(heads)], units, masked)

    def body(j, carry):
        block(j, False)
        return carry

    lax.fori_loop(0, i, body, 0)
    block(i, True)


def _attn_result(acc_ref):
    acc = acc_ref[...]
    return acc[0:128] / acc[128:129]


def _attn_specs(dk, nq, seq, heads):
    tb = ATT_BLOCK
    return [pl.BlockSpec((tb, dk * heads), lambda b, h, i: (b * nq + i, h)),
            pl.BlockSpec((seq, dk * heads), lambda b, h, i: (b, h)),
            pl.BlockSpec((None, heads, nq, DV_AUG, tb), lambda b, h, i: (b, h, 0, 0, 0))]


def _mla_attn_kernel(q_ref, k_ref, vt_ref, o_ref, *stats):
    heads = vt_ref.shape[0]
    units = []
    for h in range(heads):
        m_ref, acc_ref = stats[2 * h:2 * h + 2]
        _init_stats(m_ref, acc_ref)
        units += [(q_ref[c0:c0 + ATT_SUB, MLA_PAD * h:MLA_PAD * (h + 1)], m_ref, acc_ref, c0, h)
                  for c0 in range(0, ATT_BLOCK, ATT_SUB)]
    _attn_sweep(pl.program_id(2), k_ref, vt_ref, MLA_PAD, units)
    o_ref[...] = jnp.concatenate([_attn_result(stats[2 * h + 1]).T for h in range(heads)], axis=1)


def _mla_attn(q, k, vt, batch, seq):
    n = batch * seq
    tb = ATT_BLOCK
    nq = seq // tb
    return pl.pallas_call(
        _mla_attn_kernel,
        grid=(batch, MLA_HEADS // MLA_ATT_HEADS, nq),
        in_specs=_attn_specs(MLA_PAD, nq, seq, MLA_ATT_HEADS),
        out_specs=pl.BlockSpec((tb, 128 * MLA_ATT_HEADS), lambda b, h, i: (b * nq + i, h)),
        out_shape=jax.ShapeDtypeStruct((n, 512), F32),
        scratch_shapes=[pltpu.VMEM((1, tb), F32), pltpu.VMEM((DV_AUG, tb), F32)] * MLA_ATT_HEADS,
        compiler_params=_params("arbitrary", "arbitrary", "arbitrary"),
        name="mla_attn",
    )(q, k, vt)


def _da_prep_kernel(q_ref, k_ref, v_ref, cos_ref, sin_ref, vec_ref, ones_ref, qo_ref, ko_ref, vo_ref):
    vec = vec_ref[...]
    ones = ones_ref[...]
    cos_t = cos_ref[...]
    sin_t = sin_ref[...]

    def norm_rope(x, gain):
        ss = _seg_sum(x * x, ones) * (1.0 / DA_QK)
        x = x * lax.rsqrt(ss + NORM_EPS) * gain
        parts = [_rope128(x[:, 128 * c:128 * (c + 1)], cos_t, sin_t, DA_ROT // 2) for c in range(4)]
        return jnp.concatenate(parts, axis=1)

    qo_ref[...] = norm_rope(q_ref[...], vec[0:1]).astype(BF16)
    ko_ref[...] = norm_rope(k_ref[...], vec[1:2]).astype(BF16)
    _store_vt(vo_ref, v_ref[...])


def _da_prep(p_all, cos_t, sin_t, vec, ones64, batch, seq):
    n = batch * seq
    tr = ATT_BLOCK
    nt = seq // tr
    const = lambda i: (0, 0)
    base = COL_DA // 512
    out = jax.ShapeDtypeStruct((n, 512), BF16)
    return pl.pallas_call(
        _da_prep_kernel,
        grid=(n // tr,),
        in_specs=[pl.BlockSpec((tr, 512), lambda i: (i, base)),
                  pl.BlockSpec((tr, 512), lambda i: (i, base + 1)),
                  pl.BlockSpec((tr, 512), lambda i: (i, base + 2)),
                  pl.BlockSpec((tr, 128), lambda i: (i % nt, 0)),
                  pl.BlockSpec((tr, 128), lambda i: (i % nt, 0)),
                  pl.BlockSpec((8, 512), const),
                  pl.BlockSpec((512, 512), const)],
        out_specs=[pl.BlockSpec((tr, 512), lambda i: (i, 0))] * 2 + [_vt_spec(DA_HEADS, nt)],
        out_shape=[out, out, jax.ShapeDtypeStruct((batch, DA_HEADS, nt, DV_AUG, tr), BF16)],
        compiler_params=_params("arbitrary"),
        name="da_prep",
    )(p_all, p_all, p_all, cos_t, sin_t, vec, ones64)


def _da_attn_kernel(lam_init, q_ref, k_ref, vt_ref, lam_ref, g_ref, o_ref, *stats):
    heads = vt_ref.shape[0]
    lane = lax.broadcasted_iota(jnp.int32, (ATT_BLOCK, 128), 1)
    units = []
    for h in range(heads):
        q = q_ref[:, 128 * h:128 * (h + 1)]
        for mp in range(2):
            qm = jnp.where((lane < DA_QK) == (mp == 0), q, jnp.zeros_like(q))
            m_ref, acc_ref = stats[4 * h + 2 * mp:4 * h + 2 * mp + 2]
            _init_stats(m_ref, acc_ref)
            units += [(qm[c0:c0 + ATT_SUB], m_ref, acc_ref, c0, h) for c0 in range(0, ATT_BLOCK, ATT_SUB)]
    _attn_sweep(pl.program_id(2), k_ref, vt_ref, 128, units)

    lv = lam_ref[...]
    lam = (jnp.exp(jnp.sum(lv[0:1] * lv[1:2], axis=-1, keepdims=True))
           - jnp.exp(jnp.sum(lv[2:3] * lv[3:4], axis=-1, keepdims=True)) + lam_init)
    outs = []
    for h in range(heads):
        o = (_attn_result(stats[4 * h + 1]) - lam * _attn_result(stats[4 * h + 3])).T
        o = o * lax.rsqrt(jnp.mean(o * o, axis=-1, keepdims=True) + NORM_EPS) * g_ref[0:1]
        outs.append(o * (1.0 - lam_init))
    o_ref[...] = jnp.concatenate(outs, axis=1)


def _da_attn(q, k, vt, lam_vec, sub_g, lam_init, batch, seq):
    n = batch * seq
    tb = ATT_BLOCK
    nq = seq // tb
    const = lambda b, h, i: (0, 0)
    row = pltpu.VMEM((1, tb), F32)
    acc = pltpu.VMEM((DV_AUG, tb), F32)
    return pl.pallas_call(
        functools.partial(_da_attn_kernel, lam_init),
        grid=(batch, DA_HEADS // DA_ATT_HEADS, nq),
        in_specs=_attn_specs(128, nq, seq, DA_ATT_HEADS)
        + [pl.BlockSpec((8, 128), const), pl.BlockSpec((8, 128), const)],
        out_specs=pl.BlockSpec((tb, 128 * DA_ATT_HEADS), lambda b, h, i: (b * nq + i, h)),
        out_shape=jax.ShapeDtypeStruct((n, 512), F32),
        scratch_shapes=[row, acc] * (2 * DA_ATT_HEADS),
        compiler_params=_params("arbitrary", "arbitrary", "arbitrary"),
        name="da_attn",
    )(q, k, vt, lam_vec, sub_g)


def _outproj_kernel(ya_ref, yb_ref, yc_ref, yd_ref, z_ref, beta_ref, x_ref, w_ref, o_ref):
    y_refs = (ya_ref, yb_ref, yc_ref, yd_ref)
    acc = x_ref[...]
    for c in range(4):
        sl = slice(GROUP_W * c, GROUP_W * (c + 1))
        z = z_ref[:, sl]
        gated = (y_refs[c][...] * beta_ref[0:1, sl] * (z * _sigmoid(z))).astype(BF16)
        acc = acc + jnp.dot(gated, w_ref[sl, :], preferred_element_type=F32)
    o_ref[...] = acc


def _outproj(ys, p_all, beta_row, x2, w, tm=512):
    n = x2.shape[0]
    yspec = pl.BlockSpec((tm, 512), lambda i: (i, 0))
    resident = dict(pipeline_mode=pl.Buffered(1))
    return pl.pallas_call(
        _outproj_kernel,
        grid=(n // tm,),
        in_specs=[yspec] * 4 + [pl.BlockSpec((tm, D_MODEL), lambda i: (i, COL_Z // D_MODEL))]
        + [pl.BlockSpec((8, D_MODEL), lambda i: (0, 0), **resident),
           pl.BlockSpec((tm, D_MODEL), lambda i: (i, 0)),
           pl.BlockSpec((D_MODEL, D_MODEL), lambda i: (0, 0), **resident)],
        out_specs=pl.BlockSpec((tm, D_MODEL), lambda i: (i, 0)),
        out_shape=jax.ShapeDtypeStruct((n, D_MODEL), F32),
        compiler_params=_params("arbitrary"),
        name="outproj",
    )(*ys, p_all, beta_row, x2, w)


def _rows(rows, width, n_rows=8):
    padded = [jnp.pad(r.astype(F32), (0, width - r.shape[0])) for r in rows]
    padded += [jnp.zeros((width,), F32)] * (n_rows - len(rows))
    return jnp.stack(padded)


def _pad_to(a, shape):
    return jnp.pad(a, [(0, s - d) for s, d in zip(shape, a.shape)])


def _arrange_w_in(w):
    w = w.astype(BF16)
    zeros = lambda c: jnp.zeros((D_MODEL, c), BF16)
    return jnp.concatenate([
        w[:, 5704:6408], zeros(64),
        w[:, 3584:3648], zeros(64),
        w[:, 5696:5704], zeros(120),
        w[:, 2048:3584],
        w[:, 3648:5696],
        w[:, 6408:7944],
        w[:, 0:2048],
    ], axis=1).reshape(D_MODEL, N_PROJ // INPROJ_TN, INPROJ_TN).transpose(1, 0, 2)


def _rope_tables(seq, rot, seg, fill):
    half = rot // 2
    inv = ROPE_THETA ** (-jnp.arange(0, rot, 2, dtype=F32) / rot)
    ang = jnp.arange(seq, dtype=F32)[:, None] * inv[None, :]
    cos, sin = jnp.cos(ang), jnp.sin(ang)
    rest = seg - rot
    cos_s = jnp.concatenate([cos, cos, jnp.full((seq, rest), fill, F32)], axis=1)
    sin_s = jnp.concatenate([-sin, sin, jnp.zeros((seq, rest), F32)], axis=1)
    reps = 128 // seg
    return jnp.tile(cos_s, (1, reps)), jnp.tile(sin_s, (1, reps))


def kernel(x, norm_g, w_in, w_out, branch_beta, rw_mu, rw_w0, rw_w_up, rw_a0, rw_a_up, rw_v0, rw_v_dn, rw_v_up, rw_k_k, rw_k_a, rw_r_k, rw_ln_w, rw_ln_b, ml_conv_w, ml_conv_b, ml_i_b, ml_f_b, ml_norm_g, mla_q_norm_g, mla_kv_norm_g, mla_w_q_b, mla_w_kv_b, mla_q_g, mla_k_g, da_q_g, da_k_g, da_lq1, da_lk1, da_lq2, da_lk2, da_sub_g):
    batch, seq, _ = x.shape
    depth = w_in.shape[0]
    n = batch * seq
    x2 = x.reshape(n, D_MODEL)

    seg_id = jnp.arange(GROUP_W) // RW_HEAD
    ones64 = (seg_id[:, None] == seg_id[None, :]).astype(BF16)
    mla_cos, mla_sin = _rope_tables(seq, MLA_ROPE, 128, 0.0)
    da_cos, da_sin = _rope_tables(seq, DA_ROT, DA_QK, 1.0)

    v_first = None
    for l in range(depth):
        p_all = _inproj(x2, norm_g[l][None, :], _arrange_w_in(w_in[l]))

        mu = rw_mu[l]
        vec = _rows([mu[0:512], mu[512:1024], mu[1024:1536], rw_w0[l], rw_a0[l], rw_k_k[l], rw_k_a[l]]
                    + ([rw_v0[l - 1]] if l > 0 else []), 512)
        mulo = _rows([mu[1536:1600]], 128)
        wup = _pad_to(rw_w_up[l], (128, 512)).astype(BF16)
        aup = jnp.zeros((128, 512), F32).at[32:64].set(rw_a_up[l]).astype(BF16)
        if l > 0:
            vdn = _pad_to(rw_v_dn[l - 1], (512, 128)).astype(BF16)
            vup = _pad_to(rw_v_up[l - 1], (128, 512)).astype(BF16)
        else:
            vdn = vup = None
        par = _rows([rw_r_k[l].reshape(-1), rw_ln_w[l], rw_ln_b[l]], 512)
        y_rw, v_first = _rw(p_all, v_first, vec, mulo, wup, aup, vdn, vup, par, ones64, batch, seq)

        cw = ml_conv_w[l]
        cb = ml_conv_b[l]
        ml_vec = _rows([cb[0:512], cb[512:1024], ml_norm_g[l]], 512)
        gb = _rows([jnp.concatenate([ml_i_b[l], ml_f_b[l]])], 128)
        y_ml = _ml(p_all, _pad_to(cw[:, 0:512], (8, 512)), _pad_to(cw[:, 512:1024], (8, 512)),
                   ml_vec, gb, batch, seq)

        scale = MLA_QK ** -0.5 * LOG2E
        qg = jnp.tile(jnp.concatenate([mla_q_g[l] * scale, jnp.zeros((64,), F32)]), MLA_HEADS)
        mla_vec = _rows([mla_q_norm_g[l], mla_kv_norm_g[l], qg, mla_k_g[l][0:128], mla_k_g[l][128:192]], 1024)
        wq = _pad_to(mla_w_q_b[l].reshape(MLA_Q_LORA, MLA_HEADS, MLA_QK),
                     (MLA_Q_LORA, MLA_HEADS, MLA_PAD)).reshape(MLA_Q_LORA, 1024).astype(BF16)
        wkv = mla_w_kv_b[l].reshape(MLA_KV_LORA, MLA_HEADS, 2, 128).transpose(0, 2, 1, 3)
        wkv = wkv.reshape(MLA_KV_LORA, 1024).astype(BF16)
        mq, mk, mv = _mla_prep(p_all, mla_cos, mla_sin, mla_vec, wq, wkv, batch, seq)
        y_mla = _mla_attn(mq, mk, mv, batch, seq)

        da_vec = _rows([jnp.tile(da_q_g[l], 8) * (DA_QK ** -0.5 * LOG2E), jnp.tile(da_k_g[l], 8)], 512)
        dq, dk, dv = _da_prep(p_all, da_cos, da_sin, da_vec, ones64, batch, seq)
        lam_vec = _rows([da_lq1[l], da_lk1[l], da_lq2[l], da_lk2[l]], 128)
        lam_init = 0.8 - 0.6 * math.exp(-0.3 * l)
        y_da = _da_attn(dq, dk, dv, lam_vec, _rows([da_sub_g[l]], 128), lam_init, batch, seq)

        beta_row = _rows([jnp.repeat(branch_beta[l], GROUP_W)], D_MODEL)
        x2 = _outproj((y_rw, y_ml, y_mla, y_da), p_all, beta_row, x2, w_out[l].astype(BF16))
    return x2.reshape(batch, seq, D_MODEL)
```

```python
import functools
import math

import jax
import jax.numpy as jnp
from jax import lax
from jax.experimental import pallas as pl
from jax.experimental.pallas import tpu as pltpu

F32 = jnp.float32
BF16 = jnp.bfloat16

D_MODEL = 2048
GROUP_W = 512
NORM_EPS = 1e-6
ROPE_THETA = 500000.0

RW_HEAD = 64
RW_GN_EPS = 64e-5
RW_CHUNK = 64
ML_HEADS = 4
ML_HEAD = 128
ML_CHUNK = 256
MLA_HEADS = 4
MLA_NOPE = 128
MLA_ROPE = 64
MLA_QK = 192
MLA_Q_LORA = 384
MLA_KV_LORA = 256
MLA_PAD = 256
DA_HEADS = 4
DA_QK = 64
DA_ROT = 16
ATT_BLOCK = 512
ATT_SUB = 256
MLA_ATT_HEADS = 4
DA_ATT_HEADS = 4
DV_AUG = 144
LOG2E = math.log2(math.e)

N_PROJ = 8192
COL_MLA = 0
COL_RW_LORA = 768
COL_ML_GATE = 896
COL_RW = 1024
COL_ML = 2560
COL_DA = 4608
COL_Z = 6144

INPROJ_TN = 1024

VMEM_LIMIT = 48 * 1024 * 1024


def _dot(a, b):
    return jnp.dot(a.astype(BF16), b.astype(BF16), preferred_element_type=F32)


def _dot_nt(a, b):
    return lax.dot_general(a.astype(BF16), b.astype(BF16), (((1,), (1,)), ((), ())),
                           preferred_element_type=F32)


def _dot_tn(a, b):
    return lax.dot_general(a.astype(BF16), b.astype(BF16), (((0,), (0,)), ((), ())),
                           preferred_element_type=F32)


def _split3(x):
    h1 = x.astype(BF16)
    r1 = x - h1.astype(F32)
    h2 = r1.astype(BF16)
    h3 = (r1 - h2.astype(F32)).astype(BF16)
    return h1, h2, h3


def _dot_exact_lhs(a_bf16, x):
    h1, h2, h3 = _split3(x)
    f = functools.partial(jnp.dot, preferred_element_type=F32)
    return f(a_bf16, h1) + f(a_bf16, h2) + f(a_bf16, h3)


def _seg_sum(x, ones_bf16):
    hi = x.astype(BF16)
    lo = (x - hi.astype(F32)).astype(BF16)
    f = functools.partial(jnp.dot, preferred_element_type=F32)
    return f(hi, ones_bf16) + f(lo, ones_bf16)


def _sigmoid(x):
    return 1.0 / (1.0 + jnp.exp(-x))


def _log_sigmoid(x):
    return jnp.minimum(x, 0.0) - jnp.log(1.0 + jnp.exp(-jnp.abs(x)))


def _params(*sem):
    return pltpu.CompilerParams(dimension_semantics=sem, vmem_limit_bytes=VMEM_LIMIT)


def _inproj_kernel(x_ref, g_ref, w_ref, o_ref, h_ref):
    @pl.when(pl.program_id(1) == 0)
    def _():
        x = x_ref[...]
        ms = jnp.mean(x * x, axis=-1, keepdims=True)
        h_ref[...] = (x * lax.rsqrt(ms + NORM_EPS) * g_ref[...]).astype(BF16)

    o_ref[...] = jnp.dot(h_ref[...], w_ref[...], preferred_element_type=F32)


def _inproj(x2, g, w, tm=1024):
    n = x2.shape[0]
    tn = w.shape[2]
    return pl.pallas_call(
        _inproj_kernel,
        grid=(n // tm, N_PROJ // tn),
        in_specs=[pl.BlockSpec((tm, D_MODEL), lambda i, j: (i, 0)),
                  pl.BlockSpec((1, D_MODEL), lambda i, j: (0, 0)),
                  pl.BlockSpec((None, D_MODEL, tn), lambda i, j: (j, 0, 0))],
        out_specs=pl.BlockSpec((tm, tn), lambda i, j: (i, j)),
        out_shape=jax.ShapeDtypeStruct((n, N_PROJ), F32),
        scratch_shapes=[pltpu.VMEM((tm, D_MODEL), BF16)],
        compiler_params=_params("arbitrary", "arbitrary"),
        name="inproj",
    )(x2, g, w)


def _rw_kernel(has_vres, *refs):
    if has_vres:
        (rr_ref, kr_ref, vr_ref, lo_ref, vf_ref, vec_ref, mulo_ref, wup_ref, aup_ref, vdn_ref, vup_ref,
         par_ref, ones_ref, y_ref, s_ref, cr_ref, ck_ref, cv_ref, cl_ref) = refs
    else:
        (rr_ref, kr_ref, vr_ref, lo_ref, vec_ref, mulo_ref, wup_ref, aup_ref,
         par_ref, ones_ref, y_ref, vo_ref, s_ref, cr_ref, ck_ref, cv_ref, cl_ref) = refs
    L = RW_CHUNK
    batch = rr_ref.shape[0]

    @pl.when(pl.program_id(0) == 0)
    def _():
        for ref in (s_ref, cr_ref, ck_ref, cv_ref, cl_ref):
            ref[...] = jnp.zeros_like(ref)

    def shift(x_ref, carry_ref, mu):
        outs = []
        for bi in range(batch):
            x = x_ref[bi]
            row = lax.broadcasted_iota(jnp.int32, x.shape, 0)
            prev = jnp.where(row == 0, carry_ref[bi, 0:1, :], pltpu.roll(x, 1, 0))
            carry_ref[bi, 0:1, :] = x[L - 1:L, :]
            outs.append(x + (prev - x) * mu)
        return jnp.concatenate(outs, axis=0)

    vec = vec_ref[...]
    ones = ones_ref[...]
    r_all = shift(rr_ref, cr_ref, vec[0:1])
    k_all = shift(kr_ref, ck_ref, vec[1:2])
    v_all = shift(vr_ref, cv_ref, vec[2:3])
    lo = shift(lo_ref, cl_ref, mulo_ref[0:1])

    lw_all = -math.exp(-0.5) * _sigmoid(vec[3:4] + _dot(jnp.tanh(lo), wup_ref[...]))
    a_all = _sigmoid(vec[4:5] + _dot(lo, aup_ref[...]))
    if has_vres:
        gate = _sigmoid(vec[7:8] + _dot(_dot(v_all, vdn_ref[...]), vup_ref[...]))
        vf = jnp.concatenate([vf_ref[bi] for bi in range(batch)], axis=0)
        v_all = v_all + (vf - v_all) * gate
    else:
        for bi in range(batch):
            vo_ref[bi] = v_all[L * bi:L * (bi + 1)]
    kk_all = k_all * vec[5:6]
    kk_all = kk_all / jnp.maximum(jnp.sqrt(_seg_sum(kk_all * kk_all, ones)), 1e-12)
    k_all = k_all * (1.0 + (a_all - 1.0) * vec[6:7])

    row = lax.broadcasted_iota(jnp.int32, (L, 128), 0)
    lane = lax.broadcasted_iota(jnp.int32, (L, 128), 1)
    col = lane & (RW_HEAD - 1)
    strict = row > col
    incl = row >= col
    eye = (row == col).astype(F32)
    first = lane < RW_HEAD
    r128 = lax.broadcasted_iota(jnp.int32, (128, 128), 0)
    c128 = lax.broadcasted_iota(jnp.int32, (128, 128), 1)
    same_head = (r128 < RW_HEAD) == (c128 < RW_HEAD)
    tri = (lax.broadcasted_iota(jnp.int32, (L, L), 0) >= lax.broadcasted_iota(jnp.int32, (L, L), 1)).astype(BF16)

    def bd(x):
        return jnp.concatenate([jnp.where(first, x, 0.0), jnp.where(first, 0.0, x)], axis=0)

    n_pairs = GROUP_W // 128
    units = [(bi, pr) for bi in range(batch) for pr in range(n_pairs)]
    idx = range(len(units))
    al, rt, bet, kt, beh, kh, p_chunk, vs = [], [], [], [], [], [], [], []
    for bi in range(batch):
        rows = slice(L * bi, L * (bi + 1))
        lw = lw_all[rows]
        r = r_all[rows]
        k = k_all[rows]
        kk = kk_all[rows]
        b = _dot_exact_lhs(tri, lw)
        b_last = b[L - 1:L, :]
        p_inv = jnp.exp(-b)
        p_end = jnp.exp(b_last - b)
        be = kk * a_all[rows]
        al.append(-(kk * jnp.exp(b - lw)))
        rt.append(r * jnp.exp(b))
        bet.append(be * p_inv)
        kt.append(k * p_inv)
        beh.append(be * p_end)
        kh.append(k * p_end)
        p_chunk.append(jnp.exp(b_last))
        vs.append(v_all[rows])

    def part(xs, i):
        bi, pr = units[i]
        return xs[bi][:, 128 * pr:128 * pr + 128]

    lhs = [jnp.concatenate([part(al, i), part(rt, i)], axis=0).astype(BF16) for i in idx]
    rhs = [jnp.concatenate([bd(part(bet, i)), bd(part(kt, i))], axis=0).astype(BF16) for i in idx]
    g = [_dot_nt(lhs[i], rhs[i]) for i in idx]
    a_ab = [jnp.where(strict, g[i][0:L, 0:128], 0.0) for i in idx]
    a_ak = [jnp.where(strict, g[i][0:L, 128:256], 0.0) for i in idx]
    b_rb = [jnp.where(incl, g[i][L:2 * L, 0:128], 0.0) for i in idx]
    b_rk = [jnp.where(incl, g[i][L:2 * L, 128:256], 0.0) for i in idx]
    s = [s_ref[i] for i in idx]
    sa = [_dot_nt(lhs[i], s[i]) for i in idx]
    vv = [_dot(jnp.concatenate([a_ak[i], b_rk[i]], axis=0), bd(part(vs, i))) for i in idx]

    t_inv = [eye + a_ab[i] for i in idx]
    x = [_dot(a_ab[i], bd(a_ab[i])) for i in idx]
    for _ in range(4):
        xt = [_dot(jnp.concatenate([x[i], t_inv[i]], axis=0), bd(x[i])) for i in idx]
        x = [xt[i][0:L] for i in idx]
        t_inv = [t_inv[i] + xt[i][L:2 * L] for i in idx]
    t_inv = [t_inv[i] + _dot(t_inv[i], bd(x[i])) for i in idx]

    u = [_dot(t_inv[i], bd(sa[i][0:L] + vv[i][0:L])) for i in idx]
    ys = [sa[i][L:2 * L] + vv[i][L:2 * L] + _dot(b_rb[i], bd(u[i])) for i in idx]
    for i in idx:
        upd = _dot_tn(jnp.concatenate([u[i], part(vs, i)], axis=0),
                      jnp.concatenate([part(beh, i), part(kh, i)], axis=0))
        s_ref[i] = s[i] * part(p_chunk, i) + jnp.where(same_head, upd, 0.0)

    par = par_ref[...]
    inv_n = 1.0 / RW_HEAD
    rows_all = batch * L
    y_all = jnp.concatenate([jnp.concatenate(ys[n_pairs * bi:n_pairs * (bi + 1)], axis=1)
                             for bi in range(batch)], axis=0)
    sums = _dot(jnp.concatenate([y_all, r_all * k_all * par[0:1]], axis=0), ones)
    yc = y_all - sums[0:rows_all] * inv_n
    var = _dot(yc * yc, ones) * inv_n
    out = yc * lax.rsqrt(var + RW_GN_EPS) * par[1:2] + par[2:3] + sums[rows_all:2 * rows_all] * v_all
    for bi in range(batch):
        y_ref[bi] = out[L * bi:L * (bi + 1)]


def _rw(p_all, v_first, vec, mulo, wup, aup, vdn, vup, par, ones64, batch, seq):
    nc = seq // RW_CHUNK
    has_vres = v_first is not None
    p3 = p_all.reshape(batch, seq, N_PROJ)

    def cols(width, c):
        return pl.BlockSpec((batch, RW_CHUNK, width), lambda i: (0, i, c))

    const = lambda i: (0, 0)
    blk = cols(512, 0)
    in_specs = [cols(512, COL_RW // 512), cols(512, COL_RW // 512 + 1), cols(512, COL_RW // 512 + 2),
                cols(128, COL_RW_LORA // 128)]
    args = [p3, p3, p3, p3]
    if has_vres:
        in_specs.append(blk)
        args.append(v_first.reshape(batch, seq, 512))
    in_specs += [pl.BlockSpec((8, 512), const), pl.BlockSpec((8, 128), const),
                 pl.BlockSpec((128, 512), const), pl.BlockSpec((128, 512), const)]
    args += [vec, mulo, wup, aup]
    if has_vres:
        in_specs += [pl.BlockSpec((512, 128), const), pl.BlockSpec((128, 512), const)]
        args += [vdn, vup]
    in_specs += [pl.BlockSpec((8, 512), const), pl.BlockSpec((512, 512), const)]
    args += [par, ones64]
    out = jax.ShapeDtypeStruct((batch, seq, 512), F32)
    n_out = 1 if has_vres else 2
    res = pl.pallas_call(
        functools.partial(_rw_kernel, has_vres),
        grid=(nc,),
        in_specs=in_specs,
        out_specs=[blk] * n_out,
        out_shape=[out] * n_out,
        scratch_shapes=[pltpu.VMEM((batch * (GROUP_W // 128), 128, 128), F32)]
        + [pltpu.VMEM((batch, 8, 512), F32)] * 3 + [pltpu.VMEM((batch, 8, 128), F32)],
        compiler_params=_params("arbitrary"),
        name="rwkv7",
    )(*args)
    y = res[0].reshape(batch * seq, 512)
    return y, (v_first if has_vres else res[1].reshape(batch * seq, 512))


def _ml_kernel(q_ref, k_ref, v_ref, o_ref, g_ref, cwq_ref, cwk_ref, vec_ref, gb_ref, y_ref,
               hq_ref, hk_ref, c_ref, n_ref, m_ref):
    L = ML_CHUNK

    @pl.when(pl.program_id(1) == 0)
    def _():
        hq_ref[...] = jnp.zeros_like(hq_ref)
        hk_ref[...] = jnp.zeros_like(hk_ref)
        c_ref[...] = jnp.zeros_like(c_ref)
        n_ref[...] = jnp.zeros_like(n_ref)
        m_ref[...] = jnp.zeros_like(m_ref)

    vec = vec_ref[...]

    def conv(x_ref, buf_ref, w_ref, bias):
        x = x_ref[...]
        w = w_ref[...]
        buf_ref[8:8 + L, :] = x
        acc = x * w[3:4] + bias
        for s in (1, 2, 3):
            acc = acc + buf_ref[8 - s:8 - s + L, :] * w[3 - s:4 - s]
        buf_ref[0:8, :] = x[L - 8:L]
        return acc * _sigmoid(acc)

    q = conv(q_ref, hq_ref, cwq_ref, vec[0:1])
    k = conv(k_ref, hk_ref, cwk_ref, vec[1:2]) * (ML_HEAD ** -0.5)
    v = v_ref[...]
    og = _sigmoid(o_ref[...])

    g = g_ref[...] + gb_ref[0:1]
    lane = lax.broadcasted_iota(jnp.int32, (L, 128), 1)
    tri_b = lax.broadcasted_iota(jnp.int32, (L, L), 0) >= lax.broadcasted_iota(jnp.int32, (L, L), 1)
    cum = _dot_exact_lhs(tri_b.astype(BF16), _log_sigmoid(g))
    comb_t = jnp.where(lane < ML_HEADS, g, cum).T

    H = range(ML_HEADS)
    sls = [slice(128 * h, 128 * h + 128) for h in H]
    qs = [q[:, sl] for sl in sls]
    ks = [k[:, sl] for sl in sls]
    vs = [v[:, sl] for sl in sls]
    qk = [_dot_nt(qs[h], ks[h]) for h in H]
    c_old = [c_ref[h] for h in H]
    n_old = [n_ref[h:h + 1, :] for h in H]
    m_prev = [m_ref[h:h + 1, 0:1] for h in H]
    qc = [_dot_nt(qs[h], c_old[h]) for h in H]
    b_col = [cum[:, ML_HEADS + h:ML_HEADS + h + 1] for h in H]
    ig_col = [g[:, h:h + 1] for h in H]
    dm = [jnp.where(tri_b, b_col[h] - comb_t[ML_HEADS + h:ML_HEADS + h + 1, :] + comb_t[h:h + 1, :], -jnp.inf)
          for h in H]
    inter = [b_col[h] + m_prev[h] for h in H]
    mj = [jnp.maximum(inter[h], jnp.max(dm[h], axis=-1, keepdims=True)) for h in H]
    s = [qk[h] * jnp.exp(dm[h] - mj[h]) for h in H]
    iw = [jnp.exp(inter[h] - mj[h]) for h in H]
    num = [_dot(s[h], vs[h]) + iw[h] * qc[h] for h in H]
    den = [jnp.sum(s[h], axis=-1, keepdims=True) + iw[h] * jnp.sum(qs[h] * n_old[h], axis=-1, keepdims=True)
           for h in H]
    hh = [num[h] / jnp.maximum(jnp.abs(den[h]), jnp.exp(-mj[h])) for h in H]

    b_last = [b_col[h][L - 1:L, :] for h in H]
    g_col = [b_last[h] - b_col[h] + ig_col[h] for h in H]
    m_new = [jnp.maximum(b_last[h] + m_prev[h], jnp.max(g_col[h], axis=0, keepdims=True)) for h in H]
    a_old = [jnp.exp(b_last[h] + m_prev[h] - m_new[h]) for h in H]
    wg = [jnp.exp(g_col[h] - m_new[h]) for h in H]
    kv = [_dot_tn(vs[h] * wg[h], ks[h]) for h in H]
    outs = []
    for h in H:
        c_ref[h] = a_old[h] * c_old[h] + kv[h]
        n_ref[h:h + 1, :] = a_old[h] * n_old[h] + jnp.sum(ks[h] * wg[h], axis=0, keepdims=True)
        m_ref[h:h + 1, :] = jnp.broadcast_to(m_new[h], (1, 128))
        hn = hh[h] * lax.rsqrt(jnp.mean(hh[h] * hh[h], axis=-1, keepdims=True) + NORM_EPS) * vec[2:3, sls[h]]
        outs.append(og[:, sls[h]] * hn)
    y_ref[...] = jnp.concatenate(outs, axis=1)


def _ml(p_all, cwq, cwk, vec, gb, batch, seq):
    n = batch * seq
    nc = seq // ML_CHUNK

    def rows(c):
        return lambda b, i: (b * nc + i, c)

    const = lambda b, i: (0, 0)
    base = COL_ML // 512
    return pl.pallas_call(
        _ml_kernel,
        grid=(batch, nc),
        in_specs=[pl.BlockSpec((ML_CHUNK, 512), rows(base)),
                  pl.BlockSpec((ML_CHUNK, 512), rows(base + 1)),
                  pl.BlockSpec((ML_CHUNK, 512), rows(base + 2)),
                  pl.BlockSpec((ML_CHUNK, 512), rows(base + 3)),
                  pl.BlockSpec((ML_CHUNK, 128), rows(COL_ML_GATE // 128)),
                  pl.BlockSpec((8, 512), const), pl.BlockSpec((8, 512), const),
                  pl.BlockSpec((8, 512), const), pl.BlockSpec((8, 128), const)],
        out_specs=pl.BlockSpec((ML_CHUNK, 512), rows(0)),
        out_shape=jax.ShapeDtypeStruct((n, 512), F32),
        scratch_shapes=[pltpu.VMEM((ML_CHUNK + 8, 512), F32), pltpu.VMEM((ML_CHUNK + 8, 512), F32),
                        pltpu.VMEM((ML_HEADS, 128, 128), F32), pltpu.VMEM((8, 128), F32),
                        pltpu.VMEM((8, 128), F32)],
        compiler_params=_params("arbitrary", "arbitrary"),
        name="mlstm",
    )(p_all, p_all, p_all, p_all, p_all, cwq, cwk, vec, gb)


def _rope128(x, cos_t, sin_t, swap):
    return x * cos_t + jnp.dot(x.astype(BF16), swap, preferred_element_type=F32) * sin_t


def _swap_matrix(half, seg):
    lane = jnp.arange(128)
    pos = lane % seg
    src = jnp.where(pos < half, lane + half, jnp.where(pos < 2 * half, lane - half, lane))
    return (lane[:, None] == src[None, :]).astype(BF16)


def _mla_prep_kernel(p_ref, cos_ref, sin_ref, vec_ref, wq_ref, wkv_ref, swap_ref, q_ref, k_ref, vt_ref):
    p = p_ref[...]
    vec = vec_ref[...]
    cos_t = cos_ref[...]
    sin_t = sin_ref[...]
    swap = swap_ref[...]
    q_lat = p[:, 0:MLA_Q_LORA]
    kv_lat = p[:, MLA_Q_LORA:MLA_Q_LORA + MLA_KV_LORA]
    k_pe = p[:, 640:768]
    qn = q_lat * lax.rsqrt(jnp.mean(q_lat * q_lat, axis=-1, keepdims=True) + NORM_EPS) * vec[0:1, 0:MLA_Q_LORA]
    kvn = kv_lat * lax.rsqrt(jnp.mean(kv_lat * kv_lat, axis=-1, keepdims=True) + NORM_EPS) * vec[1:2, 0:MLA_KV_LORA]
    q4 = _dot(qn, wq_ref[...])
    kv4 = _dot(kvn, wkv_ref[...])
    pe_ss = jnp.sum(k_pe * k_pe, axis=-1, keepdims=True)
    pe_rot = _rope128(k_pe * vec[4:5, 0:128], cos_t, sin_t, swap)
    qs, ks = [], []
    for h in range(MLA_HEADS):
        qh = q4[:, MLA_PAD * h:MLA_PAD * (h + 1)]
        inv = lax.rsqrt(jnp.sum(qh * qh, axis=-1, keepdims=True) * (1.0 / MLA_QK) + NORM_EPS)
        qh = qh * inv * vec[2:3, MLA_PAD * h:MLA_PAD * (h + 1)]
        qs += [qh[:, 0:128], _rope128(qh[:, 128:256], cos_t, sin_t, swap)]
        kn = kv4[:, 128 * h:128 * (h + 1)]
        inv = lax.rsqrt((jnp.sum(kn * kn, axis=-1, keepdims=True) + pe_ss) * (1.0 / MLA_QK) + NORM_EPS)
        ks += [kn * inv * vec[3:4, 0:128], pe_rot * inv]
    q_ref[...] = jnp.concatenate(qs, axis=1).astype(BF16)
    k_ref[...] = jnp.concatenate(ks, axis=1).astype(BF16)
    _store_vt(vt_ref, kv4[:, 512:1024])


def _store_vt(vt_ref, v):
    heads, _, rows = vt_ref.shape
    eye = (lax.broadcasted_iota(jnp.int32, (128, 128), 0)
           == lax.broadcasted_iota(jnp.int32, (128, 128), 1)).astype(BF16)
    ones = jnp.ones((DV_AUG - 128, rows), BF16)
    for h in range(heads):
        vt = _dot_nt(eye, v[:, 128 * h:128 * (h + 1)]).astype(BF16)
        vt_ref[h] = jnp.concatenate([vt, ones], axis=0)


def _vt_spec(heads, nt):
    return pl.BlockSpec((None, heads, None, DV_AUG, ATT_BLOCK), lambda i: (i // nt, 0, i % nt, 0, 0))


def _mla_prep(p_all, cos_t, sin_t, vec, wq, wkv, batch, seq):
    n = batch * seq
    tr = ATT_BLOCK
    nt = seq // tr
    const = lambda i: (0, 0)
    return pl.pallas_call(
        _mla_prep_kernel,
        grid=(n // tr,),
        in_specs=[pl.BlockSpec((tr, 768), lambda i: (i, COL_MLA // 768)),
                  pl.BlockSpec((tr, 128), lambda i: (i % nt, 0)),
                  pl.BlockSpec((tr, 128), lambda i: (i % nt, 0)),
                  pl.BlockSpec((8, 1024), const),
                  pl.BlockSpec((MLA_Q_LORA, 1024), const),
                  pl.BlockSpec((MLA_KV_LORA, 1024), const),
                  pl.BlockSpec((128, 128), const)],
        out_specs=[pl.BlockSpec((tr, 1024), lambda i: (i, 0)),
                   pl.BlockSpec((tr, 1024), lambda i: (i, 0)),
                   _vt_spec(MLA_HEADS, nt)],
        out_shape=[jax.ShapeDtypeStruct((n, 1024), BF16), jax.ShapeDtypeStruct((n, 1024), BF16),
                   jax.ShapeDtypeStruct((batch, MLA_HEADS, nt, DV_AUG, tr), BF16)],
        compiler_params=_params("arbitrary"),
        name="mla_prep",
    )(p_all, cos_t, sin_t, vec, wq, wkv, _swap_matrix(MLA_ROPE // 2, 128))


def _attn_block_t(kbs, vts, units, masked):
    sub = ATT_SUB
    s, cols = [], []
    for q, _, _, c0, h in units:
        keys = c0 + sub if masked else kbs[h].shape[0]
        st = _dot_nt(kbs[h][0:keys], q)
        if masked:
            keep = (lax.broadcasted_iota(jnp.int32, st.shape, 0)
                    <= lax.broadcasted_iota(jnp.int32, st.shape, 1) + c0)
            st = jnp.where(keep, st, -jnp.inf)
        s.append(st)
        cols.append(slice(c0, c0 + sub))
    idx = range(len(units))
    m_old = [units[i][1][:, cols[i]] for i in idx]
    m_new = [jnp.maximum(m_old[i], jnp.max(s[i], axis=0, keepdims=True)) for i in idx]
    p = [jnp.exp2(s[i] - m_new[i]).astype(BF16) for i in idx]
    alpha = [jnp.exp2(m_old[i] - m_new[i]) for i in idx]
    for i in idx:
        _, m_ref, acc_ref, _, h = units[i]
        pv = jnp.dot(vts[h][:, 0:s[i].shape[0]], p[i], preferred_element_type=F32)
        acc_ref[:, cols[i]] = alpha[i] * acc_ref[:, cols[i]] + pv
        m_ref[:, cols[i]] = m_new[i]


def _init_stats(m_ref, acc_ref):
    m_ref[...] = jnp.full_like(m_ref, -jnp.inf)
    acc_ref[...] = jnp.zeros_like(acc_ref)


def _attn_sweep(i, k_ref, vt_ref, dk, units):
    tb = ATT_BLOCK
    heads = vt_ref.shape[0]

    def block(j, masked):
        start = pl.multiple_of(j * tb, tb)
        kbs = [k_ref[pl.ds(start, tb), dk * h:dk * (h + 1)] for h in range(heads)]
        _attn_block_t(kbs, [vt_ref[h, j] for h in range(heads)], units, masked)

    def body(j, carry):
        block(j, False)
        return carry

    lax.fori_loop(0, i, body, 0)
    block(i, True)


def _attn_result(acc_ref):
    acc = acc_ref[...]
    return acc[0:128] / acc[128:129]


def _attn_specs(dk, nq, seq, heads):
    tb = ATT_BLOCK
    return [pl.BlockSpec((tb, dk * heads), lambda b, h, i: (b * nq + i, h)),
            pl.BlockSpec((seq, dk * heads), lambda b, h, i: (b, h)),
            pl.BlockSpec((None, heads, nq, DV_AUG, tb), lambda b, h, i: (b, h, 0, 0, 0))]


def _mla_attn_kernel(q_ref, k_ref, vt_ref, o_ref, *stats):
    heads = vt_ref.shape[0]
    units = []
    for h in range(heads):
        m_ref, acc_ref = stats[2 * h:2 * h + 2]
        _init_stats(m_ref, acc_ref)
        units += [(q_ref[c0:c0 + ATT_SUB, MLA_PAD * h:MLA_PAD * (h + 1)], m_ref, acc_ref, c0, h)
                  for c0 in range(0, ATT_BLOCK, ATT_SUB)]
    _attn_sweep(pl.program_id(2), k_ref, vt_ref, MLA_PAD, units)
    o_ref[...] = jnp.concatenate([_attn_result(stats[2 * h + 1]).T for h in range(heads)], axis=1)


def _mla_attn(q, k, vt, batch, seq):
    n = batch * seq
    tb = ATT_BLOCK
    nq = seq // tb
    return pl.pallas_call(
        _mla_attn_kernel,
        grid=(batch, MLA_HEADS // MLA_ATT_HEADS, nq),
        in_specs=_attn_specs(MLA_PAD, nq, seq, MLA_ATT_HEADS),
        out_specs=pl.BlockSpec((tb, 128 * MLA_ATT_HEADS), lambda b, h, i: (b * nq + i, h)),
        out_shape=jax.ShapeDtypeStruct((n, 512), F32),
        scratch_shapes=[pltpu.VMEM((1, tb), F32), pltpu.VMEM((DV_AUG, tb), F32)] * MLA_ATT_HEADS,
        compiler_params=_params("arbitrary", "arbitrary", "arbitrary"),
        name="mla_attn",
    )(q, k, vt)


def _da_prep_kernel(q_ref, k_ref, v_ref, cos_ref, sin_ref, vec_ref, ones_ref, swap_ref, qo_ref, ko_ref, vo_ref):
    vec = vec_ref[...]
    ones = ones_ref[...]
    cos_t = cos_ref[...]
    sin_t = sin_ref[...]
    swap = swap_ref[...]

    def norm_rope(x, gain):
        ss = _dot(x * x, ones) * (1.0 / DA_QK)
        x = x * lax.rsqrt(ss + NORM_EPS) * gain
        parts = [_rope128(x[:, 128 * c:128 * (c + 1)], cos_t, sin_t, swap) for c in range(4)]
        return jnp.concatenate(parts, axis=1)

    qo_ref[...] = norm_rope(q_ref[...], vec[0:1]).astype(BF16)
    ko_ref[...] = norm_rope(k_ref[...], vec[1:2]).astype(BF16)
    _store_vt(vo_ref, v_ref[...])


def _da_prep(p_all, cos_t, sin_t, vec, ones64, batch, seq):
    n = batch * seq
    tr = ATT_BLOCK
    nt = seq // tr
    const = lambda i: (0, 0)
    base = COL_DA // 512
    out = jax.ShapeDtypeStruct((n, 512), BF16)
    return pl.pallas_call(
        _da_prep_kernel,
        grid=(n // tr,),
        in_specs=[pl.BlockSpec((tr, 512), lambda i: (i, base)),
                  pl.BlockSpec((tr, 512), lambda i: (i, base + 1)),
                  pl.BlockSpec((tr, 512), lambda i: (i, base + 2)),
                  pl.BlockSpec((tr, 128), lambda i: (i % nt, 0)),
                  pl.BlockSpec((tr, 128), lambda i: (i % nt, 0)),
                  pl.BlockSpec((8, 512), const),
                  pl.BlockSpec((512, 512), const),
                  pl.BlockSpec((128, 128), const)],
        out_specs=[pl.BlockSpec((tr, 512), lambda i: (i, 0))] * 2 + [_vt_spec(DA_HEADS, nt)],
        out_shape=[out, out, jax.ShapeDtypeStruct((batch, DA_HEADS, nt, DV_AUG, tr), BF16)],
        compiler_params=_params("arbitrary"),
        name="da_prep",
    )(p_all, p_all, p_all, cos_t, sin_t, vec, ones64, _swap_matrix(DA_ROT // 2, DA_QK))


def _da_attn_kernel(lam_init, q_ref, k_ref, vt_ref, lam_ref, g_ref, o_ref, *stats):
    heads = vt_ref.shape[0]
    lane = lax.broadcasted_iota(jnp.int32, (ATT_BLOCK, 128), 1)
    units = []
    for h in range(heads):
        q = q_ref[:, 128 * h:128 * (h + 1)]
        for mp in range(2):
            qm = jnp.where((lane < DA_QK) == (mp == 0), q, jnp.zeros_like(q))
            m_ref, acc_ref = stats[4 * h + 2 * mp:4 * h + 2 * mp + 2]
            _init_stats(m_ref, acc_ref)
            units += [(qm[c0:c0 + ATT_SUB], m_ref, acc_ref, c0, h) for c0 in range(0, ATT_BLOCK, ATT_SUB)]
    _attn_sweep(pl.program_id(2), k_ref, vt_ref, 128, units)

    lv = lam_ref[...]
    lam = (jnp.exp(jnp.sum(lv[0:1] * lv[1:2], axis=-1, keepdims=True))
           - jnp.exp(jnp.sum(lv[2:3] * lv[3:4], axis=-1, keepdims=True)) + lam_init)
    outs = []
    for h in range(heads):
        o = (_attn_result(stats[4 * h + 1]) - lam * _attn_result(stats[4 * h + 3])).T
        o = o * lax.rsqrt(jnp.mean(o * o, axis=-1, keepdims=True) + NORM_EPS) * g_ref[0:1]
        outs.append(o * (1.0 - lam_init))
    o_ref[...] = jnp.concatenate(outs, axis=1)


def _da_attn(q, k, vt, lam_vec, sub_g, lam_init, batch, seq):
    n = batch * seq
    tb = ATT_BLOCK
    nq = seq // tb
    const = lambda b, h, i: (0, 0)
    row = pltpu.VMEM((1, tb), F32)
    acc = pltpu.VMEM((DV_AUG, tb), F32)
    return pl.pallas_call(
        functools.partial(_da_attn_kernel, lam_init),
        grid=(batch, DA_HEADS // DA_ATT_HEADS, nq),
        in_specs=_attn_specs(128, nq, seq, DA_ATT_HEADS)
        + [pl.BlockSpec((8, 128), const), pl.BlockSpec((8, 128), const)],
        out_specs=pl.BlockSpec((tb, 128 * DA_ATT_HEADS), lambda b, h, i: (b * nq + i, h)),
        out_shape=jax.ShapeDtypeStruct((n, 512), F32),
        scratch_shapes=[row, acc] * (2 * DA_ATT_HEADS),
        compiler_params=_params("arbitrary", "arbitrary", "arbitrary"),
        name="da_attn",
    )(q, k, vt, lam_vec, sub_g)


def _outproj_kernel(ya_ref, yb_ref, yc_ref, yd_ref, z_ref, beta_ref, x_ref, w_ref, o_ref):
    y_refs = (ya_ref, yb_ref, yc_ref, yd_ref)
    acc = x_ref[...]
    for c in range(4):
        sl = slice(GROUP_W * c, GROUP_W * (c + 1))
        z = z_ref[:, sl]
        gated = (y_refs[c][...] * beta_ref[0:1, sl] * (z * _sigmoid(z))).astype(BF16)
        acc = acc + jnp.dot(gated, w_ref[sl, :], preferred_element_type=F32)
    o_ref[...] = acc


def _outproj(ys, p_all, beta_row, x2, w, tm=512):
    n = x2.shape[0]
    yspec = pl.BlockSpec((tm, 512), lambda i: (i, 0))
    resident = dict(pipeline_mode=pl.Buffered(1))
    return pl.pallas_call(
        _outproj_kernel,
        grid=(n // tm,),
        in_specs=[yspec] * 4 + [pl.BlockSpec((tm, D_MODEL), lambda i: (i, COL_Z // D_MODEL))]
        + [pl.BlockSpec((8, D_MODEL), lambda i: (0, 0), **resident),
           pl.BlockSpec((tm, D_MODEL), lambda i: (i, 0)),
           pl.BlockSpec((D_MODEL, D_MODEL), lambda i: (0, 0), **resident)],
        out_specs=pl.BlockSpec((tm, D_MODEL), lambda i: (i, 0)),
        out_shape=jax.ShapeDtypeStruct((n, D_MODEL), F32),
        compiler_params=_params("arbitrary"),
        name="outproj",
    )(*ys, p_all, beta_row, x2, w)


def _rows(rows, width, n_rows=8):
    padded = [jnp.pad(r.astype(F32), (0, width - r.shape[0])) for r in rows]
    padded += [jnp.zeros((width,), F32)] * (n_rows - len(rows))
    return jnp.stack(padded)


def _pad_to(a, shape):
    return jnp.pad(a, [(0, s - d) for s, d in zip(shape, a.shape)])


def _arrange_w_in(w):
    w = w.astype(BF16)
    zeros = lambda c: jnp.zeros((D_MODEL, c), BF16)
    return jnp.concatenate([
        w[:, 5704:6408], zeros(64),
        w[:, 3584:3648], zeros(64),
        w[:, 5696:5704], zeros(120),
        w[:, 2048:3584],
        w[:, 3648:5696],
        w[:, 6408:7944],
        w[:, 0:2048],
    ], axis=1).reshape(D_MODEL, N_PROJ // INPROJ_TN, INPROJ_TN).transpose(1, 0, 2)


def _rope_tables(seq, rot, seg, fill):
    half = rot // 2
    inv = ROPE_THETA ** (-jnp.arange(0, rot, 2, dtype=F32) / rot)
    ang = jnp.arange(seq, dtype=F32)[:, None] * inv[None, :]
    cos, sin = jnp.cos(ang), jnp.sin(ang)
    rest = seg - rot
    cos_s = jnp.concatenate([cos, cos, jnp.full((seq, rest), fill, F32)], axis=1)
    sin_s = jnp.concatenate([-sin, sin, jnp.zeros((seq, rest), F32)], axis=1)
    reps = 128 // seg
    return jnp.tile(cos_s, (1, reps)), jnp.tile(sin_s, (1, reps))


def kernel(x, norm_g, w_in, w_out, branch_beta, rw_mu, rw_w0, rw_w_up, rw_a0, rw_a_up, rw_v0, rw_v_dn, rw_v_up, rw_k_k, rw_k_a, rw_r_k, rw_ln_w, rw_ln_b, ml_conv_w, ml_conv_b, ml_i_b, ml_f_b, ml_norm_g, mla_q_norm_g, mla_kv_norm_g, mla_w_q_b, mla_w_kv_b, mla_q_g, mla_k_g, da_q_g, da_k_g, da_lq1, da_lk1, da_lq2, da_lk2, da_sub_g):
    batch, seq, _ = x.shape
    depth = w_in.shape[0]
    n = batch * seq
    x2 = x.reshape(n, D_MODEL)

    seg_id = jnp.arange(GROUP_W) // RW_HEAD
    ones64 = (seg_id[:, None] == seg_id[None, :]).astype(BF16)
    mla_cos, mla_sin = _rope_tables(seq, MLA_ROPE, 128, 0.0)
    da_cos, da_sin = _rope_tables(seq, DA_ROT, DA_QK, 1.0)

    v_first = None
    for l in range(depth):
        p_all = _inproj(x2, norm_g[l][None, :], _arrange_w_in(w_in[l]))

        mu = rw_mu[l]
        vec = _rows([mu[0:512], mu[512:1024], mu[1024:1536], rw_w0[l], rw_a0[l], rw_k_k[l], rw_k_a[l]]
                    + ([rw_v0[l - 1]] if l > 0 else []), 512)
        mulo = _rows([mu[1536:1600]], 128)
        wup = _pad_to(rw_w_up[l], (128, 512)).astype(BF16)
        aup = jnp.zeros((128, 512), F32).at[32:64].set(rw_a_up[l]).astype(BF16)
        if l > 0:
            vdn = _pad_to(rw_v_dn[l - 1], (512, 128)).astype(BF16)
            vup = _pad_to(rw_v_up[l - 1], (128, 512)).astype(BF16)
        else:
            vdn = vup = None
        par = _rows([rw_r_k[l].reshape(-1), rw_ln_w[l], rw_ln_b[l]], 512)
        y_rw, v_first = _rw(p_all, v_first, vec, mulo, wup, aup, vdn, vup, par, ones64, batch, seq)

        cw = ml_conv_w[l]
        cb = ml_conv_b[l]
        ml_vec = _rows([cb[0:512], cb[512:1024], ml_norm_g[l]], 512)
        gb = _rows([jnp.concatenate([ml_i_b[l], ml_f_b[l]])], 128)
        y_ml = _ml(p_all, _pad_to(cw[:, 0:512], (8, 512)), _pad_to(cw[:, 512:1024], (8, 512)),
                   ml_vec, gb, batch, seq)

        scale = MLA_QK ** -0.5 * LOG2E
        qg = jnp.tile(jnp.concatenate([mla_q_g[l] * scale, jnp.zeros((64,), F32)]), MLA_HEADS)
        mla_vec = _rows([mla_q_norm_g[l], mla_kv_norm_g[l], qg, mla_k_g[l][0:128], mla_k_g[l][128:192]], 1024)
        wq = _pad_to(mla_w_q_b[l].reshape(MLA_Q_LORA, MLA_HEADS, MLA_QK),
                     (MLA_Q_LORA, MLA_HEADS, MLA_PAD)).reshape(MLA_Q_LORA, 1024).astype(BF16)
        wkv = mla_w_kv_b[l].reshape(MLA_KV_LORA, MLA_HEADS, 2, 128).transpose(0, 2, 1, 3)
        wkv = wkv.reshape(MLA_KV_LORA, 1024).astype(BF16)
        mq, mk, mv = _mla_prep(p_all, mla_cos, mla_sin, mla_vec, wq, wkv, batch, seq)
        y_mla = _mla_attn(mq, mk, mv, batch, seq)

        da_vec = _rows([jnp.tile(da_q_g[l], 8) * (DA_QK ** -0.5 * LOG2E), jnp.tile(da_k_g[l], 8)], 512)
        dq, dk, dv = _da_prep(p_all, da_cos, da_sin, da_vec, ones64, batch, seq)
        lam_vec = _rows([da_lq1[l], da_lk1[l], da_lq2[l], da_lk2[l]], 128)
        lam_init = 0.8 - 0.6 * math.exp(-0.3 * l)
        y_da = _da_attn(dq, dk, dv, lam_vec, _rows([da_sub_g[l]], 128), lam_init, batch, seq)

        beta_row = _rows([jnp.repeat(branch_beta[l], GROUP_W)], D_MODEL)
        x2 = _outproj((y_rw, y_ml, y_mla, y_da), p_all, beta_row, x2, w_out[l].astype(BF16))
    return x2.reshape(batch, seq, D_MODEL)
```

```python
import functools
import math

import jax
import jax.numpy as jnp
from jax import lax
from jax.experimental import pallas as pl
from jax.experimental.pallas import tpu as pltpu

F32 = jnp.float32
BF16 = jnp.bfloat16

D_MODEL = 2048
GROUP_W = 512
NORM_EPS = 1e-6
ROPE_THETA = 500000.0

RW_HEAD = 64
RW_GN_EPS = 64e-5
RW_CHUNK = 64
ML_HEADS = 4
ML_HEAD = 128
ML_CHUNK = 256
MLA_HEADS = 4
MLA_NOPE = 128
MLA_ROPE = 64
MLA_QK = 192
MLA_Q_LORA = 384
MLA_KV_LORA = 256
MLA_PAD = 256
DA_HEADS = 4
DA_QK = 64
DA_ROT = 16
ATT_BLOCK = 512
ATT_SUB = 256
MLA_ATT_HEADS = 4
DA_ATT_HEADS = 4
DV_AUG = 144
LOG2E = math.log2(math.e)

N_PROJ = 8192
COL_MLA = 0
COL_RW_LORA = 768
COL_ML_GATE = 896
COL_RW = 1024
COL_ML = 2560
COL_DA = 4608
COL_Z = 6144

INPROJ_TN = 1024

VMEM_LIMIT = 48 * 1024 * 1024


def _dot(a, b):
    return jnp.dot(a.astype(BF16), b.astype(BF16), preferred_element_type=F32)


def _dot_nt(a, b):
    return lax.dot_general(a.astype(BF16), b.astype(BF16), (((1,), (1,)), ((), ())),
                           preferred_element_type=F32)


def _dot_tn(a, b):
    return lax.dot_general(a.astype(BF16), b.astype(BF16), (((0,), (0,)), ((), ())),
                           preferred_element_type=F32)


def _split3(x):
    h1 = x.astype(BF16)
    r1 = x - h1.astype(F32)
    h2 = r1.astype(BF16)
    h3 = (r1 - h2.astype(F32)).astype(BF16)
    return h1, h2, h3


def _dot_exact_lhs(a_bf16, x):
    h1, h2, h3 = _split3(x)
    f = functools.partial(jnp.dot, preferred_element_type=F32)
    return f(a_bf16, h1) + f(a_bf16, h2) + f(a_bf16, h3)


def _seg_sum(x, ones_bf16):
    hi = x.astype(BF16)
    lo = (x - hi.astype(F32)).astype(BF16)
    f = functools.partial(jnp.dot, preferred_element_type=F32)
    return f(hi, ones_bf16) + f(lo, ones_bf16)


def _sigmoid(x):
    return 1.0 / (1.0 + jnp.exp(-x))


def _log_sigmoid(x):
    return jnp.minimum(x, 0.0) - jnp.log(1.0 + jnp.exp(-jnp.abs(x)))


def _params(*sem):
    return pltpu.CompilerParams(dimension_semantics=sem, vmem_limit_bytes=VMEM_LIMIT)


def _inproj_kernel(x_ref, g_ref, w_ref, o_ref, h_ref):
    @pl.when(pl.program_id(1) == 0)
    def _():
        x = x_ref[...]
        ms = jnp.mean(x * x, axis=-1, keepdims=True)
        h_ref[...] = (x * lax.rsqrt(ms + NORM_EPS) * g_ref[...]).astype(BF16)

    o_ref[...] = jnp.dot(h_ref[...], w_ref[...], preferred_element_type=F32)


def _inproj(x2, g, w, tm=1024):
    n = x2.shape[0]
    tn = w.shape[2]
    return pl.pallas_call(
        _inproj_kernel,
        grid=(n // tm, N_PROJ // tn),
        in_specs=[pl.BlockSpec((tm, D_MODEL), lambda i, j: (i, 0)),
                  pl.BlockSpec((1, D_MODEL), lambda i, j: (0, 0)),
                  pl.BlockSpec((None, D_MODEL, tn), lambda i, j: (j, 0, 0))],
        out_specs=pl.BlockSpec((tm, tn), lambda i, j: (i, j)),
        out_shape=jax.ShapeDtypeStruct((n, N_PROJ), F32),
        scratch_shapes=[pltpu.VMEM((tm, D_MODEL), BF16)],
        compiler_params=_params("arbitrary", "arbitrary"),
        name="inproj",
    )(x2, g, w)


def _rw_kernel(has_vres, *refs):
    if has_vres:
        (rr_ref, kr_ref, vr_ref, lo_ref, vf_ref, vec_ref, mulo_ref, wup_ref, aup_ref, vdn_ref, vup_ref,
         par_ref, ones_ref, y_ref, s_ref, cr_ref, ck_ref, cv_ref, cl_ref) = refs
    else:
        (rr_ref, kr_ref, vr_ref, lo_ref, vec_ref, mulo_ref, wup_ref, aup_ref,
         par_ref, ones_ref, y_ref, vo_ref, s_ref, cr_ref, ck_ref, cv_ref, cl_ref) = refs
    L = RW_CHUNK
    batch = rr_ref.shape[0]

    @pl.when(pl.program_id(0) == 0)
    def _():
        for ref in (s_ref, cr_ref, ck_ref, cv_ref, cl_ref):
            ref[...] = jnp.zeros_like(ref)

    def shift(x_ref, carry_ref, mu):
        outs = []
        for bi in range(batch):
            x = x_ref[bi]
            row = lax.broadcasted_iota(jnp.int32, x.shape, 0)
            prev = jnp.where(row == 0, carry_ref[bi, 0:1, :], pltpu.roll(x, 1, 0))
            carry_ref[bi, 0:1, :] = x[L - 1:L, :]
            outs.append(x + (prev - x) * mu)
        return jnp.concatenate(outs, axis=0)

    vec = vec_ref[...]
    ones = ones_ref[...]
    r_all = shift(rr_ref, cr_ref, vec[0:1])
    k_all = shift(kr_ref, ck_ref, vec[1:2])
    v_all = shift(vr_ref, cv_ref, vec[2:3])
    lo = shift(lo_ref, cl_ref, mulo_ref[0:1])

    lw_all = -math.exp(-0.5) * _sigmoid(vec[3:4] + _dot(jnp.tanh(lo), wup_ref[...]))
    a_all = _sigmoid(vec[4:5] + _dot(lo, aup_ref[...]))
    if has_vres:
        gate = _sigmoid(vec[7:8] + _dot(_dot(v_all, vdn_ref[...]), vup_ref[...]))
        vf = jnp.concatenate([vf_ref[bi] for bi in range(batch)], axis=0)
        v_all = v_all + (vf - v_all) * gate
    else:
        for bi in range(batch):
            vo_ref[bi] = v_all[L * bi:L * (bi + 1)]
    kk_all = k_all * vec[5:6]
    kk_all = kk_all / jnp.maximum(jnp.sqrt(_seg_sum(kk_all * kk_all, ones)), 1e-12)
    k_all = k_all * (1.0 + (a_all - 1.0) * vec[6:7])

    row = lax.broadcasted_iota(jnp.int32, (L, 128), 0)
    lane = lax.broadcasted_iota(jnp.int32, (L, 128), 1)
    col = lane & (RW_HEAD - 1)
    strict = row > col
    incl = row >= col
    eye = (row == col).astype(F32)
    first = lane < RW_HEAD
    r128 = lax.broadcasted_iota(jnp.int32, (128, 128), 0)
    c128 = lax.broadcasted_iota(jnp.int32, (128, 128), 1)
    same_head = (r128 < RW_HEAD) == (c128 < RW_HEAD)
    tri = (lax.broadcasted_iota(jnp.int32, (L, L), 0) >= lax.broadcasted_iota(jnp.int32, (L, L), 1)).astype(BF16)

    def bd(x):
        return jnp.concatenate([jnp.where(first, x, 0.0), jnp.where(first, 0.0, x)], axis=0)

    n_pairs = GROUP_W // 128
    units = [(bi, pr) for bi in range(batch) for pr in range(n_pairs)]
    idx = range(len(units))
    al, rt, bet, kt, beh, kh, p_chunk, vs = [], [], [], [], [], [], [], []
    for bi in range(batch):
        rows = slice(L * bi, L * (bi + 1))
        lw = lw_all[rows]
        r = r_all[rows]
        k = k_all[rows]
        kk = kk_all[rows]
        b = _dot_exact_lhs(tri, lw)
        b_last = b[L - 1:L, :]
        p_inv = jnp.exp(-b)
        p_end = jnp.exp(b_last - b)
        be = kk * a_all[rows]
        al.append(-(kk * jnp.exp(b - lw)))
        rt.append(r * jnp.exp(b))
        bet.append(be * p_inv)
        kt.append(k * p_inv)
        beh.append(be * p_end)
        kh.append(k * p_end)
        p_chunk.append(jnp.exp(b_last))
        vs.append(v_all[rows])

    def part(xs, i):
        bi, pr = units[i]
        return xs[bi][:, 128 * pr:128 * pr + 128]

    lhs = [jnp.concatenate([part(al, i), part(rt, i)], axis=0).astype(BF16) for i in idx]
    rhs = [jnp.concatenate([bd(part(bet, i)), bd(part(kt, i))], axis=0).astype(BF16) for i in idx]
    g = [_dot_nt(lhs[i], rhs[i]) for i in idx]
    a_ab = [jnp.where(strict, g[i][0:L, 0:128], 0.0) for i in idx]
    a_ak = [jnp.where(strict, g[i][0:L, 128:256], 0.0) for i in idx]
    b_rb = [jnp.where(incl, g[i][L:2 * L, 0:128], 0.0) for i in idx]
    b_rk = [jnp.where(incl, g[i][L:2 * L, 128:256], 0.0) for i in idx]
    s = [s_ref[i] for i in idx]
    sa = [_dot_nt(lhs[i], s[i]) for i in idx]
    vv = [_dot(jnp.concatenate([a_ak[i], b_rk[i]], axis=0), bd(part(vs, i))) for i in idx]

    t_inv = [eye + a_ab[i] for i in idx]
    x = [_dot(a_ab[i], bd(a_ab[i])) for i in idx]
    for _ in range(4):
        xt = [_dot(jnp.concatenate([x[i], t_inv[i]], axis=0), bd(x[i])) for i in idx]
        x = [xt[i][0:L] for i in idx]
        t_inv = [t_inv[i] + xt[i][L:2 * L] for i in idx]
    t_inv = [t_inv[i] + _dot(t_inv[i], bd(x[i])) for i in idx]

    u = [_dot(t_inv[i], bd(sa[i][0:L] + vv[i][0:L])) for i in idx]
    ys = [sa[i][L:2 * L] + vv[i][L:2 * L] + _dot(b_rb[i], bd(u[i])) for i in idx]
    for i in idx:
        upd = _dot_tn(jnp.concatenate([u[i], part(vs, i)], axis=0),
                      jnp.concatenate([part(beh, i), part(kh, i)], axis=0))
        s_ref[i] = s[i] * part(p_chunk, i) + jnp.where(same_head, upd, 0.0)

    par = par_ref[...]
    inv_n = 1.0 / RW_HEAD
    rows_all = batch * L
    y_all = jnp.concatenate([jnp.concatenate(ys[n_pairs * bi:n_pairs * (bi + 1)], axis=1)
                             for bi in range(batch)], axis=0)
    sums = _dot(jnp.concatenate([y_all, r_all * k_all * par[0:1]], axis=0), ones)
    yc = y_all - sums[0:rows_all] * inv_n
    var = _dot(yc * yc, ones) * inv_n
    out = yc * lax.rsqrt(var + RW_GN_EPS) * par[1:2] + par[2:3] + sums[rows_all:2 * rows_all] * v_all
    for bi in range(batch):
        y_ref[bi] = out[L * bi:L * (bi + 1)].astype(BF16)


def _rw(p_all, v_first, vec, mulo, wup, aup, vdn, vup, par, ones64, batch, seq):
    nc = seq // RW_CHUNK
    has_vres = v_first is not None
    p3 = p_all.reshape(batch, seq, N_PROJ)

    def cols(width, c):
        return pl.BlockSpec((batch, RW_CHUNK, width), lambda i: (0, i, c))

    const = lambda i: (0, 0)
    blk = cols(512, 0)
    in_specs = [cols(512, COL_RW // 512), cols(512, COL_RW // 512 + 1), cols(512, COL_RW // 512 + 2),
                cols(128, COL_RW_LORA // 128)]
    args = [p3, p3, p3, p3]
    if has_vres:
        in_specs.append(blk)
        args.append(v_first.reshape(batch, seq, 512))
    in_specs += [pl.BlockSpec((8, 512), const), pl.BlockSpec((8, 128), const),
                 pl.BlockSpec((128, 512), const), pl.BlockSpec((128, 512), const)]
    args += [vec, mulo, wup, aup]
    if has_vres:
        in_specs += [pl.BlockSpec((512, 128), const), pl.BlockSpec((128, 512), const)]
        args += [vdn, vup]
    in_specs += [pl.BlockSpec((8, 512), const), pl.BlockSpec((512, 512), const)]
    args += [par, ones64]
    n_out = 1 if has_vres else 2
    res = pl.pallas_call(
        functools.partial(_rw_kernel, has_vres),
        grid=(nc,),
        in_specs=in_specs,
        out_specs=[blk] * n_out,
        out_shape=[jax.ShapeDtypeStruct((batch, seq, 512), dt) for dt in (BF16, F32)[:n_out]],
        scratch_shapes=[pltpu.VMEM((batch * (GROUP_W // 128), 128, 128), F32)]
        + [pltpu.VMEM((batch, 8, 512), F32)] * 3 + [pltpu.VMEM((batch, 8, 128), F32)],
        compiler_params=_params("arbitrary"),
        name="rwkv7",
    )(*args)
    y = res[0].reshape(batch * seq, 512)
    return y, (v_first if has_vres else res[1].reshape(batch * seq, 512))


def _ml_kernel(q_ref, k_ref, v_ref, o_ref, g_ref, cwq_ref, cwk_ref, vec_ref, gb_ref, y_ref,
               hq_ref, hk_ref, c_ref, n_ref, m_ref):
    L = ML_CHUNK

    @pl.when(pl.program_id(1) == 0)
    def _():
        hq_ref[...] = jnp.zeros_like(hq_ref)
        hk_ref[...] = jnp.zeros_like(hk_ref)
        c_ref[...] = jnp.zeros_like(c_ref)
        n_ref[...] = jnp.zeros_like(n_ref)
        m_ref[...] = jnp.zeros_like(m_ref)

    vec = vec_ref[...]

    def conv(x_ref, buf_ref, w_ref, bias):
        x = x_ref[...]
        w = w_ref[...]
        buf_ref[8:8 + L, :] = x
        acc = x * w[3:4] + bias
        for s in (1, 2, 3):
            acc = acc + buf_ref[8 - s:8 - s + L, :] * w[3 - s:4 - s]
        buf_ref[0:8, :] = x[L - 8:L]
        return acc * _sigmoid(acc)

    q = conv(q_ref, hq_ref, cwq_ref, vec[0:1])
    k = conv(k_ref, hk_ref, cwk_ref, vec[1:2]) * (ML_HEAD ** -0.5)
    v = v_ref[...]
    og = _sigmoid(o_ref[...])

    g = g_ref[...] + gb_ref[0:1]
    lane = lax.broadcasted_iota(jnp.int32, (L, 128), 1)
    tri_b = lax.broadcasted_iota(jnp.int32, (L, L), 0) >= lax.broadcasted_iota(jnp.int32, (L, L), 1)
    cum = _dot_exact_lhs(tri_b.astype(BF16), _log_sigmoid(g))
    comb_t = jnp.where(lane < ML_HEADS, g, cum).T

    H = range(ML_HEADS)
    sls = [slice(128 * h, 128 * h + 128) for h in H]
    qs = [q[:, sl] for sl in sls]
    ks = [k[:, sl] for sl in sls]
    vs = [v[:, sl] for sl in sls]
    qk = [_dot_nt(qs[h], ks[h]) for h in H]
    c_old = [c_ref[h] for h in H]
    n_old = [n_ref[h:h + 1, :] for h in H]
    m_prev = [m_ref[h:h + 1, 0:1] for h in H]
    qc = [_dot_nt(qs[h], c_old[h]) for h in H]
    b_col = [cum[:, ML_HEADS + h:ML_HEADS + h + 1] for h in H]
    ig_col = [g[:, h:h + 1] for h in H]
    dm = [jnp.where(tri_b, b_col[h] - comb_t[ML_HEADS + h:ML_HEADS + h + 1, :] + comb_t[h:h + 1, :], -jnp.inf)
          for h in H]
    inter = [b_col[h] + m_prev[h] for h in H]
    mj = [jnp.maximum(inter[h], jnp.max(dm[h], axis=-1, keepdims=True)) for h in H]
    s = [qk[h] * jnp.exp(dm[h] - mj[h]) for h in H]
    iw = [jnp.exp(inter[h] - mj[h]) for h in H]
    num = [_dot(s[h], vs[h]) + iw[h] * qc[h] for h in H]
    den = [jnp.sum(s[h], axis=-1, keepdims=True) + iw[h] * jnp.sum(qs[h] * n_old[h], axis=-1, keepdims=True)
           for h in H]
    hh = [num[h] / jnp.maximum(jnp.abs(den[h]), jnp.exp(-mj[h])) for h in H]

    b_last = [b_col[h][L - 1:L, :] for h in H]
    g_col = [b_last[h] - b_col[h] + ig_col[h] for h in H]
    m_new = [jnp.maximum(b_last[h] + m_prev[h], jnp.max(g_col[h], axis=0, keepdims=True)) for h in H]
    a_old = [jnp.exp(b_last[h] + m_prev[h] - m_new[h]) for h in H]
    wg = [jnp.exp(g_col[h] - m_new[h]) for h in H]
    kv = [_dot_tn(vs[h] * wg[h], ks[h]) for h in H]
    outs = []
    for h in H:
        c_ref[h] = a_old[h] * c_old[h] + kv[h]
        n_ref[h:h + 1, :] = a_old[h] * n_old[h] + jnp.sum(ks[h] * wg[h], axis=0, keepdims=True)
        m_ref[h:h + 1, :] = jnp.broadcast_to(m_new[h], (1, 128))
        hn = hh[h] * lax.rsqrt(jnp.mean(hh[h] * hh[h], axis=-1, keepdims=True) + NORM_EPS) * vec[2:3, sls[h]]
        outs.append(og[:, sls[h]] * hn)
    y_ref[...] = jnp.concatenate(outs, axis=1).astype(BF16)


def _ml(p_all, cwq, cwk, vec, gb, batch, seq):
    n = batch * seq
    nc = seq // ML_CHUNK

    def rows(c):
        return lambda b, i: (b * nc + i, c)

    const = lambda b, i: (0, 0)
    base = COL_ML // 512
    return pl.pallas_call(
        _ml_kernel,
        grid=(batch, nc),
        in_specs=[pl.BlockSpec((ML_CHUNK, 512), rows(base)),
                  pl.BlockSpec((ML_CHUNK, 512), rows(base + 1)),
                  pl.BlockSpec((ML_CHUNK, 512), rows(base + 2)),
                  pl.BlockSpec((ML_CHUNK, 512), rows(base + 3)),
                  pl.BlockSpec((ML_CHUNK, 128), rows(COL_ML_GATE // 128)),
                  pl.BlockSpec((8, 512), const), pl.BlockSpec((8, 512), const),
                  pl.BlockSpec((8, 512), const), pl.BlockSpec((8, 128), const)],
        out_specs=pl.BlockSpec((ML_CHUNK, 512), rows(0)),
        out_shape=jax.ShapeDtypeStruct((n, 512), BF16),
        scratch_shapes=[pltpu.VMEM((ML_CHUNK + 8, 512), F32), pltpu.VMEM((ML_CHUNK + 8, 512), F32),
                        pltpu.VMEM((ML_HEADS, 128, 128), F32), pltpu.VMEM((8, 128), F32),
                        pltpu.VMEM((8, 128), F32)],
        compiler_params=_params("arbitrary", "arbitrary"),
        name="mlstm",
    )(p_all, p_all, p_all, p_all, p_all, cwq, cwk, vec, gb)


def _rope128(x, cos_t, sin_t, swap):
    return x * cos_t + jnp.dot(x.astype(BF16), swap, preferred_element_type=F32) * sin_t


def _swap_matrix(half, seg):
    lane = jnp.arange(128)
    pos = lane % seg
    src = jnp.where(pos < half, lane + half, jnp.where(pos < 2 * half, lane - half, lane))
    return (lane[:, None] == src[None, :]).astype(BF16)


def _mla_prep_kernel(p_ref, cos_ref, sin_ref, vec_ref, wq_ref, wkv_ref, swap_ref, q_ref, k_ref, vt_ref):
    p = p_ref[...]
    vec = vec_ref[...]
    cos_t = cos_ref[...]
    sin_t = sin_ref[...]
    swap = swap_ref[...]
    q_lat = p[:, 0:MLA_Q_LORA]
    kv_lat = p[:, MLA_Q_LORA:MLA_Q_LORA + MLA_KV_LORA]
    k_pe = p[:, 640:768]
    qn = q_lat * lax.rsqrt(jnp.mean(q_lat * q_lat, axis=-1, keepdims=True) + NORM_EPS) * vec[0:1, 0:MLA_Q_LORA]
    kvn = kv_lat * lax.rsqrt(jnp.mean(kv_lat * kv_lat, axis=-1, keepdims=True) + NORM_EPS) * vec[1:2, 0:MLA_KV_LORA]
    q4 = _dot(qn, wq_ref[...])
    kv4 = _dot(kvn, wkv_ref[...])
    pe_ss = jnp.sum(k_pe * k_pe, axis=-1, keepdims=True)
    pe_rot = _rope128(k_pe * vec[4:5, 0:128], cos_t, sin_t, swap)
    qs, ks = [], []
    for h in range(MLA_HEADS):
        qh = q4[:, MLA_PAD * h:MLA_PAD * (h + 1)]
        inv = lax.rsqrt(jnp.sum(qh * qh, axis=-1, keepdims=True) * (1.0 / MLA_QK) + NORM_EPS)
        qh = qh * inv * vec[2:3, MLA_PAD * h:MLA_PAD * (h + 1)]
        qs += [qh[:, 0:128], _rope128(qh[:, 128:256], cos_t, sin_t, swap)]
        kn = kv4[:, 128 * h:128 * (h + 1)]
        inv = lax.rsqrt((jnp.sum(kn * kn, axis=-1, keepdims=True) + pe_ss) * (1.0 / MLA_QK) + NORM_EPS)
        ks += [kn * inv * vec[3:4, 0:128], pe_rot * inv]
    q_ref[...] = jnp.concatenate(qs, axis=1).astype(BF16)
    k_ref[...] = jnp.concatenate(ks, axis=1).astype(BF16)
    _store_vt(vt_ref, kv4[:, 512:1024])


def _store_vt(vt_ref, v):
    heads, _, rows = vt_ref.shape
    eye = (lax.broadcasted_iota(jnp.int32, (128, 128), 0)
           == lax.broadcasted_iota(jnp.int32, (128, 128), 1)).astype(BF16)
    ones = jnp.ones((DV_AUG - 128, rows), BF16)
    for h in range(heads):
        vt = _dot_nt(eye, v[:, 128 * h:128 * (h + 1)]).astype(BF16)
        vt_ref[h] = jnp.concatenate([vt, ones], axis=0)


def _vt_spec(heads, nt):
    return pl.BlockSpec((None, heads, None, DV_AUG, ATT_BLOCK), lambda i: (i // nt, 0, i % nt, 0, 0))


def _mla_prep(p_all, cos_t, sin_t, vec, wq, wkv, batch, seq):
    n = batch * seq
    tr = ATT_BLOCK
    nt = seq // tr
    const = lambda i: (0, 0)
    return pl.pallas_call(
        _mla_prep_kernel,
        grid=(n // tr,),
        in_specs=[pl.BlockSpec((tr, 768), lambda i: (i, COL_MLA // 768)),
                  pl.BlockSpec((tr, 128), lambda i: (i % nt, 0)),
                  pl.BlockSpec((tr, 128), lambda i: (i % nt, 0)),
                  pl.BlockSpec((8, 1024), const),
                  pl.BlockSpec((MLA_Q_LORA, 1024), const),
                  pl.BlockSpec((MLA_KV_LORA, 1024), const),
                  pl.BlockSpec((128, 128), const)],
        out_specs=[pl.BlockSpec((tr, 1024), lambda i: (i, 0)),
                   pl.BlockSpec((tr, 1024), lambda i: (i, 0)),
                   _vt_spec(MLA_HEADS, nt)],
        out_shape=[jax.ShapeDtypeStruct((n, 1024), BF16), jax.ShapeDtypeStruct((n, 1024), BF16),
                   jax.ShapeDtypeStruct((batch, MLA_HEADS, nt, DV_AUG, tr), BF16)],
        compiler_params=_params("arbitrary"),
        name="mla_prep",
    )(p_all, cos_t, sin_t, vec, wq, wkv, _swap_matrix(MLA_ROPE // 2, 128))


def _attn_block_t(kbs, vts, units, masked):
    sub = ATT_SUB
    s, cols = [], []
    for q, _, _, c0, h in units:
        keys = c0 + sub if masked else kbs[h].shape[0]
        st = _dot_nt(kbs[h][0:keys], q)
        if masked:
            keep = (lax.broadcasted_iota(jnp.int32, st.shape, 0)
                    <= lax.broadcasted_iota(jnp.int32, st.shape, 1) + c0)
            st = jnp.where(keep, st, -jnp.inf)
        s.append(st)
        cols.append(slice(c0, c0 + sub))
    idx = range(len(units))
    m_old = [units[i][1][:, cols[i]] for i in idx]
    m_new = [jnp.maximum(m_old[i], jnp.max(s[i], axis=0, keepdims=True)) for i in idx]
    p = [jnp.exp2(s[i] - m_new[i]).astype(BF16) for i in idx]
    alpha = [jnp.exp2(m_old[i] - m_new[i]) for i in idx]
    for i in idx:
        _, m_ref, acc_ref, _, h = units[i]
        pv = jnp.dot(vts[h][:, 0:s[i].shape[0]], p[i], preferred_element_type=F32)
        acc_ref[:, cols[i]] = alpha[i] * acc_ref[:, cols[i]] + pv
        m_ref[:, cols[i]] = m_new[i]


def _init_stats(m_ref, acc_ref):
    m_ref[...] = jnp.full_like(m_ref, -jnp.inf)
    acc_ref[...] = jnp.zeros_like(acc_ref)


def _attn_sweep(i, k_ref, vt_ref, dk, units):
    tb = ATT_BLOCK
    heads = vt_ref.shape[0]

    def block(j, masked):
        start = pl.multiple_of(j * tb, tb)
        kbs = [k_ref[pl.ds(start, tb), dk * h:dk * (h + 1)] for h in range(heads)]
        _attn_block_t(kbs, [vt_ref[h, j] for h in range(heads)], units, masked)

    def body(j, carry):
        block(j, False)
        return carry

    lax.fori_loop(0, i, body, 0)
    block(i, True)


def _attn_result(acc_ref):
    acc = acc_ref[...]
    return acc[0:128] / acc[128:129]


def _attn_specs(dk, nq, seq, heads):
    tb = ATT_BLOCK
    return [pl.BlockSpec((tb, dk * heads), lambda b, h, i: (b * nq + i, h)),
            pl.BlockSpec((seq, dk * heads), lambda b, h, i: (b, h)),
            pl.BlockSpec((None, heads, nq, DV_AUG, tb), lambda b, h, i: (b, h, 0, 0, 0))]


def _mla_attn_kernel(q_ref, k_ref, vt_ref, o_ref, *stats):
    heads = vt_ref.shape[0]
    units = []
    for h in range(heads):
        m_ref, acc_ref = stats[2 * h:2 * h + 2]
        _init_stats(m_ref, acc_ref)
        units += [(q_ref[c0:c0 + ATT_SUB, MLA_PAD * h:MLA_PAD * (h + 1)], m_ref, acc_ref, c0, h)
                  for c0 in range(0, ATT_BLOCK, ATT_SUB)]
    _attn_sweep(pl.program_id(2), k_ref, vt_ref, MLA_PAD, units)
    o_ref[...] = jnp.concatenate([_attn_result(stats[2 * h + 1]).T for h in range(heads)], axis=1).astype(BF16)


def _mla_attn(q, k, vt, batch, seq):
    n = batch * seq
    tb = ATT_BLOCK
    nq = seq // tb
    return pl.pallas_call(
        _mla_attn_kernel,
        grid=(batch, MLA_HEADS // MLA_ATT_HEADS, nq),
        in_specs=_attn_specs(MLA_PAD, nq, seq, MLA_ATT_HEADS),
        out_specs=pl.BlockSpec((tb, 128 * MLA_ATT_HEADS), lambda b, h, i: (b * nq + i, h)),
        out_shape=jax.ShapeDtypeStruct((n, 512), BF16),
        scratch_shapes=[pltpu.VMEM((1, tb), F32), pltpu.VMEM((DV_AUG, tb), F32)] * MLA_ATT_HEADS,
        compiler_params=_params("arbitrary", "arbitrary", "arbitrary"),
        name="mla_attn",
    )(q, k, vt)


def _da_prep_kernel(q_ref, k_ref, v_ref, cos_ref, sin_ref, vec_ref, ones_ref, swap_ref, qo_ref, ko_ref, vo_ref):
    vec = vec_ref[...]
    ones = ones_ref[...]
    cos_t = cos_ref[...]
    sin_t = sin_ref[...]
    swap = swap_ref[...]

    def norm_rope(x, gain):
        ss = _dot(x * x, ones) * (1.0 / DA_QK)
        x = x * lax.rsqrt(ss + NORM_EPS) * gain
        parts = [_rope128(x[:, 128 * c:128 * (c + 1)], cos_t, sin_t, swap) for c in range(4)]
        return jnp.concatenate(parts, axis=1)

    qo_ref[...] = norm_rope(q_ref[...], vec[0:1]).astype(BF16)
    ko_ref[...] = norm_rope(k_ref[...], vec[1:2]).astype(BF16)
    _store_vt(vo_ref, v_ref[...])


def _da_prep(p_all, cos_t, sin_t, vec, ones64, batch, seq):
    n = batch * seq
    tr = ATT_BLOCK
    nt = seq // tr
    const = lambda i: (0, 0)
    base = COL_DA // 512
    out = jax.ShapeDtypeStruct((n, 512), BF16)
    return pl.pallas_call(
        _da_prep_kernel,
        grid=(n // tr,),
        in_specs=[pl.BlockSpec((tr, 512), lambda i: (i, base)),
                  pl.BlockSpec((tr, 512), lambda i: (i, base + 1)),
                  pl.BlockSpec((tr, 512), lambda i: (i, base + 2)),
                  pl.BlockSpec((tr, 128), lambda i: (i % nt, 0)),
                  pl.BlockSpec((tr, 128), lambda i: (i % nt, 0)),
                  pl.BlockSpec((8, 512), const),
                  pl.BlockSpec((512, 512), const),
                  pl.BlockSpec((128, 128), const)],
        out_specs=[pl.BlockSpec((tr, 512), lambda i: (i, 0))] * 2 + [_vt_spec(DA_HEADS, nt)],
        out_shape=[out, out, jax.ShapeDtypeStruct((batch, DA_HEADS, nt, DV_AUG, tr), BF16)],
        compiler_params=_params("arbitrary"),
        name="da_prep",
    )(p_all, p_all, p_all, cos_t, sin_t, vec, ones64, _swap_matrix(DA_ROT // 2, DA_QK))


def _da_attn_kernel(lam_init, q_ref, k_ref, vt_ref, lam_ref, g_ref, o_ref, *stats):
    heads = vt_ref.shape[0]
    lane = lax.broadcasted_iota(jnp.int32, (ATT_BLOCK, 128), 1)
    units = []
    for h in range(heads):
        q = q_ref[:, 128 * h:128 * (h + 1)]
        for mp in range(2):
            qm = jnp.where((lane < DA_QK) == (mp == 0), q, jnp.zeros_like(q))
            m_ref, acc_ref = stats[4 * h + 2 * mp:4 * h + 2 * mp + 2]
            _init_stats(m_ref, acc_ref)
            units += [(qm[c0:c0 + ATT_SUB], m_ref, acc_ref, c0, h) for c0 in range(0, ATT_BLOCK, ATT_SUB)]
    _attn_sweep(pl.program_id(2), k_ref, vt_ref, 128, units)

    lv = lam_ref[...]
    lam = (jnp.exp(jnp.sum(lv[0:1] * lv[1:2], axis=-1, keepdims=True))
           - jnp.exp(jnp.sum(lv[2:3] * lv[3:4], axis=-1, keepdims=True)) + lam_init)
    outs = []
    for h in range(heads):
        o = (_attn_result(stats[4 * h + 1]) - lam * _attn_result(stats[4 * h + 3])).T
        o = o * lax.rsqrt(jnp.mean(o * o, axis=-1, keepdims=True) + NORM_EPS) * g_ref[0:1]
        outs.append(o * (1.0 - lam_init))
    o_ref[...] = jnp.concatenate(outs, axis=1).astype(BF16)


def _da_attn(q, k, vt, lam_vec, sub_g, lam_init, batch, seq):
    n = batch * seq
    tb = ATT_BLOCK
    nq = seq // tb
    const = lambda b, h, i: (0, 0)
    row = pltpu.VMEM((1, tb), F32)
    acc = pltpu.VMEM((DV_AUG, tb), F32)
    return pl.pallas_call(
        functools.partial(_da_attn_kernel, lam_init),
        grid=(batch, DA_HEADS // DA_ATT_HEADS, nq),
        in_specs=_attn_specs(128, nq, seq, DA_ATT_HEADS)
        + [pl.BlockSpec((8, 128), const), pl.BlockSpec((8, 128), const)],
        out_specs=pl.BlockSpec((tb, 128 * DA_ATT_HEADS), lambda b, h, i: (b * nq + i, h)),
        out_shape=jax.ShapeDtypeStruct((n, 512), BF16),
        scratch_shapes=[row, acc] * (2 * DA_ATT_HEADS),
        compiler_params=_params("arbitrary", "arbitrary", "arbitrary"),
        name="da_attn",
    )(q, k, vt, lam_vec, sub_g)


def _outproj_kernel(ya_ref, yb_ref, yc_ref, yd_ref, z_ref, beta_ref, x_ref, w_ref, o_ref):
    y_refs = (ya_ref, yb_ref, yc_ref, yd_ref)
    acc = x_ref[...]
    for c in range(4):
        sl = slice(GROUP_W * c, GROUP_W * (c + 1))
        z = z_ref[:, sl]
        gated = (y_refs[c][...].astype(F32) * beta_ref[0:1, sl] * (z * _sigmoid(z))).astype(BF16)
        acc = acc + jnp.dot(gated, w_ref[sl, :], preferred_element_type=F32)
    o_ref[...] = acc


def _outproj(ys, p_all, beta_row, x2, w, tm=512):
    n = x2.shape[0]
    yspec = pl.BlockSpec((tm, 512), lambda i: (i, 0))
    resident = dict(pipeline_mode=pl.Buffered(1))
    return pl.pallas_call(
        _outproj_kernel,
        grid=(n // tm,),
        in_specs=[yspec] * 4 + [pl.BlockSpec((tm, D_MODEL), lambda i: (i, COL_Z // D_MODEL))]
        + [pl.BlockSpec((8, D_MODEL), lambda i: (0, 0), **resident),
           pl.BlockSpec((tm, D_MODEL), lambda i: (i, 0)),
           pl.BlockSpec((D_MODEL, D_MODEL), lambda i: (0, 0), **resident)],
        out_specs=pl.BlockSpec((tm, D_MODEL), lambda i: (i, 0)),
        out_shape=jax.ShapeDtypeStruct((n, D_MODEL), F32),
        compiler_params=_params("arbitrary"),
        name="outproj",
    )(*ys, p_all, beta_row, x2, w)


def _rows(rows, width, n_rows=8):
    padded = [jnp.pad(r.astype(F32), (0, width - r.shape[0])) for r in rows]
    padded += [jnp.zeros((width,), F32)] * (n_rows - len(rows))
    return jnp.stack(padded)


def _pad_to(a, shape):
    return jnp.pad(a, [(0, s - d) for s, d in zip(shape, a.shape)])


def _arrange_w_in(w):
    w = w.astype(BF16)
    zeros = lambda c: jnp.zeros((D_MODEL, c), BF16)
    return jnp.concatenate([
        w[:, 5704:6408], zeros(64),
        w[:, 3584:3648], zeros(64),
        w[:, 5696:5704], zeros(120),
        w[:, 2048:3584],
        w[:, 3648:5696],
        w[:, 6408:7944],
        w[:, 0:2048],
    ], axis=1).reshape(D_MODEL, N_PROJ // INPROJ_TN, INPROJ_TN).transpose(1, 0, 2)


def _rope_tables(seq, rot, seg, fill):
    half = rot // 2
    inv = ROPE_THETA ** (-jnp.arange(0, rot, 2, dtype=F32) / rot)
    ang = jnp.arange(seq, dtype=F32)[:, None] * inv[None, :]
    cos, sin = jnp.cos(ang), jnp.sin(ang)
    rest = seg - rot
    cos_s = jnp.concatenate([cos, cos, jnp.full((seq, rest), fill, F32)], axis=1)
    sin_s = jnp.concatenate([-sin, sin, jnp.zeros((seq, rest), F32)], axis=1)
    reps = 128 // seg
    return jnp.tile(cos_s, (1, reps)), jnp.tile(sin_s, (1, reps))


def kernel(x, norm_g, w_in, w_out, branch_beta, rw_mu, rw_w0, rw_w_up, rw_a0, rw_a_up, rw_v0, rw_v_dn, rw_v_up, rw_k_k, rw_k_a, rw_r_k, rw_ln_w, rw_ln_b, ml_conv_w, ml_conv_b, ml_i_b, ml_f_b, ml_norm_g, mla_q_norm_g, mla_kv_norm_g, mla_w_q_b, mla_w_kv_b, mla_q_g, mla_k_g, da_q_g, da_k_g, da_lq1, da_lk1, da_lq2, da_lk2, da_sub_g):
    batch, seq, _ = x.shape
    depth = w_in.shape[0]
    n = batch * seq
    x2 = x.reshape(n, D_MODEL)

    seg_id = jnp.arange(GROUP_W) // RW_HEAD
    ones64 = (seg_id[:, None] == seg_id[None, :]).astype(BF16)
    mla_cos, mla_sin = _rope_tables(seq, MLA_ROPE, 128, 0.0)
    da_cos, da_sin = _rope_tables(seq, DA_ROT, DA_QK, 1.0)

    v_first = None
    for l in range(depth):
        p_all = _inproj(x2, norm_g[l][None, :], _arrange_w_in(w_in[l]))

        mu = rw_mu[l]
        vec = _rows([mu[0:512], mu[512:1024], mu[1024:1536], rw_w0[l], rw_a0[l], rw_k_k[l], rw_k_a[l]]
                    + ([rw_v0[l - 1]] if l > 0 else []), 512)
        mulo = _rows([mu[1536:1600]], 128)
        wup = _pad_to(rw_w_up[l], (128, 512)).astype(BF16)
        aup = jnp.zeros((128, 512), F32).at[32:64].set(rw_a_up[l]).astype(BF16)
        if l > 0:
            vdn = _pad_to(rw_v_dn[l - 1], (512, 128)).astype(BF16)
            vup = _pad_to(rw_v_up[l - 1], (128, 512)).astype(BF16)
        else:
            vdn = vup = None
        par = _rows([rw_r_k[l].reshape(-1), rw_ln_w[l], rw_ln_b[l]], 512)
        y_rw, v_first = _rw(p_all, v_first, vec, mulo, wup, aup, vdn, vup, par, ones64, batch, seq)

        cw = ml_conv_w[l]
        cb = ml_conv_b[l]
        ml_vec = _rows([cb[0:512], cb[512:1024], ml_norm_g[l]], 512)
        gb = _rows([jnp.concatenate([ml_i_b[l], ml_f_b[l]])], 128)
        y_ml = _ml(p_all, _pad_to(cw[:, 0:512], (8, 512)), _pad_to(cw[:, 512:1024], (8, 512)),
                   ml_vec, gb, batch, seq)

        scale = MLA_QK ** -0.5 * LOG2E
        qg = jnp.tile(jnp.concatenate([mla_q_g[l] * scale, jnp.zeros((64,), F32)]), MLA_HEADS)
        mla_vec = _rows([mla_q_norm_g[l], mla_kv_norm_g[l], qg, mla_k_g[l][0:128], mla_k_g[l][128:192]], 1024)
        wq = _pad_to(mla_w_q_b[l].reshape(MLA_Q_LORA, MLA_HEADS, MLA_QK),
                     (MLA_Q_LORA, MLA_HEADS, MLA_PAD)).reshape(MLA_Q_LORA, 1024).astype(BF16)
        wkv = mla_w_kv_b[l].reshape(MLA_KV_LORA, MLA_HEADS, 2, 128).transpose(0, 2, 1, 3)
        wkv = wkv.reshape(MLA_KV_LORA, 1024).astype(BF16)
        mq, mk, mv = _mla_prep(p_all, mla_cos, mla_sin, mla_vec, wq, wkv, batch, seq)
        y_mla = _mla_attn(mq, mk, mv, batch, seq)

        da_vec = _rows([jnp.tile(da_q_g[l], 8) * (DA_QK ** -0.5 * LOG2E), jnp.tile(da_k_g[l], 8)], 512)
        dq, dk, dv = _da_prep(p_all, da_cos, da_sin, da_vec, ones64, batch, seq)
        lam_vec = _rows([da_lq1[l], da_lk1[l], da_lq2[l], da_lk2[l]], 128)
        lam_init = 0.8 - 0.6 * math.exp(-0.3 * l)
        y_da = _da_attn(dq, dk, dv, lam_vec, _rows([da_sub_g[l]], 128), lam_init, batch, seq)

        beta_row = _rows([jnp.repeat(branch_beta[l], GROUP_W)], D_MODEL)
        x2 = _outproj((y_rw, y_ml, y_mla, y_da), p_all, beta_row, x2, w_out[l].astype(BF16))
    return x2.reshape(batch, seq, D_MODEL)
```

```python
import functools
import math

import jax
import jax.numpy as jnp
from jax import lax
from jax.experimental import pallas as pl
from jax.experimental.pallas import tpu as pltpu

F32 = jnp.float32
BF16 = jnp.bfloat16

D_MODEL = 2048
GROUP_W = 512
NORM_EPS = 1e-6
ROPE_THETA = 500000.0

RW_HEAD = 64
RW_GN_EPS = 64e-5
RW_CHUNK = 64
ML_HEADS = 4
ML_HEAD = 128
ML_CHUNK = 256
MLA_HEADS = 4
MLA_NOPE = 128
MLA_ROPE = 64
MLA_QK = 192
MLA_Q_LORA = 384
MLA_KV_LORA = 256
MLA_PAD = 256
DA_HEADS = 4
DA_QK = 64
DA_ROT = 16
ATT_BLOCK = 512
ATT_SUB = 256
MLA_ATT_HEADS = 4
DA_ATT_HEADS = 4
DV_AUG = 144
LOG2E = math.log2(math.e)

N_PROJ = 8192
COL_MLA = 0
COL_RW_LORA = 768
COL_ML_GATE = 896
COL_RW = 1024
COL_ML = 2560
COL_DA = 4608
COL_Z = 6144

INPROJ_TN = 1024

VMEM_LIMIT = 48 * 1024 * 1024


def _dot(a, b):
    return jnp.dot(a.astype(BF16), b.astype(BF16), preferred_element_type=F32)


def _dot_nt(a, b):
    return lax.dot_general(a.astype(BF16), b.astype(BF16), (((1,), (1,)), ((), ())),
                           preferred_element_type=F32)


def _dot_tn(a, b):
    return lax.dot_general(a.astype(BF16), b.astype(BF16), (((0,), (0,)), ((), ())),
                           preferred_element_type=F32)


def _split3(x):
    h1 = x.astype(BF16)
    r1 = x - h1.astype(F32)
    h2 = r1.astype(BF16)
    h3 = (r1 - h2.astype(F32)).astype(BF16)
    return h1, h2, h3


def _dot_exact_lhs(a_bf16, x):
    h1, h2, h3 = _split3(x)
    f = functools.partial(jnp.dot, preferred_element_type=F32)
    return f(a_bf16, h1) + f(a_bf16, h2) + f(a_bf16, h3)


def _seg_sum(x, ones_bf16):
    hi = x.astype(BF16)
    lo = (x - hi.astype(F32)).astype(BF16)
    f = functools.partial(jnp.dot, preferred_element_type=F32)
    return f(hi, ones_bf16) + f(lo, ones_bf16)


def _sigmoid(x):
    return 1.0 / (1.0 + jnp.exp(-x))


def _log_sigmoid(x):
    return jnp.minimum(x, 0.0) - jnp.log(1.0 + jnp.exp(-jnp.abs(x)))


def _params(*sem):
    return pltpu.CompilerParams(dimension_semantics=sem, vmem_limit_bytes=VMEM_LIMIT)


def _inproj_kernel(x_ref, g_ref, w_ref, o_ref, h_ref):
    @pl.when(pl.program_id(1) == 0)
    def _():
        x = x_ref[...]
        ms = jnp.mean(x * x, axis=-1, keepdims=True)
        h_ref[...] = (x * lax.rsqrt(ms + NORM_EPS) * g_ref[...]).astype(BF16)

    o_ref[...] = jnp.dot(h_ref[...], w_ref[...], preferred_element_type=F32).astype(o_ref.dtype)


def _inproj(x2, g, w, tm=1024):
    n = x2.shape[0]
    tn = w.shape[2]
    return pl.pallas_call(
        _inproj_kernel,
        grid=(n // tm, N_PROJ // tn),
        in_specs=[pl.BlockSpec((tm, D_MODEL), lambda i, j: (i, 0)),
                  pl.BlockSpec((1, D_MODEL), lambda i, j: (0, 0)),
                  pl.BlockSpec((None, D_MODEL, tn), lambda i, j: (j, 0, 0))],
        out_specs=pl.BlockSpec((tm, tn), lambda i, j: (i, j)),
        out_shape=jax.ShapeDtypeStruct((n, N_PROJ), BF16),
        scratch_shapes=[pltpu.VMEM((tm, D_MODEL), BF16)],
        compiler_params=_params("arbitrary", "arbitrary"),
        name="inproj",
    )(x2, g, w)


def _rw_kernel(has_vres, *refs):
    if has_vres:
        (rr_ref, kr_ref, vr_ref, lo_ref, vf_ref, vec_ref, mulo_ref, wup_ref, aup_ref, vdn_ref, vup_ref,
         par_ref, ones_ref, y_ref, s_ref, cr_ref, ck_ref, cv_ref, cl_ref) = refs
    else:
        (rr_ref, kr_ref, vr_ref, lo_ref, vec_ref, mulo_ref, wup_ref, aup_ref,
         par_ref, ones_ref, y_ref, vo_ref, s_ref, cr_ref, ck_ref, cv_ref, cl_ref) = refs
    L = RW_CHUNK
    batch = rr_ref.shape[0]

    @pl.when(pl.program_id(0) == 0)
    def _():
        for ref in (s_ref, cr_ref, ck_ref, cv_ref, cl_ref):
            ref[...] = jnp.zeros_like(ref)

    def shift(x_ref, carry_ref, mu):
        outs = []
        for bi in range(batch):
            x = x_ref[bi].astype(F32)
            row = lax.broadcasted_iota(jnp.int32, x.shape, 0)
            prev = jnp.where(row == 0, carry_ref[bi, 0:1, :], pltpu.roll(x, 1, 0))
            carry_ref[bi, 0:1, :] = x[L - 1:L, :]
            outs.append(x + (prev - x) * mu)
        return jnp.concatenate(outs, axis=0)

    vec = vec_ref[...]
    ones = ones_ref[...]
    r_all = shift(rr_ref, cr_ref, vec[0:1])
    k_all = shift(kr_ref, ck_ref, vec[1:2])
    v_all = shift(vr_ref, cv_ref, vec[2:3])
    lo = shift(lo_ref, cl_ref, mulo_ref[0:1])

    lw_all = -math.exp(-0.5) * _sigmoid(vec[3:4] + _dot(jnp.tanh(lo), wup_ref[...]))
    a_all = _sigmoid(vec[4:5] + _dot(lo, aup_ref[...]))
    if has_vres:
        gate = _sigmoid(vec[7:8] + _dot(_dot(v_all, vdn_ref[...]), vup_ref[...]))
        vf = jnp.concatenate([vf_ref[bi] for bi in range(batch)], axis=0)
        v_all = v_all + (vf - v_all) * gate
    else:
        for bi in range(batch):
            vo_ref[bi] = v_all[L * bi:L * (bi + 1)]
    kk_all = k_all * vec[5:6]
    kk_all = kk_all / jnp.maximum(jnp.sqrt(_seg_sum(kk_all * kk_all, ones)), 1e-12)
    k_all = k_all * (1.0 + (a_all - 1.0) * vec[6:7])

    row = lax.broadcasted_iota(jnp.int32, (L, 128), 0)
    lane = lax.broadcasted_iota(jnp.int32, (L, 128), 1)
    col = lane & (RW_HEAD - 1)
    strict = row > col
    incl = row >= col
    eye = (row == col).astype(F32)
    first = lane < RW_HEAD
    r128 = lax.broadcasted_iota(jnp.int32, (128, 128), 0)
    c128 = lax.broadcasted_iota(jnp.int32, (128, 128), 1)
    same_head = (r128 < RW_HEAD) == (c128 < RW_HEAD)
    tri = (lax.broadcasted_iota(jnp.int32, (L, L), 0) >= lax.broadcasted_iota(jnp.int32, (L, L), 1)).astype(BF16)

    def bd(x):
        return jnp.concatenate([jnp.where(first, x, 0.0), jnp.where(first, 0.0, x)], axis=0)

    n_pairs = GROUP_W // 128
    units = [(bi, pr) for bi in range(batch) for pr in range(n_pairs)]
    idx = range(len(units))
    al, rt, bet, kt, beh, kh, p_chunk, vs = [], [], [], [], [], [], [], []
    for bi in range(batch):
        rows = slice(L * bi, L * (bi + 1))
        lw = lw_all[rows]
        r = r_all[rows]
        k = k_all[rows]
        kk = kk_all[rows]
        b = _dot_exact_lhs(tri, lw)
        b_last = b[L - 1:L, :]
        p_inv = jnp.exp(-b)
        p_end = jnp.exp(b_last - b)
        be = kk * a_all[rows]
        al.append(-(kk * jnp.exp(b - lw)))
        rt.append(r * jnp.exp(b))
        bet.append(be * p_inv)
        kt.append(k * p_inv)
        beh.append(be * p_end)
        kh.append(k * p_end)
        p_chunk.append(jnp.exp(b_last))
        vs.append(v_all[rows])

    def part(xs, i):
        bi, pr = units[i]
        return xs[bi][:, 128 * pr:128 * pr + 128]

    lhs = [jnp.concatenate([part(al, i), part(rt, i)], axis=0).astype(BF16) for i in idx]
    rhs = [jnp.concatenate([bd(part(bet, i)), bd(part(kt, i))], axis=0).astype(BF16) for i in idx]
    g = [_dot_nt(lhs[i], rhs[i]) for i in idx]
    a_ab = [jnp.where(strict, g[i][0:L, 0:128], 0.0) for i in idx]
    a_ak = [jnp.where(strict, g[i][0:L, 128:256], 0.0) for i in idx]
    b_rb = [jnp.where(incl, g[i][L:2 * L, 0:128], 0.0) for i in idx]
    b_rk = [jnp.where(incl, g[i][L:2 * L, 128:256], 0.0) for i in idx]
    s = [s_ref[i] for i in idx]
    sa = [_dot_nt(lhs[i], s[i]) for i in idx]
    vv = [_dot(jnp.concatenate([a_ak[i], b_rk[i]], axis=0), bd(part(vs, i))) for i in idx]

    t_inv = [eye + a_ab[i] for i in idx]
    x = [_dot(a_ab[i], bd(a_ab[i])) for i in idx]
    for _ in range(4):
        xt = [_dot(jnp.concatenate([x[i], t_inv[i]], axis=0), bd(x[i])) for i in idx]
        x = [xt[i][0:L] for i in idx]
        t_inv = [t_inv[i] + xt[i][L:2 * L] for i in idx]
    t_inv = [t_inv[i] + _dot(t_inv[i], bd(x[i])) for i in idx]

    u = [_dot(t_inv[i], bd(sa[i][0:L] + vv[i][0:L])) for i in idx]
    ys = [sa[i][L:2 * L] + vv[i][L:2 * L] + _dot(b_rb[i], bd(u[i])) for i in idx]
    for i in idx:
        upd = _dot_tn(jnp.concatenate([u[i], part(vs, i)], axis=0),
                      jnp.concatenate([part(beh, i), part(kh, i)], axis=0))
        s_ref[i] = s[i] * part(p_chunk, i) + jnp.where(same_head, upd, 0.0)

    par = par_ref[...]
    inv_n = 1.0 / RW_HEAD
    rows_all = batch * L
    y_all = jnp.concatenate([jnp.concatenate(ys[n_pairs * bi:n_pairs * (bi + 1)], axis=1)
                             for bi in range(batch)], axis=0)
    sums = _dot(jnp.concatenate([y_all, r_all * k_all * par[0:1]], axis=0), ones)
    yc = y_all - sums[0:rows_all] * inv_n
    var = _dot(yc * yc, ones) * inv_n
    out = yc * lax.rsqrt(var + RW_GN_EPS) * par[1:2] + par[2:3] + sums[rows_all:2 * rows_all] * v_all
    for bi in range(batch):
        y_ref[bi] = out[L * bi:L * (bi + 1)].astype(BF16)


def _rw(p_all, v_first, vec, mulo, wup, aup, vdn, vup, par, ones64, batch, seq):
    nc = seq // RW_CHUNK
    has_vres = v_first is not None
    p3 = p_all.reshape(batch, seq, N_PROJ)

    def cols(width, c):
        return pl.BlockSpec((batch, RW_CHUNK, width), lambda i: (0, i, c))

    const = lambda i: (0, 0)
    blk = cols(512, 0)
    in_specs = [cols(512, COL_RW // 512), cols(512, COL_RW // 512 + 1), cols(512, COL_RW // 512 + 2),
                cols(128, COL_RW_LORA // 128)]
    args = [p3, p3, p3, p3]
    if has_vres:
        in_specs.append(blk)
        args.append(v_first.reshape(batch, seq, 512))
    in_specs += [pl.BlockSpec((8, 512), const), pl.BlockSpec((8, 128), const),
                 pl.BlockSpec((128, 512), const), pl.BlockSpec((128, 512), const)]
    args += [vec, mulo, wup, aup]
    if has_vres:
        in_specs += [pl.BlockSpec((512, 128), const), pl.BlockSpec((128, 512), const)]
        args += [vdn, vup]
    in_specs += [pl.BlockSpec((8, 512), const), pl.BlockSpec((512, 512), const)]
    args += [par, ones64]
    n_out = 1 if has_vres else 2
    res = pl.pallas_call(
        functools.partial(_rw_kernel, has_vres),
        grid=(nc,),
        in_specs=in_specs,
        out_specs=[blk] * n_out,
        out_shape=[jax.ShapeDtypeStruct((batch, seq, 512), dt) for dt in (BF16, F32)[:n_out]],
        scratch_shapes=[pltpu.VMEM((batch * (GROUP_W // 128), 128, 128), F32)]
        + [pltpu.VMEM((batch, 8, 512), F32)] * 3 + [pltpu.VMEM((batch, 8, 128), F32)],
        compiler_params=_params("arbitrary"),
        name="rwkv7",
    )(*args)
    y = res[0].reshape(batch * seq, 512)
    return y, (v_first if has_vres else res[1].reshape(batch * seq, 512))


def _ml_kernel(q_ref, k_ref, v_ref, o_ref, g_ref, cwq_ref, cwk_ref, vec_ref, gb_ref, y_ref,
               hq_ref, hk_ref, c_ref, n_ref, m_ref):
    L = ML_CHUNK

    @pl.when(pl.program_id(1) == 0)
    def _():
        hq_ref[...] = jnp.zeros_like(hq_ref)
        hk_ref[...] = jnp.zeros_like(hk_ref)
        c_ref[...] = jnp.zeros_like(c_ref)
        n_ref[...] = jnp.zeros_like(n_ref)
        m_ref[...] = jnp.zeros_like(m_ref)

    vec = vec_ref[...]

    def conv(x_ref, buf_ref, w_ref, bias):
        x = x_ref[...].astype(F32)
        w = w_ref[...]
        buf_ref[8:8 + L, :] = x
        acc = x * w[3:4] + bias
        for s in (1, 2, 3):
            acc = acc + buf_ref[8 - s:8 - s + L, :] * w[3 - s:4 - s]
        buf_ref[0:8, :] = x[L - 8:L]
        return acc * _sigmoid(acc)

    q = conv(q_ref, hq_ref, cwq_ref, vec[0:1])
    k = conv(k_ref, hk_ref, cwk_ref, vec[1:2]) * (ML_HEAD ** -0.5)
    v = v_ref[...].astype(F32)
    og = _sigmoid(o_ref[...].astype(F32))

    g = g_ref[...].astype(F32) + gb_ref[0:1]
    lane = lax.broadcasted_iota(jnp.int32, (L, 128), 1)
    tri_b = lax.broadcasted_iota(jnp.int32, (L, L), 0) >= lax.broadcasted_iota(jnp.int32, (L, L), 1)
    cum = _dot_exact_lhs(tri_b.astype(BF16), _log_sigmoid(g))
    comb_t = jnp.where(lane < ML_HEADS, g, cum).T

    H = range(ML_HEADS)
    sls = [slice(128 * h, 128 * h + 128) for h in H]
    qs = [q[:, sl] for sl in sls]
    ks = [k[:, sl] for sl in sls]
    vs = [v[:, sl] for sl in sls]
    qk = [_dot_nt(qs[h], ks[h]) for h in H]
    c_old = [c_ref[h] for h in H]
    n_old = [n_ref[h:h + 1, :] for h in H]
    m_prev = [m_ref[h:h + 1, 0:1] for h in H]
    qc = [_dot_nt(qs[h], c_old[h]) for h in H]
    b_col = [cum[:, ML_HEADS + h:ML_HEADS + h + 1] for h in H]
    ig_col = [g[:, h:h + 1] for h in H]
    dm = [jnp.where(tri_b, b_col[h] - comb_t[ML_HEADS + h:ML_HEADS + h + 1, :] + comb_t[h:h + 1, :], -jnp.inf)
          for h in H]
    inter = [b_col[h] + m_prev[h] for h in H]
    mj = [jnp.maximum(inter[h], jnp.max(dm[h], axis=-1, keepdims=True)) for h in H]
    s = [qk[h] * jnp.exp(dm[h] - mj[h]) for h in H]
    iw = [jnp.exp(inter[h] - mj[h]) for h in H]
    num = [_dot(s[h], vs[h]) + iw[h] * qc[h] for h in H]
    den = [jnp.sum(s[h], axis=-1, keepdims=True) + iw[h] * jnp.sum(qs[h] * n_old[h], axis=-1, keepdims=True)
           for h in H]
    hh = [num[h] / jnp.maximum(jnp.abs(den[h]), jnp.exp(-mj[h])) for h in H]

    b_last = [b_col[h][L - 1:L, :] for h in H]
    g_col = [b_last[h] - b_col[h] + ig_col[h] for h in H]
    m_new = [jnp.maximum(b_last[h] + m_prev[h], jnp.max(g_col[h], axis=0, keepdims=True)) for h in H]
    a_old = [jnp.exp(b_last[h] + m_prev[h] - m_new[h]) for h in H]
    wg = [jnp.exp(g_col[h] - m_new[h]) for h in H]
    kv = [_dot_tn(vs[h] * wg[h], ks[h]) for h in H]
    outs = []
    for h in H:
        c_ref[h] = a_old[h] * c_old[h] + kv[h]
        n_ref[h:h + 1, :] = a_old[h] * n_old[h] + jnp.sum(ks[h] * wg[h], axis=0, keepdims=True)
        m_ref[h:h + 1, :] = jnp.broadcast_to(m_new[h], (1, 128))
        hn = hh[h] * lax.rsqrt(jnp.mean(hh[h] * hh[h], axis=-1, keepdims=True) + NORM_EPS) * vec[2:3, sls[h]]
        outs.append(og[:, sls[h]] * hn)
    y_ref[...] = jnp.concatenate(outs, axis=1).astype(BF16)


def _ml(p_all, cwq, cwk, vec, gb, batch, seq):
    n = batch * seq
    nc = seq // ML_CHUNK

    def rows(c):
        return lambda b, i: (b * nc + i, c)

    const = lambda b, i: (0, 0)
    base = COL_ML // 512
    return pl.pallas_call(
        _ml_kernel,
        grid=(batch, nc),
        in_specs=[pl.BlockSpec((ML_CHUNK, 512), rows(base)),
                  pl.BlockSpec((ML_CHUNK, 512), rows(base + 1)),
                  pl.BlockSpec((ML_CHUNK, 512), rows(base + 2)),
                  pl.BlockSpec((ML_CHUNK, 512), rows(base + 3)),
                  pl.BlockSpec((ML_CHUNK, 128), rows(COL_ML_GATE // 128)),
                  pl.BlockSpec((8, 512), const), pl.BlockSpec((8, 512), const),
                  pl.BlockSpec((8, 512), const), pl.BlockSpec((8, 128), const)],
        out_specs=pl.BlockSpec((ML_CHUNK, 512), rows(0)),
        out_shape=jax.ShapeDtypeStruct((n, 512), BF16),
        scratch_shapes=[pltpu.VMEM((ML_CHUNK + 8, 512), F32), pltpu.VMEM((ML_CHUNK + 8, 512), F32),
                        pltpu.VMEM((ML_HEADS, 128, 128), F32), pltpu.VMEM((8, 128), F32),
                        pltpu.VMEM((8, 128), F32)],
        compiler_params=_params("arbitrary", "arbitrary"),
        name="mlstm",
    )(p_all, p_all, p_all, p_all, p_all, cwq, cwk, vec, gb)


def _rope128(x, cos_t, sin_t, swap):
    return x * cos_t + jnp.dot(x.astype(BF16), swap, preferred_element_type=F32) * sin_t


def _swap_matrix(half, seg):
    lane = jnp.arange(128)
    pos = lane % seg
    src = jnp.where(pos < half, lane + half, jnp.where(pos < 2 * half, lane - half, lane))
    return (lane[:, None] == src[None, :]).astype(BF16)


def _mla_prep_kernel(p_ref, cos_ref, sin_ref, vec_ref, wq_ref, wkv_ref, swap_ref, q_ref, k_ref, vt_ref):
    p = p_ref[...].astype(F32)
    vec = vec_ref[...]
    cos_t = cos_ref[...]
    sin_t = sin_ref[...]
    swap = swap_ref[...]
    q_lat = p[:, 0:MLA_Q_LORA]
    kv_lat = p[:, MLA_Q_LORA:MLA_Q_LORA + MLA_KV_LORA]
    k_pe = p[:, 640:768]
    qn = q_lat * lax.rsqrt(jnp.mean(q_lat * q_lat, axis=-1, keepdims=True) + NORM_EPS) * vec[0:1, 0:MLA_Q_LORA]
    kvn = kv_lat * lax.rsqrt(jnp.mean(kv_lat * kv_lat, axis=-1, keepdims=True) + NORM_EPS) * vec[1:2, 0:MLA_KV_LORA]
    q4 = _dot(qn, wq_ref[...])
    kv4 = _dot(kvn, wkv_ref[...])
    pe_ss = jnp.sum(k_pe * k_pe, axis=-1, keepdims=True)
    pe_rot = _rope128(k_pe * vec[4:5, 0:128], cos_t, sin_t, swap)
    qs, ks = [], []
    for h in range(MLA_HEADS):
        qh = q4[:, MLA_PAD * h:MLA_PAD * (h + 1)]
        inv = lax.rsqrt(jnp.sum(qh * qh, axis=-1, keepdims=True) * (1.0 / MLA_QK) + NORM_EPS)
        qh = qh * inv * vec[2:3, MLA_PAD * h:MLA_PAD * (h + 1)]
        qs += [qh[:, 0:128], _rope128(qh[:, 128:256], cos_t, sin_t, swap)]
        kn = kv4[:, 128 * h:128 * (h + 1)]
        inv = lax.rsqrt((jnp.sum(kn * kn, axis=-1, keepdims=True) + pe_ss) * (1.0 / MLA_QK) + NORM_EPS)
        ks += [kn * inv * vec[3:4, 0:128], pe_rot * inv]
    q_ref[...] = jnp.concatenate(qs, axis=1).astype(BF16)
    k_ref[...] = jnp.concatenate(ks, axis=1).astype(BF16)
    _store_vt(vt_ref, kv4[:, 512:1024])


def _store_vt(vt_ref, v):
    heads, _, rows = vt_ref.shape
    eye = (lax.broadcasted_iota(jnp.int32, (128, 128), 0)
           == lax.broadcasted_iota(jnp.int32, (128, 128), 1)).astype(BF16)
    ones = jnp.ones((DV_AUG - 128, rows), BF16)
    for h in range(heads):
        vt = _dot_nt(eye, v[:, 128 * h:128 * (h + 1)]).astype(BF16)
        vt_ref[h] = jnp.concatenate([vt, ones], axis=0)


def _vt_spec(heads, nt):
    return pl.BlockSpec((None, heads, None, DV_AUG, ATT_BLOCK), lambda i: (i // nt, 0, i % nt, 0, 0))


def _mla_prep(p_all, cos_t, sin_t, vec, wq, wkv, batch, seq):
    n = batch * seq
    tr = ATT_BLOCK
    nt = seq // tr
    const = lambda i: (0, 0)
    return pl.pallas_call(
        _mla_prep_kernel,
        grid=(n // tr,),
        in_specs=[pl.BlockSpec((tr, 768), lambda i: (i, COL_MLA // 768)),
                  pl.BlockSpec((tr, 128), lambda i: (i % nt, 0)),
                  pl.BlockSpec((tr, 128), lambda i: (i % nt, 0)),
                  pl.BlockSpec((8, 1024), const),
                  pl.BlockSpec((MLA_Q_LORA, 1024), const),
                  pl.BlockSpec((MLA_KV_LORA, 1024), const),
                  pl.BlockSpec((128, 128), const)],
        out_specs=[pl.BlockSpec((tr, 1024), lambda i: (i, 0)),
                   pl.BlockSpec((tr, 1024), lambda i: (i, 0)),
                   _vt_spec(MLA_HEADS, nt)],
        out_shape=[jax.ShapeDtypeStruct((n, 1024), BF16), jax.ShapeDtypeStruct((n, 1024), BF16),
                   jax.ShapeDtypeStruct((batch, MLA_HEADS, nt, DV_AUG, tr), BF16)],
        compiler_params=_params("arbitrary"),
        name="mla_prep",
    )(p_all, cos_t, sin_t, vec, wq, wkv, _swap_matrix(MLA_ROPE // 2, 128))


def _attn_block_t(kbs, vts, units, masked):
    sub = ATT_SUB
    s, cols = [], []
    for q, _, _, c0, h in units:
        keys = c0 + sub if masked else kbs[h].shape[0]
        st = _dot_nt(kbs[h][0:keys], q)
        if masked:
            keep = (lax.broadcasted_iota(jnp.int32, st.shape, 0)
                    <= lax.broadcasted_iota(jnp.int32, st.shape, 1) + c0)
            st = jnp.where(keep, st, -jnp.inf)
        s.append(st)
        cols.append(slice(c0, c0 + sub))
    idx = range(len(units))
    m_old = [units[i][1][:, cols[i]] for i in idx]
    m_new = [jnp.maximum(m_old[i], jnp.max(s[i], axis=0, keepdims=True)) for i in idx]
    p = [jnp.exp2(s[i] - m_new[i]).astype(BF16) for i in idx]
    alpha = [jnp.exp2(m_old[i] - m_new[i]) for i in idx]
    for i in idx:
        _, m_ref, acc_ref, _, h = units[i]
        pv = jnp.dot(vts[h][:, 0:s[i].shape[0]], p[i], preferred_element_type=F32)
        acc_ref[:, cols[i]] = alpha[i] * acc_ref[:, cols[i]] + pv
        m_ref[:, cols[i]] = m_new[i]


def _init_stats(m_ref, acc_ref):
    m_ref[...] = jnp.full_like(m_ref, -jnp.inf)
    acc_ref[...] = jnp.zeros_like(acc_ref)


def _attn_sweep(i, k_ref, vt_ref, dk, units):
    tb = ATT_BLOCK
    heads = vt_ref.shape[0]

    def block(j, masked):
        start = pl.multiple_of(j * tb, tb)
        kbs = [k_ref[pl.ds(start, tb), dk * h:dk * (h + 1)] for h in range(heads)]
        _attn_block_t(kbs, [vt_ref[h, j] for h in range(heads)], units, masked)

    def body(j, carry):
        block(j, False)
        return carry

    lax.fori_loop(0, i, body, 0)
    block(i, True)


def _attn_result(acc_ref):
    acc = acc_ref[...]
    return acc[0:128] / acc[128:129]


def _attn_specs(dk, nq, seq, heads):
    tb = ATT_BLOCK
    return [pl.BlockSpec((tb, dk * heads), lambda b, h, i: (b * nq + i, h)),
            pl.BlockSpec((seq, dk * heads), lambda b, h, i: (b, h)),
            pl.BlockSpec((None, heads, nq, DV_AUG, tb), lambda b, h, i: (b, h, 0, 0, 0))]


def _mla_attn_kernel(q_ref, k_ref, vt_ref, o_ref, *stats):
    heads = vt_ref.shape[0]
    units = []
    for h in range(heads):
        m_ref, acc_ref = stats[2 * h:2 * h + 2]
        _init_stats(m_ref, acc_ref)
        units += [(q_ref[c0:c0 + ATT_SUB, MLA_PAD * h:MLA_PAD * (h + 1)], m_ref, acc_ref, c0, h)
                  for c0 in range(0, ATT_BLOCK, ATT_SUB)]
    _attn_sweep(pl.program_id(2), k_ref, vt_ref, MLA_PAD, units)
    o_ref[...] = jnp.concatenate([_attn_result(stats[2 * h + 1]).T for h in range(heads)], axis=1).astype(BF16)


def _mla_attn(q, k, vt, batch, seq):
    n = batch * seq
    tb = ATT_BLOCK
    nq = seq // tb
    return pl.pallas_call(
        _mla_attn_kernel,
        grid=(batch, MLA_HEADS // MLA_ATT_HEADS, nq),
        in_specs=_attn_specs(MLA_PAD, nq, seq, MLA_ATT_HEADS),
        out_specs=pl.BlockSpec((tb, 128 * MLA_ATT_HEADS), lambda b, h, i: (b * nq + i, h)),
        out_shape=jax.ShapeDtypeStruct((n, 512), BF16),
        scratch_shapes=[pltpu.VMEM((1, tb), F32), pltpu.VMEM((DV_AUG, tb), F32)] * MLA_ATT_HEADS,
        compiler_params=_params("arbitrary", "arbitrary", "arbitrary"),
        name="mla_attn",
    )(q, k, vt)


def _da_prep_kernel(q_ref, k_ref, v_ref, cos_ref, sin_ref, vec_ref, ones_ref, swap_ref, qo_ref, ko_ref, vo_ref):
    vec = vec_ref[...]
    ones = ones_ref[...]
    cos_t = cos_ref[...]
    sin_t = sin_ref[...]
    swap = swap_ref[...]

    def norm_rope(x, gain):
        ss = _dot(x * x, ones) * (1.0 / DA_QK)
        x = x * lax.rsqrt(ss + NORM_EPS) * gain
        parts = [_rope128(x[:, 128 * c:128 * (c + 1)], cos_t, sin_t, swap) for c in range(4)]
        return jnp.concatenate(parts, axis=1)

    qo_ref[...] = norm_rope(q_ref[...].astype(F32), vec[0:1]).astype(BF16)
    ko_ref[...] = norm_rope(k_ref[...].astype(F32), vec[1:2]).astype(BF16)
    _store_vt(vo_ref, v_ref[...])


def _da_prep(p_all, cos_t, sin_t, vec, ones64, batch, seq):
    n = batch * seq
    tr = ATT_BLOCK
    nt = seq // tr
    const = lambda i: (0, 0)
    base = COL_DA // 512
    out = jax.ShapeDtypeStruct((n, 512), BF16)
    return pl.pallas_call(
        _da_prep_kernel,
        grid=(n // tr,),
        in_specs=[pl.BlockSpec((tr, 512), lambda i: (i, base)),
                  pl.BlockSpec((tr, 512), lambda i: (i, base + 1)),
                  pl.BlockSpec((tr, 512), lambda i: (i, base + 2)),
                  pl.BlockSpec((tr, 128), lambda i: (i % nt, 0)),
                  pl.BlockSpec((tr, 128), lambda i: (i % nt, 0)),
                  pl.BlockSpec((8, 512), const),
                  pl.BlockSpec((512, 512), const),
                  pl.BlockSpec((128, 128), const)],
        out_specs=[pl.BlockSpec((tr, 512), lambda i: (i, 0))] * 2 + [_vt_spec(DA_HEADS, nt)],
        out_shape=[out, out, jax.ShapeDtypeStruct((batch, DA_HEADS, nt, DV_AUG, tr), BF16)],
        compiler_params=_params("arbitrary"),
        name="da_prep",
    )(p_all, p_all, p_all, cos_t, sin_t, vec, ones64, _swap_matrix(DA_ROT // 2, DA_QK))


def _da_attn_kernel(lam_init, q_ref, k_ref, vt_ref, lam_ref, g_ref, o_ref, *stats):
    heads = vt_ref.shape[0]
    lane = lax.broadcasted_iota(jnp.int32, (ATT_BLOCK, 128), 1)
    units = []
    for h in range(heads):
        q = q_ref[:, 128 * h:128 * (h + 1)]
        for mp in range(2):
            qm = jnp.where((lane < DA_QK) == (mp == 0), q, jnp.zeros_like(q))
            m_ref, acc_ref = stats[4 * h + 2 * mp:4 * h + 2 * mp + 2]
            _init_stats(m_ref, acc_ref)
            units += [(qm[c0:c0 + ATT_SUB], m_ref, acc_ref, c0, h) for c0 in range(0, ATT_BLOCK, ATT_SUB)]
    _attn_sweep(pl.program_id(2), k_ref, vt_ref, 128, units)

    lv = lam_ref[...]
    lam = (jnp.exp(jnp.sum(lv[0:1] * lv[1:2], axis=-1, keepdims=True))
           - jnp.exp(jnp.sum(lv[2:3] * lv[3:4], axis=-1, keepdims=True)) + lam_init)
    outs = []
    for h in range(heads):
        o = (_attn_result(stats[4 * h + 1]) - lam * _attn_result(stats[4 * h + 3])).T
        o = o * lax.rsqrt(jnp.mean(o * o, axis=-1, keepdims=True) + NORM_EPS) * g_ref[0:1]
        outs.append(o * (1.0 - lam_init))
    o_ref[...] = jnp.concatenate(outs, axis=1).astype(BF16)


def _da_attn(q, k, vt, lam_vec, sub_g, lam_init, batch, seq):
    n = batch * seq
    tb = ATT_BLOCK
    nq = seq // tb
    const = lambda b, h, i: (0, 0)
    row = pltpu.VMEM((1, tb), F32)
    acc = pltpu.VMEM((DV_AUG, tb), F32)
    return pl.pallas_call(
        functools.partial(_da_attn_kernel, lam_init),
        grid=(batch, DA_HEADS // DA_ATT_HEADS, nq),
        in_specs=_attn_specs(128, nq, seq, DA_ATT_HEADS)
        + [pl.BlockSpec((8, 128), const), pl.BlockSpec((8, 128), const)],
        out_specs=pl.BlockSpec((tb, 128 * DA_ATT_HEADS), lambda b, h, i: (b * nq + i, h)),
        out_shape=jax.ShapeDtypeStruct((n, 512), BF16),
        scratch_shapes=[row, acc] * (2 * DA_ATT_HEADS),
        compiler_params=_params("arbitrary", "arbitrary", "arbitrary"),
        name="da_attn",
    )(q, k, vt, lam_vec, sub_g)


def _outproj_kernel(ya_ref, yb_ref, yc_ref, yd_ref, z_ref, beta_ref, x_ref, w_ref, o_ref):
    y_refs = (ya_ref, yb_ref, yc_ref, yd_ref)
    acc = x_ref[...]
    for c in range(4):
        sl = slice(GROUP_W * c, GROUP_W * (c + 1))
        z = z_ref[:, sl].astype(F32)
        gated = (y_refs[c][...].astype(F32) * beta_ref[0:1, sl] * (z * _sigmoid(z))).astype(BF16)
        acc = acc + jnp.dot(gated, w_ref[sl, :], preferred_element_type=F32)
    o_ref[...] = acc


def _outproj(ys, p_all, beta_row, x2, w, tm=512):
    n = x2.shape[0]
    yspec = pl.BlockSpec((tm, 512), lambda i: (i, 0))
    resident = dict(pipeline_mode=pl.Buffered(1))
    return pl.pallas_call(
        _outproj_kernel,
        grid=(n // tm,),
        in_specs=[yspec] * 4 + [pl.BlockSpec((tm, D_MODEL), lambda i: (i, COL_Z // D_MODEL))]
        + [pl.BlockSpec((8, D_MODEL), lambda i: (0, 0), **resident),
           pl.BlockSpec((tm, D_MODEL), lambda i: (i, 0)),
           pl.BlockSpec((D_MODEL, D_MODEL), lambda i: (0, 0), **resident)],
        out_specs=pl.BlockSpec((tm, D_MODEL), lambda i: (i, 0)),
        out_shape=jax.ShapeDtypeStruct((n, D_MODEL), F32),
        compiler_params=_params("arbitrary"),
        name="outproj",
    )(*ys, p_all, beta_row, x2, w)


def _rows(rows, width, n_rows=8):
    padded = [jnp.pad(r.astype(F32), (0, width - r.shape[0])) for r in rows]
    padded += [jnp.zeros((width,), F32)] * (n_rows - len(rows))
    return jnp.stack(padded)


def _pad_to(a, shape):
    return jnp.pad(a, [(0, s - d) for s, d in zip(shape, a.shape)])


def _arrange_w_in(w):
    w = w.astype(BF16)
    zeros = lambda c: jnp.zeros((D_MODEL, c), BF16)
    return jnp.concatenate([
        w[:, 5704:6408], zeros(64),
        w[:, 3584:3648], zeros(64),
        w[:, 5696:5704], zeros(120),
        w[:, 2048:3584],
        w[:, 3648:5696],
        w[:, 6408:7944],
        w[:, 0:2048],
    ], axis=1).reshape(D_MODEL, N_PROJ // INPROJ_TN, INPROJ_TN).transpose(1, 0, 2)


def _rope_tables(seq, rot, seg, fill):
    half = rot // 2
    inv = ROPE_THETA ** (-jnp.arange(0, rot, 2, dtype=F32) / rot)
    ang = jnp.arange(seq, dtype=F32)[:, None] * inv[None, :]
    cos, sin = jnp.cos(ang), jnp.sin(ang)
    rest = seg - rot
    cos_s = jnp.concatenate([cos, cos, jnp.full((seq, rest), fill, F32)], axis=1)
    sin_s = jnp.concatenate([-sin, sin, jnp.zeros((seq, rest), F32)], axis=1)
    reps = 128 // seg
    return jnp.tile(cos_s, (1, reps)), jnp.tile(sin_s, (1, reps))


def kernel(x, norm_g, w_in, w_out, branch_beta, rw_mu, rw_w0, rw_w_up, rw_a0, rw_a_up, rw_v0, rw_v_dn, rw_v_up, rw_k_k, rw_k_a, rw_r_k, rw_ln_w, rw_ln_b, ml_conv_w, ml_conv_b, ml_i_b, ml_f_b, ml_norm_g, mla_q_norm_g, mla_kv_norm_g, mla_w_q_b, mla_w_kv_b, mla_q_g, mla_k_g, da_q_g, da_k_g, da_lq1, da_lk1, da_lq2, da_lk2, da_sub_g):
    batch, seq, _ = x.shape
    depth = w_in.shape[0]
    n = batch * seq
    x2 = x.reshape(n, D_MODEL)

    seg_id = jnp.arange(GROUP_W) // RW_HEAD
    ones64 = (seg_id[:, None] == seg_id[None, :]).astype(BF16)
    mla_cos, mla_sin = _rope_tables(seq, MLA_ROPE, 128, 0.0)
    da_cos, da_sin = _rope_tables(seq, DA_ROT, DA_QK, 1.0)

    v_first = None
    for l in range(depth):
        p_all = _inproj(x2, norm_g[l][None, :], _arrange_w_in(w_in[l]))

        mu = rw_mu[l]
        vec = _rows([mu[0:512], mu[512:1024], mu[1024:1536], rw_w0[l], rw_a0[l], rw_k_k[l], rw_k_a[l]]
                    + ([rw_v0[l - 1]] if l > 0 else []), 512)
        mulo = _rows([mu[1536:1600]], 128)
        wup = _pad_to(rw_w_up[l], (128, 512)).astype(BF16)
        aup = jnp.zeros((128, 512), F32).at[32:64].set(rw_a_up[l]).astype(BF16)
        if l > 0:
            vdn = _pad_to(rw_v_dn[l - 1], (512, 128)).astype(BF16)
            vup = _pad_to(rw_v_up[l - 1], (128, 512)).astype(BF16)
        else:
            vdn = vup = None
        par = _rows([rw_r_k[l].reshape(-1), rw_ln_w[l], rw_ln_b[l]], 512)
        y_rw, v_first = _rw(p_all, v_first, vec, mulo, wup, aup, vdn, vup, par, ones64, batch, seq)

        cw = ml_conv_w[l]
        cb = ml_conv_b[l]
        ml_vec = _rows([cb[0:512], cb[512:1024], ml_norm_g[l]], 512)
        gb = _rows([jnp.concatenate([ml_i_b[l], ml_f_b[l]])], 128)
        y_ml = _ml(p_all, _pad_to(cw[:, 0:512], (8, 512)), _pad_to(cw[:, 512:1024], (8, 512)),
                   ml_vec, gb, batch, seq)

        scale = MLA_QK ** -0.5 * LOG2E
        qg = jnp.tile(jnp.concatenate([mla_q_g[l] * scale, jnp.zeros((64,), F32)]), MLA_HEADS)
        mla_vec = _rows([mla_q_norm_g[l], mla_kv_norm_g[l], qg, mla_k_g[l][0:128], mla_k_g[l][128:192]], 1024)
        wq = _pad_to(mla_w_q_b[l].reshape(MLA_Q_LORA, MLA_HEADS, MLA_QK),
                     (MLA_Q_LORA, MLA_HEADS, MLA_PAD)).reshape(MLA_Q_LORA, 1024).astype(BF16)
        wkv = mla_w_kv_b[l].reshape(MLA_KV_LORA, MLA_HEADS, 2, 128).transpose(0, 2, 1, 3)
        wkv = wkv.reshape(MLA_KV_LORA, 1024).astype(BF16)
        mq, mk, mv = _mla_prep(p_all, mla_cos, mla_sin, mla_vec, wq, wkv, batch, seq)
        y_mla = _mla_attn(mq, mk, mv, batch, seq)

        da_vec = _rows([jnp.tile(da_q_g[l], 8) * (DA_QK ** -0.5 * LOG2E), jnp.tile(da_k_g[l], 8)], 512)
        dq, dk, dv = _da_prep(p_all, da_cos, da_sin, da_vec, ones64, batch, seq)
        lam_vec = _rows([da_lq1[l], da_lk1[l], da_lq2[l], da_lk2[l]], 128)
        lam_init = 0.8 - 0.6 * math.exp(-0.3 * l)
        y_da = _da_attn(dq, dk, dv, lam_vec, _rows([da_sub_g[l]], 128), lam_init, batch, seq)

        beta_row = _rows([jnp.repeat(branch_beta[l], GROUP_W)], D_MODEL)
        x2 = _outproj((y_rw, y_ml, y_mla, y_da), p_all, beta_row, x2, w_out[l].astype(BF16))
    return x2.reshape(batch, seq, D_MODEL)
```

```python
import functools
import math

import jax
import jax.numpy as jnp
from jax import lax
from jax.experimental import pallas as pl
from jax.experimental.pallas import tpu as pltpu

F32 = jnp.float32
BF16 = jnp.bfloat16

D_MODEL = 2048
GROUP_W = 512
NORM_EPS = 1e-6
ROPE_THETA = 500000.0

RW_HEAD = 64
RW_GN_EPS = 64e-5
RW_CHUNK = 64
ML_HEADS = 4
ML_HEAD = 128
ML_CHUNK = 256
MLA_HEADS = 4
MLA_NOPE = 128
MLA_ROPE = 64
MLA_QK = 192
MLA_Q_LORA = 384
MLA_KV_LORA = 256
MLA_PAD = 256
DA_HEADS = 4
DA_QK = 64
DA_ROT = 16
ATT_BLOCK = 512
ATT_SUB = 256
MLA_ATT_HEADS = 4
DA_ATT_HEADS = 4
DV_AUG = 144
LOG2E = math.log2(math.e)

N_PROJ = 8192
COL_MLA = 0
COL_RW_LORA = 768
COL_ML_GATE = 896
COL_RW = 1024
COL_ML = 2560
COL_DA = 4608
COL_Z = 6144

INPROJ_TN = 1024

VMEM_LIMIT = 48 * 1024 * 1024


def _dot(a, b):
    return jnp.dot(a.astype(BF16), b.astype(BF16), preferred_element_type=F32)


def _dot_nt(a, b):
    return lax.dot_general(a.astype(BF16), b.astype(BF16), (((1,), (1,)), ((), ())),
                           preferred_element_type=F32)


def _dot_tn(a, b):
    return lax.dot_general(a.astype(BF16), b.astype(BF16), (((0,), (0,)), ((), ())),
                           preferred_element_type=F32)


def _split3(x):
    h1 = x.astype(BF16)
    r1 = x - h1.astype(F32)
    h2 = r1.astype(BF16)
    h3 = (r1 - h2.astype(F32)).astype(BF16)
    return h1, h2, h3


def _dot_exact_lhs(a_bf16, x):
    h1, h2, h3 = _split3(x)
    f = functools.partial(jnp.dot, preferred_element_type=F32)
    return f(a_bf16, h1) + f(a_bf16, h2) + f(a_bf16, h3)


def _seg_sum(x, ones_bf16):
    hi = x.astype(BF16)
    lo = (x - hi.astype(F32)).astype(BF16)
    f = functools.partial(jnp.dot, preferred_element_type=F32)
    return f(hi, ones_bf16) + f(lo, ones_bf16)


def _sigmoid(x):
    return 1.0 / (1.0 + jnp.exp(-x))


def _log_sigmoid(x):
    return jnp.minimum(x, 0.0) - jnp.log(1.0 + jnp.exp(-jnp.abs(x)))


def _params(*sem):
    return pltpu.CompilerParams(dimension_semantics=sem, vmem_limit_bytes=VMEM_LIMIT)


def _inproj_kernel(x_ref, g_ref, w_ref, o_ref, h_ref):
    @pl.when(pl.program_id(1) == 0)
    def _():
        x = x_ref[...]
        ms = jnp.mean(x * x, axis=-1, keepdims=True)
        h_ref[...] = (x * lax.rsqrt(ms + NORM_EPS) * g_ref[...]).astype(BF16)

    o_ref[...] = jnp.dot(h_ref[...], w_ref[...], preferred_element_type=F32)


def _matmul_kernel(h_ref, w_ref, o_ref):
    o_ref[...] = jnp.dot(h_ref[...], w_ref[...], preferred_element_type=F32)


def _inproj(x2, g, w, h=None, tm=1024):
    n = x2.shape[0]
    tn = w.shape[2]
    if h is not None:
        return pl.pallas_call(
            _matmul_kernel,
            grid=(n // tm, N_PROJ // tn),
            in_specs=[pl.BlockSpec((tm, D_MODEL), lambda i, j: (i, 0)),
                      pl.BlockSpec((None, D_MODEL, tn), lambda i, j: (j, 0, 0))],
            out_specs=pl.BlockSpec((tm, tn), lambda i, j: (i, j)),
            out_shape=jax.ShapeDtypeStruct((n, N_PROJ), F32),
            compiler_params=_params("arbitrary", "arbitrary"),
            name="inproj",
        )(h, w)
    return pl.pallas_call(
        _inproj_kernel,
        grid=(n // tm, N_PROJ // tn),
        in_specs=[pl.BlockSpec((tm, D_MODEL), lambda i, j: (i, 0)),
                  pl.BlockSpec((1, D_MODEL), lambda i, j: (0, 0)),
                  pl.BlockSpec((None, D_MODEL, tn), lambda i, j: (j, 0, 0))],
        out_specs=pl.BlockSpec((tm, tn), lambda i, j: (i, j)),
        out_shape=jax.ShapeDtypeStruct((n, N_PROJ), F32),
        scratch_shapes=[pltpu.VMEM((tm, D_MODEL), BF16)],
        compiler_params=_params("arbitrary", "arbitrary"),
        name="inproj",
    )(x2, g, w)


def _rw_kernel(has_vres, *refs):
    if has_vres:
        (rr_ref, kr_ref, vr_ref, lo_ref, vf_ref, vec_ref, mulo_ref, wup_ref, aup_ref, vdn_ref, vup_ref,
         par_ref, ones_ref, y_ref, s_ref, cr_ref, ck_ref, cv_ref, cl_ref) = refs
    else:
        (rr_ref, kr_ref, vr_ref, lo_ref, vec_ref, mulo_ref, wup_ref, aup_ref,
         par_ref, ones_ref, y_ref, vo_ref, s_ref, cr_ref, ck_ref, cv_ref, cl_ref) = refs
    L = RW_CHUNK
    batch = rr_ref.shape[0]

    @pl.when(pl.program_id(0) == 0)
    def _():
        for ref in (s_ref, cr_ref, ck_ref, cv_ref, cl_ref):
            ref[...] = jnp.zeros_like(ref)

    def shift(x_ref, carry_ref, mu):
        outs = []
        for bi in range(batch):
            x = x_ref[bi]
            row = lax.broadcasted_iota(jnp.int32, x.shape, 0)
            prev = jnp.where(row == 0, carry_ref[bi, 0:1, :], pltpu.roll(x, 1, 0))
            carry_ref[bi, 0:1, :] = x[L - 1:L, :]
            outs.append(x + (prev - x) * mu)
        return jnp.concatenate(outs, axis=0)

    vec = vec_ref[...]
    ones = ones_ref[...]
    r_all = shift(rr_ref, cr_ref, vec[0:1])
    k_all = shift(kr_ref, ck_ref, vec[1:2])
    v_all = shift(vr_ref, cv_ref, vec[2:3])
    lo = shift(lo_ref, cl_ref, mulo_ref[0:1])

    lw_all = -math.exp(-0.5) * _sigmoid(vec[3:4] + _dot(jnp.tanh(lo), wup_ref[...]))
    a_all = _sigmoid(vec[4:5] + _dot(lo, aup_ref[...]))
    if has_vres:
        gate = _sigmoid(vec[7:8] + _dot(_dot(v_all, vdn_ref[...]), vup_ref[...]))
        vf = jnp.concatenate([vf_ref[bi] for bi in range(batch)], axis=0)
        v_all = v_all + (vf - v_all) * gate
    else:
        for bi in range(batch):
            vo_ref[bi] = v_all[L * bi:L * (bi + 1)]
    kk_all = k_all * vec[5:6]
    kk_all = kk_all / jnp.maximum(jnp.sqrt(_seg_sum(kk_all * kk_all, ones)), 1e-12)
    k_all = k_all * (1.0 + (a_all - 1.0) * vec[6:7])

    row = lax.broadcasted_iota(jnp.int32, (L, 128), 0)
    lane = lax.broadcasted_iota(jnp.int32, (L, 128), 1)
    col = lane & (RW_HEAD - 1)
    strict = row > col
    incl = row >= col
    eye = (row == col).astype(F32)
    first = lane < RW_HEAD
    r128 = lax.broadcasted_iota(jnp.int32, (128, 128), 0)
    c128 = lax.broadcasted_iota(jnp.int32, (128, 128), 1)
    same_head = (r128 < RW_HEAD) == (c128 < RW_HEAD)
    tri = (lax.broadcasted_iota(jnp.int32, (L, L), 0) >= lax.broadcasted_iota(jnp.int32, (L, L), 1)).astype(BF16)

    def bd(x):
        return jnp.concatenate([jnp.where(first, x, 0.0), jnp.where(first, 0.0, x)], axis=0)

    n_pairs = GROUP_W // 128
    units = [(bi, pr) for bi in range(batch) for pr in range(n_pairs)]
    idx = range(len(units))
    al, rt, bet, kt, beh, kh, p_chunk, vs = [], [], [], [], [], [], [], []
    for bi in range(batch):
        rows = slice(L * bi, L * (bi + 1))
        lw = lw_all[rows]
        r = r_all[rows]
        k = k_all[rows]
        kk = kk_all[rows]
        b = _dot_exact_lhs(tri, lw)
        b_last = b[L - 1:L, :]
        p_inv = jnp.exp(-b)
        p_end = jnp.exp(b_last - b)
        be = kk * a_all[rows]
        al.append(-(kk * jnp.exp(b - lw)))
        rt.append(r * jnp.exp(b))
        bet.append(be * p_inv)
        kt.append(k * p_inv)
        beh.append(be * p_end)
        kh.append(k * p_end)
        p_chunk.append(jnp.exp(b_last))
        vs.append(v_all[rows])

    def part(xs, i):
        bi, pr = units[i]
        return xs[bi][:, 128 * pr:128 * pr + 128]

    lhs = [jnp.concatenate([part(al, i), part(rt, i)], axis=0).astype(BF16) for i in idx]
    rhs = [jnp.concatenate([bd(part(bet, i)), bd(part(kt, i))], axis=0).astype(BF16) for i in idx]
    g = [_dot_nt(lhs[i], rhs[i]) for i in idx]
    a_ab = [jnp.where(strict, g[i][0:L, 0:128], 0.0) for i in idx]
    a_ak = [jnp.where(strict, g[i][0:L, 128:256], 0.0) for i in idx]
    b_rb = [jnp.where(incl, g[i][L:2 * L, 0:128], 0.0) for i in idx]
    b_rk = [jnp.where(incl, g[i][L:2 * L, 128:256], 0.0) for i in idx]
    s = [s_ref[i] for i in idx]
    sa = [_dot_nt(lhs[i], s[i]) for i in idx]
    vv = [_dot(jnp.concatenate([a_ak[i], b_rk[i]], axis=0), bd(part(vs, i))) for i in idx]

    t_inv = [eye + a_ab[i] for i in idx]
    x = [_dot(a_ab[i], bd(a_ab[i])) for i in idx]
    for _ in range(4):
        xt = [_dot(jnp.concatenate([x[i], t_inv[i]], axis=0), bd(x[i])) for i in idx]
        x = [xt[i][0:L] for i in idx]
        t_inv = [t_inv[i] + xt[i][L:2 * L] for i in idx]
    t_inv = [t_inv[i] + _dot(t_inv[i], bd(x[i])) for i in idx]

    u = [_dot(t_inv[i], bd(sa[i][0:L] + vv[i][0:L])) for i in idx]
    ys = [sa[i][L:2 * L] + vv[i][L:2 * L] + _dot(b_rb[i], bd(u[i])) for i in idx]
    for i in idx:
        upd = _dot_tn(jnp.concatenate([u[i], part(vs, i)], axis=0),
                      jnp.concatenate([part(beh, i), part(kh, i)], axis=0))
        s_ref[i] = s[i] * part(p_chunk, i) + jnp.where(same_head, upd, 0.0)

    par = par_ref[...]
    inv_n = 1.0 / RW_HEAD
    rows_all = batch * L
    y_all = jnp.concatenate([jnp.concatenate(ys[n_pairs * bi:n_pairs * (bi + 1)], axis=1)
                             for bi in range(batch)], axis=0)
    sums = _dot(jnp.concatenate([y_all, r_all * k_all * par[0:1]], axis=0), ones)
    yc = y_all - sums[0:rows_all] * inv_n
    var = _dot(yc * yc, ones) * inv_n
    out = yc * lax.rsqrt(var + RW_GN_EPS) * par[1:2] + par[2:3] + sums[rows_all:2 * rows_all] * v_all
    for bi in range(batch):
        y_ref[bi] = out[L * bi:L * (bi + 1)].astype(BF16)


def _rw(p_all, v_first, vec, mulo, wup, aup, vdn, vup, par, ones64, batch, seq):
    nc = seq // RW_CHUNK
    has_vres = v_first is not None
    p3 = p_all.reshape(batch, seq, N_PROJ)

    def cols(width, c):
        return pl.BlockSpec((batch, RW_CHUNK, width), lambda i: (0, i, c))

    const = lambda i: (0, 0)
    blk = cols(512, 0)
    in_specs = [cols(512, COL_RW // 512), cols(512, COL_RW // 512 + 1), cols(512, COL_RW // 512 + 2),
                cols(128, COL_RW_LORA // 128)]
    args = [p3, p3, p3, p3]
    if has_vres:
        in_specs.append(blk)
        args.append(v_first.reshape(batch, seq, 512))
    in_specs += [pl.BlockSpec((8, 512), const), pl.BlockSpec((8, 128), const),
                 pl.BlockSpec((128, 512), const), pl.BlockSpec((128, 512), const)]
    args += [vec, mulo, wup, aup]
    if has_vres:
        in_specs += [pl.BlockSpec((512, 128), const), pl.BlockSpec((128, 512), const)]
        args += [vdn, vup]
    in_specs += [pl.BlockSpec((8, 512), const), pl.BlockSpec((512, 512), const)]
    args += [par, ones64]
    n_out = 1 if has_vres else 2
    res = pl.pallas_call(
        functools.partial(_rw_kernel, has_vres),
        grid=(nc,),
        in_specs=in_specs,
        out_specs=[blk] * n_out,
        out_shape=[jax.ShapeDtypeStruct((batch, seq, 512), dt) for dt in (BF16, F32)[:n_out]],
        scratch_shapes=[pltpu.VMEM((batch * (GROUP_W // 128), 128, 128), F32)]
        + [pltpu.VMEM((batch, 8, 512), F32)] * 3 + [pltpu.VMEM((batch, 8, 128), F32)],
        compiler_params=_params("arbitrary"),
        name="rwkv7",
    )(*args)
    y = res[0].reshape(batch * seq, 512)
    return y, (v_first if has_vres else res[1].reshape(batch * seq, 512))


def _ml_kernel(q_ref, k_ref, v_ref, o_ref, g_ref, cwq_ref, cwk_ref, vec_ref, gb_ref, y_ref,
               hq_ref, hk_ref, c_ref, n_ref, m_ref):
    L = ML_CHUNK

    @pl.when(pl.program_id(1) == 0)
    def _():
        hq_ref[...] = jnp.zeros_like(hq_ref)
        hk_ref[...] = jnp.zeros_like(hk_ref)
        c_ref[...] = jnp.zeros_like(c_ref)
        n_ref[...] = jnp.zeros_like(n_ref)
        m_ref[...] = jnp.zeros_like(m_ref)

    vec = vec_ref[...]

    def conv(x_ref, buf_ref, w_ref, bias):
        x = x_ref[...]
        w = w_ref[...]
        buf_ref[8:8 + L, :] = x
        acc = x * w[3:4] + bias
        for s in (1, 2, 3):
            acc = acc + buf_ref[8 - s:8 - s + L, :] * w[3 - s:4 - s]
        buf_ref[0:8, :] = x[L - 8:L]
        return acc * _sigmoid(acc)

    q = conv(q_ref, hq_ref, cwq_ref, vec[0:1])
    k = conv(k_ref, hk_ref, cwk_ref, vec[1:2]) * (ML_HEAD ** -0.5)
    v = v_ref[...]
    og = _sigmoid(o_ref[...])

    g = g_ref[...] + gb_ref[0:1]
    lane = lax.broadcasted_iota(jnp.int32, (L, 128), 1)
    tri_b = lax.broadcasted_iota(jnp.int32, (L, L), 0) >= lax.broadcasted_iota(jnp.int32, (L, L), 1)
    cum = _dot_exact_lhs(tri_b.astype(BF16), _log_sigmoid(g))
    comb_t = jnp.where(lane < ML_HEADS, g, cum).T

    H = range(ML_HEADS)
    sls = [slice(128 * h, 128 * h + 128) for h in H]
    qs = [q[:, sl] for sl in sls]
    ks = [k[:, sl] for sl in sls]
    vs = [v[:, sl] for sl in sls]
    qk = [_dot_nt(qs[h], ks[h]) for h in H]
    c_old = [c_ref[h] for h in H]
    n_old = [n_ref[h:h + 1, :] for h in H]
    m_prev = [m_ref[h:h + 1, 0:1] for h in H]
    qc = [_dot_nt(qs[h], c_old[h]) for h in H]
    b_col = [cum[:, ML_HEADS + h:ML_HEADS + h + 1] for h in H]
    ig_col = [g[:, h:h + 1] for h in H]
    dm = [jnp.where(tri_b, b_col[h] - comb_t[ML_HEADS + h:ML_HEADS + h + 1, :] + comb_t[h:h + 1, :], -jnp.inf)
          for h in H]
    inter = [b_col[h] + m_prev[h] for h in H]
    mj = [jnp.maximum(inter[h], jnp.max(dm[h], axis=-1, keepdims=True)) for h in H]
    s = [qk[h] * jnp.exp(dm[h] - mj[h]) for h in H]
    iw = [jnp.exp(inter[h] - mj[h]) for h in H]
    num = [_dot(s[h], vs[h]) + iw[h] * qc[h] for h in H]
    den = [jnp.sum(s[h], axis=-1, keepdims=True) + iw[h] * jnp.sum(qs[h] * n_old[h], axis=-1, keepdims=True)
           for h in H]
    hh = [num[h] / jnp.maximum(jnp.abs(den[h]), jnp.exp(-mj[h])) for h in H]

    b_last = [b_col[h][L - 1:L, :] for h in H]
    g_col = [b_last[h] - b_col[h] + ig_col[h] for h in H]
    m_new = [jnp.maximum(b_last[h] + m_prev[h], jnp.max(g_col[h], axis=0, keepdims=True)) for h in H]
    a_old = [jnp.exp(b_last[h] + m_prev[h] - m_new[h]) for h in H]
    wg = [jnp.exp(g_col[h] - m_new[h]) for h in H]
    kv = [_dot_tn(vs[h] * wg[h], ks[h]) for h in H]
    outs = []
    for h in H:
        c_ref[h] = a_old[h] * c_old[h] + kv[h]
        n_ref[h:h + 1, :] = a_old[h] * n_old[h] + jnp.sum(ks[h] * wg[h], axis=0, keepdims=True)
        m_ref[h:h + 1, :] = jnp.broadcast_to(m_new[h], (1, 128))
        hn = hh[h] * lax.rsqrt(jnp.mean(hh[h] * hh[h], axis=-1, keepdims=True) + NORM_EPS) * vec[2:3, sls[h]]
        outs.append(og[:, sls[h]] * hn)
    y_ref[...] = jnp.concatenate(outs, axis=1).astype(BF16)


def _ml(p_all, cwq, cwk, vec, gb, batch, seq):
    n = batch * seq
    nc = seq // ML_CHUNK

    def rows(c):
        return lambda b, i: (b * nc + i, c)

    const = lambda b, i: (0, 0)
    base = COL_ML // 512
    return pl.pallas_call(
        _ml_kernel,
        grid=(batch, nc),
        in_specs=[pl.BlockSpec((ML_CHUNK, 512), rows(base)),
                  pl.BlockSpec((ML_CHUNK, 512), rows(base + 1)),
                  pl.BlockSpec((ML_CHUNK, 512), rows(base + 2)),
                  pl.BlockSpec((ML_CHUNK, 512), rows(base + 3)),
                  pl.BlockSpec((ML_CHUNK, 128), rows(COL_ML_GATE // 128)),
                  pl.BlockSpec((8, 512), const), pl.BlockSpec((8, 512), const),
                  pl.BlockSpec((8, 512), const), pl.BlockSpec((8, 128), const)],
        out_specs=pl.BlockSpec((ML_CHUNK, 512), rows(0)),
        out_shape=jax.ShapeDtypeStruct((n, 512), BF16),
        scratch_shapes=[pltpu.VMEM((ML_CHUNK + 8, 512), F32), pltpu.VMEM((ML_CHUNK + 8, 512), F32),
                        pltpu.VMEM((ML_HEADS, 128, 128), F32), pltpu.VMEM((8, 128), F32),
                        pltpu.VMEM((8, 128), F32)],
        compiler_params=_params("arbitrary", "arbitrary"),
        name="mlstm",
    )(p_all, p_all, p_all, p_all, p_all, cwq, cwk, vec, gb)


def _rope128(x, cos_t, sin_t, swap):
    return x * cos_t + jnp.dot(x.astype(BF16), swap, preferred_element_type=F32) * sin_t


def _swap_matrix(half, seg):
    lane = jnp.arange(128)
    pos = lane % seg
    src = jnp.where(pos < half, lane + half, jnp.where(pos < 2 * half, lane - half, lane))
    return (lane[:, None] == src[None, :]).astype(BF16)


def _mla_prep_kernel(p_ref, cos_ref, sin_ref, vec_ref, wq_ref, wkv_ref, swap_ref, q_ref, k_ref, vt_ref):
    p = p_ref[...]
    vec = vec_ref[...]
    cos_t = cos_ref[...]
    sin_t = sin_ref[...]
    swap = swap_ref[...]
    q_lat = p[:, 0:MLA_Q_LORA]
    kv_lat = p[:, MLA_Q_LORA:MLA_Q_LORA + MLA_KV_LORA]
    k_pe = p[:, 640:768]
    qn = q_lat * lax.rsqrt(jnp.mean(q_lat * q_lat, axis=-1, keepdims=True) + NORM_EPS) * vec[0:1, 0:MLA_Q_LORA]
    kvn = kv_lat * lax.rsqrt(jnp.mean(kv_lat * kv_lat, axis=-1, keepdims=True) + NORM_EPS) * vec[1:2, 0:MLA_KV_LORA]
    q4 = _dot(qn, wq_ref[...])
    kv4 = _dot(kvn, wkv_ref[...])
    pe_ss = jnp.sum(k_pe * k_pe, axis=-1, keepdims=True)
    pe_rot = _rope128(k_pe * vec[4:5, 0:128], cos_t, sin_t, swap)
    qs, ks = [], []
    for h in range(MLA_HEADS):
        qh = q4[:, MLA_PAD * h:MLA_PAD * (h + 1)]
        inv = lax.rsqrt(jnp.sum(qh * qh, axis=-1, keepdims=True) * (1.0 / MLA_QK) + NORM_EPS)
        qh = qh * inv * vec[2:3, MLA_PAD * h:MLA_PAD * (h + 1)]
        qs += [qh[:, 0:128], _rope128(qh[:, 128:256], cos_t, sin_t, swap)]
        kn = kv4[:, 128 * h:128 * (h + 1)]
        inv = lax.rsqrt((jnp.sum(kn * kn, axis=-1, keepdims=True) + pe_ss) * (1.0 / MLA_QK) + NORM_EPS)
        ks += [kn * inv * vec[3:4, 0:128], pe_rot * inv]
    q_ref[...] = jnp.concatenate(qs, axis=1).astype(BF16)
    k_ref[...] = jnp.concatenate(ks, axis=1).astype(BF16)
    _store_vt(vt_ref, kv4[:, 512:1024])


def _store_vt(vt_ref, v):
    heads, _, rows = vt_ref.shape
    eye = (lax.broadcasted_iota(jnp.int32, (128, 128), 0)
           == lax.broadcasted_iota(jnp.int32, (128, 128), 1)).astype(BF16)
    ones = jnp.ones((DV_AUG - 128, rows), BF16)
    for h in range(heads):
        vt = _dot_nt(eye, v[:, 128 * h:128 * (h + 1)]).astype(BF16)
        vt_ref[h] = jnp.concatenate([vt, ones], axis=0)


def _vt_spec(heads, nt):
    return pl.BlockSpec((None, heads, None, DV_AUG, ATT_BLOCK), lambda i: (i // nt, 0, i % nt, 0, 0))


def _mla_prep(p_all, cos_t, sin_t, vec, wq, wkv, batch, seq):
    n = batch * seq
    tr = ATT_BLOCK
    nt = seq // tr
    const = lambda i: (0, 0)
    return pl.pallas_call(
        _mla_prep_kernel,
        grid=(n // tr,),
        in_specs=[pl.BlockSpec((tr, 768), lambda i: (i, COL_MLA // 768)),
                  pl.BlockSpec((tr, 128), lambda i: (i % nt, 0)),
                  pl.BlockSpec((tr, 128), lambda i: (i % nt, 0)),
                  pl.BlockSpec((8, 1024), const),
                  pl.BlockSpec((MLA_Q_LORA, 1024), const),
                  pl.BlockSpec((MLA_KV_LORA, 1024), const),
                  pl.BlockSpec((128, 128), const)],
        out_specs=[pl.BlockSpec((tr, 1024), lambda i: (i, 0)),
                   pl.BlockSpec((tr, 1024), lambda i: (i, 0)),
                   _vt_spec(MLA_HEADS, nt)],
        out_shape=[jax.ShapeDtypeStruct((n, 1024), BF16), jax.ShapeDtypeStruct((n, 1024), BF16),
                   jax.ShapeDtypeStruct((batch, MLA_HEADS, nt, DV_AUG, tr), BF16)],
        compiler_params=_params("arbitrary"),
        name="mla_prep",
    )(p_all, cos_t, sin_t, vec, wq, wkv, _swap_matrix(MLA_ROPE // 2, 128))


def _attn_block_t(kbs, vts, units, masked):
    sub = ATT_SUB
    s, cols = [], []
    for q, _, _, c0, h in units:
        keys = c0 + sub if masked else kbs[h].shape[0]
        st = _dot_nt(kbs[h][0:keys], q)
        if masked:
            keep = (lax.broadcasted_iota(jnp.int32, st.shape, 0)
                    <= lax.broadcasted_iota(jnp.int32, st.shape, 1) + c0)
            st = jnp.where(keep, st, -jnp.inf)
        s.append(st)
        cols.append(slice(c0, c0 + sub))
    idx = range(len(units))
    m_old = [units[i][1][:, cols[i]] for i in idx]
    m_new = [jnp.maximum(m_old[i], jnp.max(s[i], axis=0, keepdims=True)) for i in idx]
    p = [jnp.exp2(s[i] - m_new[i]).astype(BF16) for i in idx]
    alpha = [jnp.exp2(m_old[i] - m_new[i]) for i in idx]
    for i in idx:
        _, m_ref, acc_ref, _, h = units[i]
        pv = jnp.dot(vts[h][:, 0:s[i].shape[0]], p[i], preferred_element_type=F32)
        acc_ref[:, cols[i]] = alpha[i] * acc_ref[:, cols[i]] + pv
        m_ref[:, cols[i]] = m_new[i]


def _init_stats(m_ref, acc_ref):
    m_ref[...] = jnp.full_like(m_ref, -jnp.inf)
    acc_ref[...] = jnp.zeros_like(acc_ref)


def _attn_sweep(i, k_ref, vt_ref, dk, units):
    tb = ATT_BLOCK
    heads = vt_ref.shape[0]

    def block(j, masked):
        start = pl.multiple_of(j * tb, tb)
        kbs = [k_ref[pl.ds(start, tb), dk * h:dk * (h + 1)] for h in range(heads)]
        _attn_block_t(kbs, [vt_ref[h, j] for h in range(heads)], units, masked)

    def body(j, carry):
        block(j, False)
        return carry

    lax.fori_loop(0, i, body, 0)
    block(i, True)


def _attn_result(acc_ref):
    acc = acc_ref[...]
    return acc[0:128] / acc[128:129]


def _attn_specs(dk, nq, seq, heads):
    tb = ATT_BLOCK
    return [pl.BlockSpec((tb, dk * heads), lambda b, h, i: (b * nq + i, h)),
            pl.BlockSpec((seq, dk * heads), lambda b, h, i: (b, h)),
            pl.BlockSpec((None, heads, nq, DV_AUG, tb), lambda b, h, i: (b, h, 0, 0, 0))]


def _mla_attn_kernel(q_ref, k_ref, vt_ref, o_ref, *stats):
    heads = vt_ref.shape[0]
    units = []
    for h in range(heads):
        m_ref, acc_ref = stats[2 * h:2 * h + 2]
        _init_stats(m_ref, acc_ref)
        units += [(q_ref[c0:c0 + ATT_SUB, MLA_PAD * h:MLA_PAD * (h + 1)], m_ref, acc_ref, c0, h)
                  for c0 in range(0, ATT_BLOCK, ATT_SUB)]
    _attn_sweep(pl.program_id(2), k_ref, vt_ref, MLA_PAD, units)
    o_ref[...] = jnp.concatenate([_attn_result(stats[2 * h + 1]).T for h in range(heads)], axis=1).astype(BF16)


def _mla_attn(q, k, vt, batch, seq):
    n = batch * seq
    tb = ATT_BLOCK
    nq = seq // tb
    return pl.pallas_call(
        _mla_attn_kernel,
        grid=(batch, MLA_HEADS // MLA_ATT_HEADS, nq),
        in_specs=_attn_specs(MLA_PAD, nq, seq, MLA_ATT_HEADS),
        out_specs=pl.BlockSpec((tb, 128 * MLA_ATT_HEADS), lambda b, h, i: (b * nq + i, h)),
        out_shape=jax.ShapeDtypeStruct((n, 512), BF16),
        scratch_shapes=[pltpu.VMEM((1, tb), F32), pltpu.VMEM((DV_AUG, tb), F32)] * MLA_ATT_HEADS,
        compiler_params=_params("arbitrary", "arbitrary", "arbitrary"),
        name="mla_attn",
    )(q, k, vt)


def _da_prep_kernel(q_ref, k_ref, v_ref, cos_ref, sin_ref, vec_ref, ones_ref, swap_ref, qo_ref, ko_ref, vo_ref):
    vec = vec_ref[...]
    ones = ones_ref[...]
    cos_t = cos_ref[...]
    sin_t = sin_ref[...]
    swap = swap_ref[...]

    def norm_rope(x, gain):
        ss = _dot(x * x, ones) * (1.0 / DA_QK)
        x = x * lax.rsqrt(ss + NORM_EPS) * gain
        parts = [_rope128(x[:, 128 * c:128 * (c + 1)], cos_t, sin_t, swap) for c in range(4)]
        return jnp.concatenate(parts, axis=1)

    qo_ref[...] = norm_rope(q_ref[...], vec[0:1]).astype(BF16)
    ko_ref[...] = norm_rope(k_ref[...], vec[1:2]).astype(BF16)
    _store_vt(vo_ref, v_ref[...])


def _da_prep(p_all, cos_t, sin_t, vec, ones64, batch, seq):
    n = batch * seq
    tr = ATT_BLOCK
    nt = seq // tr
    const = lambda i: (0, 0)
    base = COL_DA // 512
    out = jax.ShapeDtypeStruct((n, 512), BF16)
    return pl.pallas_call(
        _da_prep_kernel,
        grid=(n // tr,),
        in_specs=[pl.BlockSpec((tr, 512), lambda i: (i, base)),
                  pl.BlockSpec((tr, 512), lambda i: (i, base + 1)),
                  pl.BlockSpec((tr, 512), lambda i: (i, base + 2)),
                  pl.BlockSpec((tr, 128), lambda i: (i % nt, 0)),
                  pl.BlockSpec((tr, 128), lambda i: (i % nt, 0)),
                  pl.BlockSpec((8, 512), const),
                  pl.BlockSpec((512, 512), const),
                  pl.BlockSpec((128, 128), const)],
        out_specs=[pl.BlockSpec((tr, 512), lambda i: (i, 0))] * 2 + [_vt_spec(DA_HEADS, nt)],
        out_shape=[out, out, jax.ShapeDtypeStruct((batch, DA_HEADS, nt, DV_AUG, tr), BF16)],
        compiler_params=_params("arbitrary"),
        name="da_prep",
    )(p_all, p_all, p_all, cos_t, sin_t, vec, ones64, _swap_matrix(DA_ROT // 2, DA_QK))


def _da_attn_kernel(lam_init, q_ref, k_ref, vt_ref, lam_ref, g_ref, o_ref, *stats):
    heads = vt_ref.shape[0]
    lane = lax.broadcasted_iota(jnp.int32, (ATT_BLOCK, 128), 1)
    units = []
    for h in range(heads):
        q = q_ref[:, 128 * h:128 * (h + 1)]
        for mp in range(2):
            qm = jnp.where((lane < DA_QK) == (mp == 0), q, jnp.zeros_like(q))
            m_ref, acc_ref = stats[4 * h + 2 * mp:4 * h + 2 * mp + 2]
            _init_stats(m_ref, acc_ref)
            units += [(qm[c0:c0 + ATT_SUB], m_ref, acc_ref, c0, h) for c0 in range(0, ATT_BLOCK, ATT_SUB)]
    _attn_sweep(pl.program_id(2), k_ref, vt_ref, 128, units)

    lv = lam_ref[...]
    lam = (jnp.exp(jnp.sum(lv[0:1] * lv[1:2], axis=-1, keepdims=True))
           - jnp.exp(jnp.sum(lv[2:3] * lv[3:4], axis=-1, keepdims=True)) + lam_init)
    outs = []
    for h in range(heads):
        o = (_attn_result(stats[4 * h + 1]) - lam * _attn_result(stats[4 * h + 3])).T
        o = o * lax.rsqrt(jnp.mean(o * o, axis=-1, keepdims=True) + NORM_EPS) * g_ref[0:1]
        outs.append(o * (1.0 - lam_init))
    o_ref[...] = jnp.concatenate(outs, axis=1).astype(BF16)


def _da_attn(q, k, vt, lam_vec, sub_g, lam_init, batch, seq):
    n = batch * seq
    tb = ATT_BLOCK
    nq = seq // tb
    const = lambda b, h, i: (0, 0)
    row = pltpu.VMEM((1, tb), F32)
    acc = pltpu.VMEM((DV_AUG, tb), F32)
    return pl.pallas_call(
        functools.partial(_da_attn_kernel, lam_init),
        grid=(batch, DA_HEADS // DA_ATT_HEADS, nq),
        in_specs=_attn_specs(128, nq, seq, DA_ATT_HEADS)
        + [pl.BlockSpec((8, 128), const), pl.BlockSpec((8, 128), const)],
        out_specs=pl.BlockSpec((tb, 128 * DA_ATT_HEADS), lambda b, h, i: (b * nq + i, h)),
        out_shape=jax.ShapeDtypeStruct((n, 512), BF16),
        scratch_shapes=[row, acc] * (2 * DA_ATT_HEADS),
        compiler_params=_params("arbitrary", "arbitrary", "arbitrary"),
        name="da_attn",
    )(q, k, vt, lam_vec, sub_g)


def _outproj_kernel(ya_ref, yb_ref, yc_ref, yd_ref, z_ref, beta_ref, x_ref, w_ref, *rest):
    y_refs = (ya_ref, yb_ref, yc_ref, yd_ref)
    acc = x_ref[...]
    for c in range(4):
        sl = slice(GROUP_W * c, GROUP_W * (c + 1))
        z = z_ref[:, sl]
        gated = (y_refs[c][...].astype(F32) * beta_ref[0:1, sl] * (z * _sigmoid(z))).astype(BF16)
        acc = acc + jnp.dot(gated, w_ref[sl, :], preferred_element_type=F32)
    if len(rest) == 1:
        rest[0][...] = acc
    else:
        gn_ref, o_ref, h_ref = rest
        o_ref[...] = acc
        ms = jnp.mean(acc * acc, axis=-1, keepdims=True)
        h_ref[...] = (acc * lax.rsqrt(ms + NORM_EPS) * gn_ref[0:1]).astype(BF16)


def _outproj(ys, p_all, beta_row, x2, w, next_gain=None, tm=512):
    n = x2.shape[0]
    yspec = pl.BlockSpec((tm, 512), lambda i: (i, 0))
    row = pl.BlockSpec((tm, D_MODEL), lambda i: (i, 0))
    resident = dict(pipeline_mode=pl.Buffered(1))
    in_specs = ([yspec] * 4 + [pl.BlockSpec((tm, D_MODEL), lambda i: (i, COL_Z // D_MODEL))]
                + [pl.BlockSpec((8, D_MODEL), lambda i: (0, 0), **resident), row,
                   pl.BlockSpec((D_MODEL, D_MODEL), lambda i: (0, 0), **resident)])
    args = [*ys, p_all, beta_row, x2, w]
    out_specs, out_shape = row, jax.ShapeDtypeStruct((n, D_MODEL), F32)
    if next_gain is not None:
        in_specs.append(pl.BlockSpec((8, D_MODEL), lambda i: (0, 0), **resident))
        args.append(next_gain)
        out_specs = [row, row]
        out_shape = [out_shape, jax.ShapeDtypeStruct((n, D_MODEL), BF16)]
    return pl.pallas_call(
        _outproj_kernel,
        grid=(n // tm,),
        in_specs=in_specs,
        out_specs=out_specs,
        out_shape=out_shape,
        compiler_params=_params("arbitrary"),
        name="outproj",
    )(*args)


def _rows(rows, width, n_rows=8):
    padded = [jnp.pad(r.astype(F32), (0, width - r.shape[0])) for r in rows]
    padded += [jnp.zeros((width,), F32)] * (n_rows - len(rows))
    return jnp.stack(padded)


def _pad_to(a, shape):
    return jnp.pad(a, [(0, s - d) for s, d in zip(shape, a.shape)])


def _arrange_w_in(w):
    w = w.astype(BF16)
    zeros = lambda c: jnp.zeros((D_MODEL, c), BF16)
    return jnp.concatenate([
        w[:, 5704:6408], zeros(64),
        w[:, 3584:3648], zeros(64),
        w[:, 5696:5704], zeros(120),
        w[:, 2048:3584],
        w[:, 3648:5696],
        w[:, 6408:7944],
        w[:, 0:2048],
    ], axis=1).reshape(D_MODEL, N_PROJ // INPROJ_TN, INPROJ_TN).transpose(1, 0, 2)


def _rope_tables(seq, rot, seg, fill):
    half = rot // 2
    inv = ROPE_THETA ** (-jnp.arange(0, rot, 2, dtype=F32) / rot)
    ang = jnp.arange(seq, dtype=F32)[:, None] * inv[None, :]
    cos, sin = jnp.cos(ang), jnp.sin(ang)
    rest = seg - rot
    cos_s = jnp.concatenate([cos, cos, jnp.full((seq, rest), fill, F32)], axis=1)
    sin_s = jnp.concatenate([-sin, sin, jnp.zeros((seq, rest), F32)], axis=1)
    reps = 128 // seg
    return jnp.tile(cos_s, (1, reps)), jnp.tile(sin_s, (1, reps))


def kernel(x, norm_g, w_in, w_out, branch_beta, rw_mu, rw_w0, rw_w_up, rw_a0, rw_a_up, rw_v0, rw_v_dn, rw_v_up, rw_k_k, rw_k_a, rw_r_k, rw_ln_w, rw_ln_b, ml_conv_w, ml_conv_b, ml_i_b, ml_f_b, ml_norm_g, mla_q_norm_g, mla_kv_norm_g, mla_w_q_b, mla_w_kv_b, mla_q_g, mla_k_g, da_q_g, da_k_g, da_lq1, da_lk1, da_lq2, da_lk2, da_sub_g):
    batch, seq, _ = x.shape
    depth = w_in.shape[0]
    n = batch * seq
    x2 = x.reshape(n, D_MODEL)

    seg_id = jnp.arange(GROUP_W) // RW_HEAD
    ones64 = (seg_id[:, None] == seg_id[None, :]).astype(BF16)
    mla_cos, mla_sin = _rope_tables(seq, MLA_ROPE, 128, 0.0)
    da_cos, da_sin = _rope_tables(seq, DA_ROT, DA_QK, 1.0)

    v_first = None
    h_next = None
    for l in range(depth):
        p_all = _inproj(x2, norm_g[l][None, :], _arrange_w_in(w_in[l]), h_next)

        mu = rw_mu[l]
        vec = _rows([mu[0:512], mu[512:1024], mu[1024:1536], rw_w0[l], rw_a0[l], rw_k_k[l], rw_k_a[l]]
                    + ([rw_v0[l - 1]] if l > 0 else []), 512)
        mulo = _rows([mu[1536:1600]], 128)
        wup = _pad_to(rw_w_up[l], (128, 512)).astype(BF16)
        aup = jnp.zeros((128, 512), F32).at[32:64].set(rw_a_up[l]).astype(BF16)
        if l > 0:
            vdn = _pad_to(rw_v_dn[l - 1], (512, 128)).astype(BF16)
            vup = _pad_to(rw_v_up[l - 1], (128, 512)).astype(BF16)
        else:
            vdn = vup = None
        par = _rows([rw_r_k[l].reshape(-1), rw_ln_w[l], rw_ln_b[l]], 512)
        y_rw, v_first = _rw(p_all, v_first, vec, mulo, wup, aup, vdn, vup, par, ones64, batch, seq)

        cw = ml_conv_w[l]
        cb = ml_conv_b[l]
        ml_vec = _rows([cb[0:512], cb[512:1024], ml_norm_g[l]], 512)
        gb = _rows([jnp.concatenate([ml_i_b[l], ml_f_b[l]])], 128)
        y_ml = _ml(p_all, _pad_to(cw[:, 0:512], (8, 512)), _pad_to(cw[:, 512:1024], (8, 512)),
                   ml_vec, gb, batch, seq)

        scale = MLA_QK ** -0.5 * LOG2E
        qg = jnp.tile(jnp.concatenate([mla_q_g[l] * scale, jnp.zeros((64,), F32)]), MLA_HEADS)
        mla_vec = _rows([mla_q_norm_g[l], mla_kv_norm_g[l], qg, mla_k_g[l][0:128], mla_k_g[l][128:192]], 1024)
        wq = _pad_to(mla_w_q_b[l].reshape(MLA_Q_LORA, MLA_HEADS, MLA_QK),
                     (MLA_Q_LORA, MLA_HEADS, MLA_PAD)).reshape(MLA_Q_LORA, 1024).astype(BF16)
        wkv = mla_w_kv_b[l].reshape(MLA_KV_LORA, MLA_HEADS, 2, 128).transpose(0, 2, 1, 3)
        wkv = wkv.reshape(MLA_KV_LORA, 1024).astype(BF16)
        mq, mk, mv = _mla_prep(p_all, mla_cos, mla_sin, mla_vec, wq, wkv, batch, seq)
        y_mla = _mla_attn(mq, mk, mv, batch, seq)

        da_vec = _rows([jnp.tile(da_q_g[l], 8) * (DA_QK ** -0.5 * LOG2E), jnp.tile(da_k_g[l], 8)], 512)
        dq, dk, dv = _da_prep(p_all, da_cos, da_sin, da_vec, ones64, batch, seq)
        lam_vec = _rows([da_lq1[l], da_lk1[l], da_lq2[l], da_lk2[l]], 128)
        lam_init = 0.8 - 0.6 * math.exp(-0.3 * l)
        y_da = _da_attn(dq, dk, dv, lam_vec, _rows([da_sub_g[l]], 128), lam_init, batch, seq)

        beta_row = _rows([jnp.repeat(branch_beta[l], GROUP_W)], D_MODEL)
        ys = (y_rw, y_ml, y_mla, y_da)
        if l + 1 < depth:
            x2, h_next = _outproj(ys, p_all, beta_row, x2, w_out[l].astype(BF16), _rows([norm_g[l + 1]], D_MODEL))
        else:
            x2 = _outproj(ys, p_all, beta_row, x2, w_out[l].astype(BF16))
    return x2.reshape(batch, seq, D_MODEL)
```
